```python
import math
import jax, jax.numpy as jnp
from jax import lax
import numpy as np

D_MODEL = 4096
BATCH = 4
SEQ = 2048
DEPTH = 2
DEC_BATCH = 8
DEC_SEQ = 8
PAST_LEN = 16384
PAGE_SIZE = 128

HEAD_DIM = 128
ATTN_WIDTH = D_MODEL // 2
ATTN_HEADS = ATTN_WIDTH // HEAD_DIM
CONV_CH = D_MODEL - ATTN_WIDTH
CONV_WIDTH = 31
DILATED_PATTERNS = ((128, 1), (512, 4), (2048, 16))
WINDOW_MAX = 2048
NUM_BUCKETS = 32
MAX_EXACT = 16
MAX_DISTANCE = 2048
HGRN_DK = 128
HGRN_HEADS = D_MODEL // HGRN_DK
HGRN_DV = D_MODEL // HGRN_HEADS
HGRN_FDIM = HGRN_HEADS * HGRN_DK
HGRN_CHUNK = 32
N_EXPERTS = 32
N_GROUPS = 8
GROUP_SIZE = N_EXPERTS // N_GROUPS
TOP_K = 2
D_EXPERT = 1024
MOE_BLOCK = 128
N_AB_LAYERS = (DEPTH + 1) // 2
N_C_LAYERS = DEPTH // 2
ALPHA = (2.0 * DEPTH) ** 0.25
BETA = (8.0 * DEPTH) ** -0.25
EPS = 1e-5
AB_IN = 3 * ATTN_WIDTH + 2 * CONV_CH
C_IN = 2 * HGRN_FDIM + 2 * D_MODEL
F32 = jnp.float32

kernel_name = "dilated_conv_hgrn2_moe_decoder_step"


def layer_norm(x, g, b):
    xf = x.astype(F32)
    mu = jnp.mean(xf, -1, keepdims=True)
    var = jnp.mean(jnp.square(xf - mu), -1, keepdims=True)
    return ((xf - mu) * lax.rsqrt(var + EPS) * g.astype(F32) + b.astype(F32)).astype(x.dtype)


def rms_norm(x, g):
    xf = x.astype(F32)
    return xf * lax.rsqrt(jnp.mean(jnp.square(xf), -1, keepdims=True) + EPS) * g.astype(F32)


def rel_bucket(dist):
    dist = dist.astype(jnp.int32)
    d = jnp.maximum(dist, 1).astype(F32)
    large = MAX_EXACT + (jnp.log(d / MAX_EXACT) / math.log(MAX_DISTANCE / MAX_EXACT)
                         * (NUM_BUCKETS - MAX_EXACT)).astype(jnp.int32)
    large = jnp.minimum(large, NUM_BUCKETS - 1)
    return jnp.where(dist < MAX_EXACT, dist, large)


def _masked_softmax_stats(logits, mask):
    logits = jnp.where(mask, logits, -jnp.inf)
    m = jnp.max(logits, -1, keepdims=True)
    p = jnp.exp(logits - m)
    s = jnp.sum(p, -1, keepdims=True)
    return p / s, (m + jnp.log(s))[..., 0]


def _dilated_group_prompt(q, k, v, rel_bias, window, dilation):
    B, S, H, hd = q.shape
    span = window // dilation
    blk = span
    sr = S // dilation
    nb = -(-sr // blk)
    pad = nb * blk - sr

    def split(t):
        t = t.reshape(B, sr, dilation, H, hd).transpose(0, 2, 1, 3, 4)
        t = jnp.pad(t, ((0, 0), (0, 0), (0, pad), (0, 0), (0, 0)))
        return t.reshape(B, dilation, nb, blk, H, hd)

    def band(t):
        prev = jnp.pad(t, ((0, 0), (0, 0), (1, 0), (0, 0), (0, 0), (0, 0)))[:, :, :-1]
        return jnp.concatenate([prev, t], axis=3)

    qb = split(q)
    kb = band(split(k))
    vb = band(split(v))
    logits = jnp.einsum('brnqhd,brnkhd->brnhqk', qb, kb) * (hd ** -0.5)
    qi = jnp.arange(blk)[:, None]
    kj = jnp.arange(2 * blk)[None, :]
    steps = blk + qi - kj
    in_band = (steps >= 0) & (steps <= span)
    bias = rel_bias[rel_bucket(jnp.clip(steps, 0, span) * dilation)]
    logits = logits + jnp.transpose(bias, (2, 0, 1)).astype(F32)
    exists = (jnp.arange(nb)[:, None, None] > 0) | (kj >= blk)[None]
    mask = (in_band[None] & exists)[None, None, :, None]
    p, lse = _masked_softmax_stats(logits, mask)
    o = jnp.einsum('brnhqk,brnkhd->brnqhd', p, vb)
    o = o.reshape(B, dilation, nb * blk, H, hd)[:, :, :sr].transpose(0, 2, 1, 3, 4).reshape(B, S, H, hd)
    lse = lse.transpose(0, 1, 2, 4, 3).reshape(B, dilation, nb * blk, H)[:, :, :sr]
    lse = lse.transpose(0, 2, 1, 3).reshape(B, S, H)
    return o, lse


def _dilated_group_sample(q, k_ext, v_ext, rel_bias, window, dilation, n_past):
    T, hd = q.shape[1], q.shape[-1]
    span = window // dilation
    steps = jnp.arange(span + 1)
    idx = (n_past + jnp.arange(T))[:, None] - steps[None, :] * dilation
    valid = idx >= 0
    safe = jnp.maximum(idx, 0)
    kg = k_ext[:, safe]
    vg = v_ext[:, safe]
    logits = jnp.einsum('bthd,btjhd->bhtj', q, kg) * (hd ** -0.5)
    bias = rel_bias[rel_bucket(steps * dilation)].astype(F32)
    logits = logits + bias.T[None, :, None, :]
    p, lse = _masked_softmax_stats(logits, valid[None, None])
    o = jnp.einsum('bhtj,btjhd->bthd', p, vg)
    return o, lse.transpose(0, 2, 1)


def _combine_patterns(outs, lses):
    w = jax.nn.softmax(jnp.stack(lses, 0), axis=0)
    return jnp.sum(w[..., None] * jnp.stack(outs, 0), axis=0)


def _conv_tail(u_ext, conv_w, conv_b, g, b):
    y = lax.conv_general_dilated(u_ext, conv_w[:, None, :].astype(u_ext.dtype), (1,), 'VALID',
                                 dimension_numbers=('NWC', 'WIO', 'NWC'), feature_group_count=CONV_CH)
    y = y + conv_b.astype(y.dtype)
    return jax.nn.silu(layer_norm(y, g, b).astype(F32))


def _ab_project(x, w_in):
    b, t, _ = x.shape
    proj = x @ w_in
    q, k, v, u = jnp.split(proj, [ATTN_WIDTH, 2 * ATTN_WIDTH, 3 * ATTN_WIDTH], axis=-1)
    heads = lambda z: z.reshape(b, t, ATTN_HEADS, HEAD_DIM)
    glu = u[..., :CONV_CH] * jax.nn.sigmoid(u[..., CONV_CH:])
    return heads(q), heads(k), heads(v), glu


def ab_mixer_prompt(x, w_in, rel_bias, conv_w, conv_b, cn_g, cn_b, w_out):
    b, s, _ = x.shape
    q, k, v, glu = _ab_project(x, w_in)
    qf, kf, vf = q.astype(F32), k.astype(F32), v.astype(F32)
    outs, lses = zip(*[_dilated_group_prompt(qf, kf, vf, rel_bias, w, r) for w, r in DILATED_PATTERNS])
    attn = _combine_patterns(outs, lses).reshape(b, s, ATTN_WIDTH)
    u_ext = jnp.pad(glu, ((0, 0), (CONV_WIDTH - 1, 0), (0, 0)))
    conv = _conv_tail(u_ext, conv_w, conv_b, cn_g, cn_b)
    y = jnp.concatenate([attn, conv], -1).astype(x.dtype) @ w_out
    keep = min(WINDOW_MAX, s)
    return y, k[:, s - keep:], v[:, s - keep:], glu[:, s - (CONV_WIDTH - 1):]


def ab_mixer_sample(x, win_k, win_v, conv_state, w_in, rel_bias, conv_w, conv_b, cn_g, cn_b, w_out):
    b, t, _ = x.shape
    n_past = win_k.shape[1]
    q, k, v, glu = _ab_project(x, w_in)
    k_ext = jnp.concatenate([win_k.astype(F32), k.astype(F32)], 1)
    v_ext = jnp.concatenate([win_v.astype(F32), v.astype(F32)], 1)
    qf = q.astype(F32)
    outs, lses = zip(*[_dilated_group_sample(qf, k_ext, v_ext, rel_bias, w, r, n_past)
                       for w, r in DILATED_PATTERNS])
    attn = _combine_patterns(outs, lses).reshape(b, t, ATTN_WIDTH)
    u_ext = jnp.concatenate([conv_state.astype(glu.dtype), glu], 1)
    conv = _conv_tail(u_ext, conv_w, conv_b, cn_g, cn_b)
    y = jnp.concatenate([attn, conv], -1).astype(x.dtype) @ w_out
    return y, k, v, u_ext[:, -(CONV_WIDTH - 1):]


def _gla_chunked(q, k, v, log_f, s0):
    b, L, H, dk = q.shape
    dv = v.shape[-1]
    c = min(HGRN_CHUNK, L)
    n = -(-L // c)
    pad = n * c - L

    def chunks(z):
        z = jnp.pad(z, ((0, 0), (0, pad), (0, 0), (0, 0)))
        return z.reshape(b, n, c, H, z.shape[-1]).transpose(1, 0, 2, 3, 4)

    causal = jnp.tril(jnp.ones((c, c), bool))[None, :, :, None, None]

    def step(S, inp):
        qc, kc, vc, gc = inp
        cum = jnp.cumsum(gc, axis=1)
        decay = jnp.exp(jnp.where(causal, cum[:, :, None] - cum[:, None, :], -jnp.inf))
        scores = jnp.sum(qc[:, :, None] * decay * kc[:, None], axis=-1)
        o = (jnp.einsum('btsh,bshv->bthv', scores, vc)
             + jnp.einsum('bthc,bhcv->bthv', qc * jnp.exp(cum), S))
        last = cum[:, -1]
        S = (jnp.exp(last)[..., None] * S
             + jnp.einsum('bshc,bshv->bhcv', kc * jnp.exp(last[:, None] - cum), vc))
        return S, o

    s_fin, o = lax.scan(step, s0, (chunks(q), chunks(k), chunks(v), chunks(log_f)))
    o = o.transpose(1, 0, 2, 3, 4).reshape(b, n * c, H, dv)[:, :L]
    return o, s_fin


def hgrn2_mixer(x, s0, w_in, lower_bound, norm_g, w_out):
    b, L, _ = x.shape
    proj = (x @ w_in).astype(F32)
    q, f, i, g = jnp.split(proj, [HGRN_FDIM, 2 * HGRN_FDIM, 2 * HGRN_FDIM + D_MODEL], axis=-1)
    fgate = lower_bound + (1.0 - lower_bound) * jax.nn.sigmoid(f)
    hk = lambda z: z.reshape(b, L, HGRN_HEADS, HGRN_DK)
    o, s_fin = _gla_chunked(hk(jax.nn.silu(q)), hk(1.0 - fgate), i.reshape(b, L, HGRN_HEADS, HGRN_DV),
                            hk(jnp.log(fgate)), s0.astype(F32))
    o = rms_norm(o, norm_g).reshape(b, L, D_MODEL) * jax.nn.silu(g)
    return o.astype(x.dtype) @ w_out, s_fin


def moe_ffn(x, router_w, router_b, w_gate_up, w_down):
    n, d = x.shape
    logits = jnp.dot(x, router_w, preferred_element_type=F32) + router_b.astype(F32)
    probs = jax.nn.softmax(logits, -1).reshape(n, N_GROUPS, GROUP_SIZE)
    group_score = jnp.sum(lax.top_k(probs, TOP_K)[0], -1)
    gsel = jnp.argmax(group_score, -1)
    in_group = jnp.take_along_axis(probs, gsel[:, None, None], axis=1)[:, 0]
    top_p, top_i = lax.top_k(in_group, TOP_K)
    experts = (gsel[:, None] * GROUP_SIZE + top_i).reshape(-1).astype(jnp.int32)
    gates = (top_p / jnp.sum(top_p, -1, keepdims=True)).reshape(-1)
    tokens = jnp.repeat(jnp.arange(n, dtype=jnp.int32), TOP_K)
    n_assign = n * TOP_K
    order = jnp.argsort(experts)
    e_s, tok_s, g_s = experts[order], tokens[order], gates[order]
    counts = jnp.bincount(experts, length=N_EXPERTS)
    starts = jnp.cumsum(counts) - counts
    pcounts = ((counts + MOE_BLOCK - 1) // MOE_BLOCK) * MOE_BLOCK
    pends = jnp.cumsum(pcounts)
    pstarts = pends - pcounts
    dest = pstarts[e_s] + (jnp.arange(n_assign) - starts[e_s])
    n_blocks = -(-n_assign // MOE_BLOCK) + N_EXPERTS
    xs = jnp.zeros((n_blocks * MOE_BLOCK, d), x.dtype).at[dest].set(x[tok_s])
    block_expert = jnp.minimum(jnp.searchsorted(pends, jnp.arange(n_blocks) * MOE_BLOCK, side='right'),
                               N_EXPERTS - 1)

    def block_ffn(args):
        xb, e = args
        h = xb @ w_gate_up[e]
        a = jax.nn.silu(h[:, :D_EXPERT]) * h[:, D_EXPERT:]
        return a @ w_down[e]

    ys = lax.map(block_ffn, (xs.reshape(n_blocks, MOE_BLOCK, d), block_expert)).reshape(-1, d)
    out = jnp.zeros((n, d), F32).at[tok_s].add(g_s[:, None] * ys[dest].astype(F32))
    return out.astype(x.dtype)


def setup_inputs(seed: int = 0) -> dict:
    key = jax.random.key(seed)
    ks = jax.random.split(key, 24)

    def nrm(i, shape, scale):
        return jax.random.normal(ks[i], shape, jnp.float32) * scale

    win_buf = min(WINDOW_MAX, PAST_LEN)
    w_in_ab = nrm(6, (N_AB_LAYERS, D_MODEL, AB_IN), D_MODEL ** -0.5)
    w_in_ab = w_in_ab.at[:, :, 2 * ATTN_WIDTH:3 * ATTN_WIDTH].multiply(BETA)
    return {
        "x_prompt": nrm(0, (BATCH, SEQ, D_MODEL), 1.0),
        "x_sample": nrm(1, (DEC_BATCH, DEC_SEQ, D_MODEL), 1.0),
        "cache_win_k": nrm(2, (N_AB_LAYERS, DEC_BATCH, win_buf, ATTN_HEADS, HEAD_DIM), 1.0),
        "cache_win_v": nrm(3, (N_AB_LAYERS, DEC_BATCH, win_buf, ATTN_HEADS, HEAD_DIM), BETA),
        "state_conv": nrm(4, (N_AB_LAYERS, DEC_BATCH, CONV_WIDTH - 1, CONV_CH), 0.5),
        "state_hgrn": nrm(5, (N_C_LAYERS, DEC_BATCH, HGRN_HEADS, HGRN_DK, HGRN_DV), 0.5),
        "w_in_ab": w_in_ab,
        "rel_bias": nrm(7, (NUM_BUCKETS, ATTN_HEADS), 0.1),
        "conv_w": nrm(8, (N_AB_LAYERS, CONV_WIDTH, CONV_CH), CONV_WIDTH ** -0.5),
        "conv_b": nrm(9, (N_AB_LAYERS, CONV_CH), 0.01),
        "conv_norm_g": 1.0 + nrm(10, (N_AB_LAYERS, CONV_CH), 0.01),
        "conv_norm_b": nrm(11, (N_AB_LAYERS, CONV_CH), 0.01),
        "w_out_ab": nrm(12, (N_AB_LAYERS, ATTN_WIDTH + CONV_CH, D_MODEL), BETA * (ATTN_WIDTH + CONV_CH) ** -0.5),
        "w_in_c": nrm(13, (N_C_LAYERS, D_MODEL, C_IN), D_MODEL ** -0.5),
        "hgrn_lb": nrm(14, (DEPTH, HGRN_FDIM), 0.1),
        "hgrn_norm_g": 1.0 + nrm(15, (N_C_LAYERS, HGRN_DV), 0.01),
        "w_out_c": nrm(16, (N_C_LAYERS, D_MODEL, D_MODEL), BETA * D_MODEL ** -0.5),
        "ln_g": 1.0 + nrm(17, (DEPTH, 2, D_MODEL), 0.01),
        "ln_b": nrm(18, (DEPTH, 2, D_MODEL), 0.01),
        "router_w": nrm(19, (D_MODEL, N_EXPERTS), D_MODEL ** -0.5),
        "router_b": nrm(20, (N_EXPERTS,), 0.01),
        "w_gate_up": nrm(21, (DEPTH, N_EXPERTS, D_MODEL, 2 * D_EXPERT), D_MODEL ** -0.5),
        "w_down": nrm(22, (DEPTH, N_EXPERTS, D_EXPERT, D_MODEL), BETA * D_EXPERT ** -0.5),
    }


def reference(x_prompt, x_sample, cache_win_k, cache_win_v, state_conv, state_hgrn, w_in_ab, rel_bias,
              conv_w, conv_b, conv_norm_g, conv_norm_b, w_out_ab, w_in_c, hgrn_lb, hgrn_norm_g, w_out_c,
              ln_g, ln_b, router_w, router_b, w_gate_up, w_down):
    xp, xs = x_prompt, x_sample
    bp, sp, _ = xp.shape
    n_prompt = bp * sp
    lb_soft = jax.nn.softmax(hgrn_lb.astype(F32), axis=0)
    lower_bounds = jnp.cumsum(lb_soft, axis=0) - lb_soft[0]
    wk_p, wv_p, cv_p, hs_p = [], [], [], []
    wk_s, wv_s, cv_s, hs_s = [], [], [], []
    for l in range(DEPTH):
        if l % 2 == 0:
            a = l // 2
            conv_args = (conv_w[a], conv_b[a], conv_norm_g[a], conv_norm_b[a])
            mp, kp, vp, cp = ab_mixer_prompt(xp, w_in_ab[a], rel_bias, *conv_args, w_out_ab[a])
            ms, ks_, vs_, cs = ab_mixer_sample(xs, cache_win_k[a], cache_win_v[a], state_conv[a], w_in_ab[a],
                                               rel_bias, *conv_args, w_out_ab[a])
            wk_p.append(kp)
            wv_p.append(vp)
            cv_p.append(cp)
            wk_s.append(ks_)
            wv_s.append(vs_)
            cv_s.append(cs)
        else:
            c = l // 2
            s0 = jnp.zeros((bp, HGRN_HEADS, HGRN_DK, HGRN_DV), F32)
            mp, hp = hgrn2_mixer(xp, s0, w_in_c[c], lower_bounds[l], hgrn_norm_g[c], w_out_c[c])
            ms, hsn = hgrn2_mixer(xs, state_hgrn[c], w_in_c[c], lower_bounds[l], hgrn_norm_g[c], w_out_c[c])
            hs_p.append(hp)
            hs_s.append(hsn)
        xp = layer_norm(ALPHA * xp + mp, ln_g[l, 0], ln_b[l, 0])
        xs = layer_norm(ALPHA * xs + ms, ln_g[l, 0], ln_b[l, 0])
        tok = jnp.concatenate([xp.reshape(-1, D_MODEL), xs.reshape(-1, D_MODEL)], 0)
        ff = moe_ffn(tok, router_w, router_b, w_gate_up[l], w_down[l])
        xp = layer_norm(ALPHA * xp + ff[:n_prompt].reshape(xp.shape), ln_g[l, 1], ln_b[l, 1])
        xs = layer_norm(ALPHA * xs + ff[n_prompt:].reshape(xs.shape), ln_g[l, 1], ln_b[l, 1])
    return (xp, xs, jnp.stack(wk_p), jnp.stack(wv_p), jnp.stack(cv_p), jnp.stack(hs_p),
            jnp.stack(wk_s), jnp.stack(wv_s), jnp.stack(cv_s), jnp.stack(hs_s))
```

```python
import functools
import math

import jax
import jax.numpy as jnp
from jax import lax
from jax.experimental import pallas as pl
from jax.experimental.pallas import tpu as pltpu

F32 = jnp.float32
BF16 = jnp.bfloat16

DEPTH = 2
HEAD_DIM = 128
DILATED_PATTERNS = ((128, 1), (512, 4), (2048, 16))
NUM_BUCKETS = 32
MAX_EXACT = 16
MAX_DISTANCE = 2048
CONV_WIDTH = 31
N_EXPERTS = 32
N_GROUPS = 8
GROUP_SIZE = N_EXPERTS // N_GROUPS
TOP_K = 2
ALPHA = (2.0 * DEPTH) ** 0.25
EPS = 1e-5
NEG = -1e30
VMEM_LIMIT = 56 * 1024 * 1024
CONV_HALO = 32


def _params(*sem):
    return pltpu.CompilerParams(dimension_semantics=sem, vmem_limit_bytes=VMEM_LIMIT)


def _sigmoid(x):
    return 1.0 / (1.0 + jnp.exp(-x))


def _silu(x):
    return x * _sigmoid(x)


def _dot_nt(a, b, precision=None):
    return lax.dot_general(a, b, (((1,), (1,)), ((), ())), preferred_element_type=F32, precision=precision)


def _dot_tn(a, b, precision=None):
    return lax.dot_general(a, b, (((0,), (0,)), ((), ())), preferred_element_type=F32, precision=precision)


def _dot(a, b, precision=None):
    return jnp.dot(a, b, preferred_element_type=F32, precision=precision)


def _mm_kernel(*refs, n_x):
    x_refs = refs[:n_x]
    w_ref, o_ref, wb_ref = refs[n_x:]

    @pl.when(pl.program_id(1) == 0)
    def _():
        wb_ref[...] = w_ref[...].astype(BF16)

    acc = None
    off = 0
    for xr in x_refs:
        k = xr.shape[-1]
        part = _dot(xr[...], wb_ref[off:off + k, :])
        acc = part if acc is None else acc + part
        off += k
    o_ref[...] = acc.astype(o_ref.dtype)


def _matmul(xs, w, layer, col_off, n_cols, *, tm, tn, name):
    m_rows = xs[0].shape[0]
    k_tot = sum(x.shape[1] for x in xs)
    assert w.shape[1] == k_tot and m_rows % tm == 0 and n_cols % tn == 0 and col_off % tn == 0
    coff = col_off // tn
    in_specs = [pl.BlockSpec((tm, x.shape[1]), lambda n, m: (m, 0)) for x in xs]
    in_specs.append(pl.BlockSpec((None, k_tot, tn), lambda n, m: (layer, 0, n + coff)))
    return pl.pallas_call(
        functools.partial(_mm_kernel, n_x=len(xs)),
        grid=(n_cols // tn, m_rows // tm),
        in_specs=in_specs,
        out_specs=pl.BlockSpec((tm, tn), lambda n, m: (m, n)),
        out_shape=jax.ShapeDtypeStruct((m_rows, n_cols), F32),
        scratch_shapes=[pltpu.VMEM((k_tot, tn), BF16)],
        compiler_params=_params("arbitrary", "arbitrary"),
        name=name,
    )(*xs, w)


def _layer_norm_rows(z, g, b):
    mu = jnp.mean(z, -1, keepdims=True)
    zc = z - mu
    var = jnp.mean(zc * zc, -1, keepdims=True)
    return zc * lax.rsqrt(var + EPS) * g + b


def _ln_res_kernel(x_ref, m_ref, g_ref, b_ref, of_ref, ob_ref):
    y = _layer_norm_rows(ALPHA * x_ref[...] + m_ref[...], g_ref[...], b_ref[...])
    of_ref[...] = y
    ob_ref[...] = y.astype(BF16)


def _ln_res(x, m, g, b, *, tm, name):
    rows, d = x.shape
    row_spec = pl.BlockSpec((tm, d), lambda i: (i, 0))
    vec_spec = pl.BlockSpec((1, d), lambda i: (0, 0))
    return pl.pallas_call(
        _ln_res_kernel,
        grid=(rows // tm,),
        in_specs=[row_spec, row_spec, vec_spec, vec_spec],
        out_specs=[row_spec, row_spec],
        out_shape=[jax.ShapeDtypeStruct((rows, d), F32), jax.ShapeDtypeStruct((rows, d), BF16)],
        compiler_params=_params("arbitrary"),
        name=name,
    )(x, m, g.reshape(1, d), b.reshape(1, d))


def _ln_moe_kernel(x_ref, y0_ref, y1_ref, gt_ref, g_ref, b_ref, of_ref, ob_ref):
    gt = gt_ref[...]
    ff = gt[:, 0:1] * y0_ref[...].astype(F32) + gt[:, 1:2] * y1_ref[...].astype(F32)
    y = _layer_norm_rows(ALPHA * x_ref[...] + ff, g_ref[...], b_ref[...])
    of_ref[...] = y
    ob_ref[...] = y.astype(BF16)


def _ln_moe(x, y0, y1, gates, g, b, *, tm, name):
    rows, d = x.shape
    row_spec = pl.BlockSpec((tm, d), lambda i: (i, 0))
    vec_spec = pl.BlockSpec((1, d), lambda i: (0, 0))
    return pl.pallas_call(
        _ln_moe_kernel,
        grid=(rows // tm,),
        in_specs=[row_spec, row_spec, row_spec, pl.BlockSpec((tm, TOP_K), lambda i: (i, 0)), vec_spec, vec_spec],
        out_specs=[row_spec, row_spec],
        out_shape=[jax.ShapeDtypeStruct((rows, d), F32), jax.ShapeDtypeStruct((rows, d), BF16)],
        compiler_params=_params("arbitrary"),
        name=name,
    )(x, y0, y1, gates, g.reshape(1, d), b.reshape(1, d))


def _rel_bucket(dist):
    dist = dist.astype(jnp.int32)
    d = jnp.maximum(dist, 1).astype(F32)
    large = MAX_EXACT + (jnp.log(d / MAX_EXACT) / math.log(MAX_DISTANCE / MAX_EXACT)
                         * (NUM_BUCKETS - MAX_EXACT)).astype(jnp.int32)
    large = jnp.minimum(large, NUM_BUCKETS - 1)
    return jnp.where(dist < MAX_EXACT, dist, large)


def _pattern_bias(rel_bias, d):
    base = jnp.moveaxis(rel_bias[_rel_bucket(jnp.maximum(d, 0))].astype(F32), -1, 0)
    outs = []
    for w, r in DILATED_PATTERNS:
        ok = (d >= 0) & (d % r == 0) & (d <= w)
        outs.append(jnp.where(ok[None], base, NEG))
    return jnp.stack(outs)


def _merged_bias(rel_bias, d):
    count = sum(((d >= 0) & (d % r == 0) & (d <= w)).astype(F32) for w, r in DILATED_PATTERNS)
    base = jnp.moveaxis(rel_bias[_rel_bucket(jnp.maximum(d, 0))].astype(F32), -1, 0)
    return jnp.where((count > 0)[None], base + jnp.log(jnp.maximum(count, 1.0))[None], NEG)


def _attn_prompt_kernel(q_ref, k_ref, v_ref, bias_ref, o_ref, kb_ref, vb_ref, *, tq):
    i = pl.program_id(2)

    @pl.when(i == 0)
    def _():
        kb_ref[...] = k_ref[...].astype(BF16)
        vb_ref[...] = v_ref[...].astype(BF16)

    q = (q_ref[...] * (HEAD_DIM ** -0.5)).astype(BF16)

    def body(j, carry):
        m, l, acc = carry
        start = pl.multiple_of(j * tq, tq)
        s = _dot_nt(q, kb_ref[pl.ds(start, tq), :]) + bias_ref[i - j]
        m_new = jnp.maximum(m, jnp.max(s, -1, keepdims=True))
        p = jnp.exp(s - m_new)
        a = jnp.exp(m - m_new)
        l = a * l + jnp.sum(p, -1, keepdims=True)
        acc = a * acc + _dot(p.astype(BF16), vb_ref[pl.ds(start, tq), :])
        return m_new, l, acc

    init = (jnp.full((tq, 1), NEG, F32), jnp.zeros((tq, 1), F32), jnp.zeros((tq, HEAD_DIM), F32))
    _, l, acc = lax.fori_loop(0, i + 1, body, init)
    o_ref[...] = (acc / l).astype(o_ref.dtype)


def _attn_prompt(proj, rel_bias, n_heads, *, tq):
    b, s, _ = proj.shape
    nd = s // tq
    dist = jnp.arange(nd)[:, None, None] * tq + jnp.arange(tq)[None, :, None] - jnp.arange(tq)[None, None, :]
    table = _merged_bias(rel_bias, dist)
    return pl.pallas_call(
        functools.partial(_attn_prompt_kernel, tq=tq),
        grid=(n_heads, b, nd),
        in_specs=[
            pl.BlockSpec((None, tq, HEAD_DIM), lambda h, bb, i: (bb, i, h)),
            pl.BlockSpec((None, s, HEAD_DIM), lambda h, bb, i: (bb, 0, n_heads + h)),
            pl.BlockSpec((None, s, HEAD_DIM), lambda h, bb, i: (bb, 0, 2 * n_heads + h)),
            pl.BlockSpec((None, nd, tq, tq), lambda h, bb, i: (h, 0, 0, 0)),
        ],
        out_specs=pl.BlockSpec((None, tq, HEAD_DIM), lambda h, bb, i: (bb, i, h)),
        out_shape=jax.ShapeDtypeStruct((b, s, n_heads * HEAD_DIM), BF16),
        scratch_shapes=[pltpu.VMEM((s, HEAD_DIM), BF16), pltpu.VMEM((s, HEAD_DIM), BF16)],
        compiler_params=_params("arbitrary", "arbitrary", "arbitrary"),
        name="attn_prompt",
    )(proj, proj, proj, table)


QUERY_PAD = 16


def _attn_sample_kernel(q_ref, kn_ref, vn_ref, kc_ref, vc_ref, bias_ref, o_ref, kx_ref, vx_ref, *, hg, t):
    p = kc_ref.shape[0]
    n_keys, w = kx_ref.shape
    zrow = jnp.zeros((n_keys - p - t, w), F32)
    kx_ref[0:p, :] = kc_ref[...].astype(BF16)
    vx_ref[0:p, :] = vc_ref[...].astype(BF16)
    kx_ref[p:n_keys, :] = jnp.concatenate([kn_ref[...], zrow], 0).astype(BF16)
    vx_ref[p:n_keys, :] = jnp.concatenate([vn_ref[...], zrow], 0).astype(BF16)
    n_pat = bias_ref.shape[0]
    add = lambda a, b: a + b
    for h in range(hg):
        cs = slice(h * HEAD_DIM, (h + 1) * HEAD_DIM)
        q16 = jnp.concatenate([q_ref[:, cs], jnp.zeros((QUERY_PAD - t, HEAD_DIM), F32)], 0).astype(BF16)
        s = _dot_nt(q16, kx_ref[:, cs])[0:t, :] * (HEAD_DIM ** -0.5)
        outs, lses = [], []
        for g in range(n_pat):
            lg = s + bias_ref[g, h]
            m = jnp.max(lg, -1, keepdims=True)
            e = jnp.exp(lg - m)
            den = jnp.sum(e, -1, keepdims=True)
            p16 = jnp.concatenate([e / den, jnp.zeros((QUERY_PAD - t, n_keys), F32)], 0).astype(BF16)
            outs.append(_dot(p16, vx_ref[:, cs])[0:t, :])
            lses.append(m + jnp.log(den))
        top = functools.reduce(jnp.maximum, lses)
        ws = [jnp.exp(l - top) for l in lses]
        tot = functools.reduce(add, ws)
        o_ref[:, cs] = functools.reduce(add, [(wg / tot) * og for wg, og in zip(ws, outs)])


def _attn_sample(proj, cache_k, cache_v, rel_bias, n_heads, t, *, hg):
    bt = proj.shape[0]
    b = bt // t
    p = cache_k.shape[1]
    w = hg * HEAD_DIM
    nhg = n_heads // hg
    assert t <= QUERY_PAD and p % QUERY_PAD == 0
    n_keys = p + QUERY_PAD
    key = jnp.arange(n_keys)
    dist = jnp.where(key[None, :] < p + t, p + jnp.arange(t)[:, None] - key[None, :], -1)
    bias = _pattern_bias(rel_bias, dist)
    n_pat = bias.shape[0]
    return pl.pallas_call(
        functools.partial(_attn_sample_kernel, hg=hg, t=t),
        grid=(b, nhg),
        in_specs=[
            pl.BlockSpec((t, w), lambda bb, g: (bb, g)),
            pl.BlockSpec((t, w), lambda bb, g: (bb, nhg + g)),
            pl.BlockSpec((t, w), lambda bb, g: (bb, 2 * nhg + g)),
            pl.BlockSpec((None, p, w), lambda bb, g: (bb, 0, g)),
            pl.BlockSpec((None, p, w), lambda bb, g: (bb, 0, g)),
            pl.BlockSpec((n_pat, hg, t, n_keys), lambda bb, g: (0, g, 0, 0)),
        ],
        out_specs=pl.BlockSpec((t, w), lambda bb, g: (bb, g)),
        out_shape=jax.ShapeDtypeStruct((bt, n_heads * HEAD_DIM), F32),
        scratch_shapes=[pltpu.VMEM((n_keys, w), BF16), pltpu.VMEM((n_keys, w), BF16)],
        compiler_params=_params("arbitrary", "arbitrary"),
        name="attn_sample",
    )(proj, proj, proj, cache_k, cache_v, bias)


def _round_bf16(x):
    return x.astype(BF16).astype(F32)


def _conv_kernel(a_ref, g_ref, st_ref, w_ref, cb_ref, ng_ref, nb_ref, o_ref, tail_ref, ext_ref, y_ref,
                 extr_ref, *, t, rc, cc):
    ti = pl.program_id(1)
    ch = a_ref.shape[-1]

    @pl.when(ti == 0)
    def _():
        ext_ref[0:CONV_HALO, :] = st_ref[...]

    ext_ref[CONV_HALO:CONV_HALO + t, :] = a_ref[...] * _sigmoid(g_ref[...])
    extr_ref[...] = _round_bf16(ext_ref[...])
    first = CONV_HALO - (CONV_WIDTH - 1)
    for r0 in range(0, t, rc):
        for c0 in range(0, ch, cc):
            acc = jnp.zeros((rc, cc), F32)
            for j in range(CONV_WIDTH):
                acc = acc + w_ref[j:j + 1, c0:c0 + cc] * extr_ref[first + r0 + j:first + r0 + j + rc, c0:c0 + cc]
            y_ref[r0:r0 + rc, c0:c0 + cc] = acc + cb_ref[:, c0:c0 + cc]
    for r0 in range(0, t, rc):
        y = _layer_norm_rows(y_ref[r0:r0 + rc, :], ng_ref[...], nb_ref[...])
        o_ref[r0:r0 + rc, :] = _silu(y).astype(o_ref.dtype)

    @pl.when(ti == pl.num_programs(1) - 1)
    def _():
        tail_ref[...] = ext_ref[t + first:t + CONV_HALO, :]

    ext_ref[0:CONV_HALO, :] = ext_ref[t:t + CONV_HALO, :]


def _conv_module(proj, a_blk, state, conv_w, conv_b, ng, nb, *, ch, t, out_dtype, name):
    b, length, _ = proj.shape
    keep = CONV_WIDTH - 1
    st = jnp.pad(state, ((0, 0), (CONV_HALO - keep, 0), (0, 0)))
    rc = min(t, 32)
    vec_spec = pl.BlockSpec((1, ch), lambda bb, i: (0, 0))
    return pl.pallas_call(
        functools.partial(_conv_kernel, t=t, rc=rc, cc=512),
        grid=(b, length // t),
        in_specs=[
            pl.BlockSpec((None, t, ch), lambda bb, i: (bb, i, a_blk)),
            pl.BlockSpec((None, t, ch), lambda bb, i: (bb, i, a_blk + 1)),
            pl.BlockSpec((None, CONV_HALO, ch), lambda bb, i: (bb, 0, 0)),
            pl.BlockSpec((CONV_WIDTH, ch), lambda bb, i: (0, 0)),
            vec_spec, vec_spec, vec_spec,
        ],
        out_specs=[
            pl.BlockSpec((None, t, ch), lambda bb, i: (bb, i, 0)),
            pl.BlockSpec((None, keep, ch), lambda bb, i: (bb, 0, 0)),
        ],
        out_shape=[jax.ShapeDtypeStruct((b, length, ch), out_dtype),
                   jax.ShapeDtypeStruct((b, keep, ch), F32)],
        scratch_shapes=[pltpu.VMEM((CONV_HALO + t, ch), F32), pltpu.VMEM((t, ch), F32),
                        pltpu.VMEM((CONV_HALO + t, ch), F32)],
        compiler_params=_params("arbitrary", "arbitrary"),
        name=name,
    )(proj, proj, st, conv_w, conv_b.reshape(1, ch), ng.reshape(1, ch), nb.reshape(1, ch))


def _split3(x):
    a1 = x.astype(BF16)
    r1 = x - a1.astype(F32)
    a2 = r1.astype(BF16)
    a3 = (r1 - a2.astype(F32)).astype(BF16)
    return a1, a2, a3


def _hgrn_kernel(q_ref, f_ref, i_ref, g_ref, lb_ref, ng_ref, s0_ref, o_ref, sfin_ref, st_ref,
                 *, hg, t, c, pairwise):
    ti = pl.program_id(2)
    dk = HEAD_DIM
    hi = lax.Precision.HIGHEST

    @pl.when(ti == 0)
    def _():
        for h in range(hg):
            st_ref[h] = s0_ref[h].T

    lb = lb_ref[...]
    ng = ng_ref[...]
    causal = lax.broadcasted_iota(jnp.int32, (c, c), 0) >= lax.broadcasted_iota(jnp.int32, (c, c), 1)
    step = lax.broadcasted_iota(jnp.int32, (c, 1), 0)
    for c0 in range(0, t, c):
        rows = slice(c0, c0 + c)
        fg = lb + (1.0 - lb) * _sigmoid(f_ref[rows, :])
        kk = 1.0 - fg
        qq = _silu(q_ref[rows, :])
        logf = jnp.log(fg)
        if pairwise:
            acc = [logf[0:1, :]]
            for r in range(1, c):
                acc.append(acc[-1] + logf[r:r + 1, :])
            cum = jnp.concatenate(acc, 0)
        else:
            l1, l2, l3 = _split3(logf)
            tri = causal.astype(BF16)
            cum = _dot(tri, l1) + _dot(tri, l2) + _dot(tri, l3)
            mid = cum[c // 2 - 1:c // 2, :]
            q_md = (qq * jnp.exp(cum - mid)).astype(BF16)
            k_md = (kk * jnp.exp(mid - cum)).astype(BF16)
        last = cum[c - 1:c, :]
        q_in = qq * jnp.exp(cum)
        k_end = kk * jnp.exp(last - cum)
        dec = jnp.exp(last)
        for h in range(hg):
            cs = slice(h * dk, (h + 1) * dk)
            st = st_ref[h]
            if pairwise:
                v = _round_bf16(i_ref[rows, cs])
                o_rows = []
                for r in range(c):
                    pair = qq[r:r + 1, cs] * jnp.exp(cum[r:r + 1, cs] - cum[:, cs]) * kk[:, cs]
                    sc = jnp.where(step <= r, jnp.sum(pair, -1, keepdims=True), 0.0)
                    o_rows.append(jnp.sum(_round_bf16(sc) * v, 0, keepdims=True))
                o = _dot_nt(_round_bf16(q_in[:, cs]), _round_bf16(st), hi) + jnp.concatenate(o_rows, 0)
                st_ref[h] = st * dec[:, cs] + _dot_tn(v, _round_bf16(k_end[:, cs]), hi)
            else:
                v = i_ref[rows, cs].astype(BF16)
                sc = jnp.where(causal, _dot_nt(q_md[:, cs], k_md[:, cs]), 0.0)
                o = _dot_nt(q_in[:, cs].astype(BF16), st.astype(BF16)) + _dot(sc.astype(BF16), v)
                st_ref[h] = st * dec[:, cs] + _dot_tn(v, k_end[:, cs].astype(BF16))
            normed = o * lax.rsqrt(jnp.mean(o * o, -1, keepdims=True) + EPS) * ng
            o_ref[rows, cs] = (normed * _silu(g_ref[rows, cs])).astype(o_ref.dtype)

    @pl.when(ti == pl.num_programs(2) - 1)
    def _():
        for h in range(hg):
            sfin_ref[h] = st_ref[h].T


def _hgrn(proj, s0, lb, ng, n_heads, *, hg, t, c, pairwise, out_dtype, name):
    b, length, _ = proj.shape
    w = hg * HEAD_DIM
    nhg = n_heads // hg
    col = lambda k: pl.BlockSpec((None, t, w), lambda bb, g, i: (bb, i, k * nhg + g))
    st_spec = pl.BlockSpec((None, hg, HEAD_DIM, HEAD_DIM), lambda bb, g, i: (bb, g, 0, 0))
    return pl.pallas_call(
        functools.partial(_hgrn_kernel, hg=hg, t=t, c=c, pairwise=pairwise),
        grid=(b, nhg, length // t),
        in_specs=[col(0), col(1), col(2), col(3),
                  pl.BlockSpec((1, w), lambda bb, g, i: (0, g)),
                  pl.BlockSpec((1, HEAD_DIM), lambda bb, g, i: (0, 0)),
                  st_spec],
        out_specs=[pl.BlockSpec((None, t, w), lambda bb, g, i: (bb, i, g)), st_spec],
        out_shape=[jax.ShapeDtypeStruct((b, length, n_heads * HEAD_DIM), out_dtype),
                   jax.ShapeDtypeStruct(s0.shape, F32)],
        scratch_shapes=[pltpu.VMEM((hg, HEAD_DIM, HEAD_DIM), F32)],
        compiler_params=_params("arbitrary", "arbitrary", "arbitrary"),
        name=name,
    )(proj, proj, proj, proj, lb.reshape(1, -1), ng.reshape(1, HEAD_DIM), s0)


def _top2(vals):
    n = len(vals)
    m1 = functools.reduce(jnp.maximum, vals)
    i1 = jnp.full(m1.shape, n - 1, jnp.int32)
    for k in range(n - 2, -1, -1):
        i1 = jnp.where(vals[k] == m1, k, i1)
    rest = [jnp.where(i1 == k, -1.0, vals[k]) for k in range(n)]
    m2 = functools.reduce(jnp.maximum, rest)
    i2 = jnp.full(m1.shape, n - 1, jnp.int32)
    for k in range(n - 2, -1, -1):
        i2 = jnp.where(rest[k] == m2, k, i2)
    return m1, i1, m2, i2


def _router_kernel(x_ref, w_ref, b_ref, ex_ref, gt_ref):
    logits = _dot_nt(w_ref[...].astype(BF16), x_ref[...]) + b_ref[...]
    e = jnp.exp(logits - jnp.max(logits, 0, keepdims=True))
    p = e / jnp.sum(e, 0, keepdims=True)
    members = [p[k * N_GROUPS:(k + 1) * N_GROUPS, :] for k in range(GROUP_SIZE)]
    m1, i1, m2, i2 = _top2(members)
    score = m1 + m2
    gid = lax.broadcasted_iota(jnp.int32, score.shape, 0)
    best = jnp.max(score, 0, keepdims=True)
    gsel = jnp.min(jnp.where(score == best, gid, N_GROUPS), 0, keepdims=True)
    sel = gid == gsel
    pick_f = lambda a: jnp.sum(jnp.where(sel, a, 0.0), 0, keepdims=True)
    pick_i = lambda a: jnp.sum(jnp.where(sel, a, 0), 0, keepdims=True)
    p1, p2 = pick_f(m1), pick_f(m2)
    ex_ref[0:1, :] = gsel * GROUP_SIZE + pick_i(i1)
    ex_ref[1:2, :] = gsel * GROUP_SIZE + pick_i(i2)
    gt_ref[0:1, :] = p1 / (p1 + p2)
    gt_ref[1:2, :] = p2 / (p1 + p2)


def _router(x, w_t, b_col, *, tm, name):
    n, d = x.shape
    out_spec = pl.BlockSpec((TOP_K, tm), lambda i: (0, i))
    return pl.pallas_call(
        _router_kernel,
        grid=(n // tm,),
        in_specs=[pl.BlockSpec((tm, d), lambda i: (i, 0)),
                  pl.BlockSpec((N_EXPERTS, d), lambda i: (0, 0)),
                  pl.BlockSpec((N_EXPERTS, 1), lambda i: (0, 0))],
        out_specs=[out_spec, out_spec],
        out_shape=[jax.ShapeDtypeStruct((TOP_K, n), jnp.int32), jax.ShapeDtypeStruct((TOP_K, n), F32)],
        compiler_params=_params("arbitrary"),
        name=name,
    )(x, w_t, b_col)


def _expert_changed(be_ref, i):
    return jnp.logical_or(i == 0, be_ref[i] != be_ref[jnp.maximum(i - 1, 0)])


def _gate_up_kernel(be_ref, nb_ref, x_ref, wg_ref, wu_ref, h_ref, wgb_ref, wub_ref):
    i = pl.program_id(1)

    @pl.when(_expert_changed(be_ref, i))
    def _():
        wgb_ref[...] = wg_ref[...].astype(BF16)
        wub_ref[...] = wu_ref[...].astype(BF16)

    @pl.when(i < nb_ref[0])
    def _():
        x = x_ref[...]
        h = _silu(_dot(x, wgb_ref[...])) * _dot(x, wub_ref[...])
        h_ref[...] = h.astype(h_ref.dtype)

    @pl.when(i >= nb_ref[0])
    def _():
        h_ref[...] = jnp.zeros(h_ref.shape, h_ref.dtype)


def _down_kernel(be_ref, nb_ref, h_ref, wd_ref, y_ref, wdb_ref):
    i = pl.program_id(1)

    @pl.when(_expert_changed(be_ref, i))
    def _():
        wdb_ref[...] = wd_ref[...].astype(BF16)

    @pl.when(i < nb_ref[0])
    def _():
        y_ref[...] = _dot(h_ref[...], wdb_ref[...]).astype(y_ref.dtype)

    @pl.when(i >= nb_ref[0])
    def _():
        y_ref[...] = jnp.zeros(y_ref.shape, y_ref.dtype)


def _expert_ffn(xs, block_expert, n_used, w_gate_up, w_down, layer, *, tm, th, tn):
    rows, d = xs.shape
    d_exp = w_down.shape[2]
    n_blocks = rows // tm
    nj = d_exp // th
    h = pl.pallas_call(
        _gate_up_kernel,
        grid_spec=pltpu.PrefetchScalarGridSpec(
            num_scalar_prefetch=2,
            grid=(nj, n_blocks),
            in_specs=[
                pl.BlockSpec((tm, d), lambda j, i, be, nb: (i, 0)),
                pl.BlockSpec((None, None, d, th), lambda j, i, be, nb: (layer, be[i], 0, j)),
                pl.BlockSpec((None, None, d, th), lambda j, i, be, nb: (layer, be[i], 0, nj + j)),
            ],
            out_specs=pl.BlockSpec((tm, th), lambda j, i, be, nb: (i, j)),
            scratch_shapes=[pltpu.VMEM((d, th), BF16), pltpu.VMEM((d, th), BF16)],
        ),
        out_shape=jax.ShapeDtypeStruct((rows, d_exp), BF16),
        compiler_params=_params("arbitrary", "arbitrary"),
        name=f"moe_gate_up_{layer}",
    )(block_expert, n_used, xs, w_gate_up, w_gate_up)
    return pl.pallas_call(
        _down_kernel,
        grid_spec=pltpu.PrefetchScalarGridSpec(
            num_scalar_prefetch=2,
            grid=(d // tn, n_blocks),
            in_specs=[
                pl.BlockSpec((tm, d_exp), lambda j, i, be, nb: (i, 0)),
                pl.BlockSpec((None, None, d_exp, tn), lambda j, i, be, nb: (layer, be[i], 0, j)),
            ],
            out_specs=pl.BlockSpec((tm, tn), lambda j, i, be, nb: (i, j)),
            scratch_shapes=[pltpu.VMEM((d_exp, tn), BF16)],
        ),
        out_shape=jax.ShapeDtypeStruct((rows, d), F32),
        compiler_params=_params("arbitrary", "arbitrary"),
        name=f"moe_down_{layer}",
    )(block_expert, n_used, h, w_down)


def _moe_layer(x_rows, xb_rows, experts, gates, w_gate_up, w_down, layer, ln_g, ln_b, *, tm_e, tm_ln):
    n, d = x_rows.shape
    n_assign = n * TOP_K
    flat_e = experts.reshape(-1)
    order = jnp.argsort(flat_e)
    e_s = flat_e[order]
    tok_s = (order // TOP_K).astype(jnp.int32)
    counts = jnp.bincount(flat_e, length=N_EXPERTS)
    starts = jnp.cumsum(counts) - counts
    pcounts = ((counts + tm_e - 1) // tm_e) * tm_e
    pends = jnp.cumsum(pcounts)
    pstarts = pends - pcounts
    dest_s = (pstarts[e_s] + (jnp.arange(n_assign) - starts[e_s])).astype(jnp.int32)
    n_blocks = -(-n_assign // tm_e) + N_EXPERTS
    src = jnp.zeros((n_blocks * tm_e,), jnp.int32).at[dest_s].set(tok_s)
    dest = jnp.zeros((n_assign,), jnp.int32).at[order].set(dest_s).reshape(n, TOP_K)
    block_expert = jnp.minimum(jnp.searchsorted(pends, jnp.arange(n_blocks) * tm_e, side='right'),
                               N_EXPERTS - 1).astype(jnp.int32)
    n_used = (pends[-1] // tm_e).astype(jnp.int32).reshape(1)
    xs = jnp.take(xb_rows, src, axis=0)
    ys = _expert_ffn(xs, block_expert, n_used, w_gate_up, w_down, layer, tm=tm_e, th=256, tn=1024)
    y0 = jnp.take(ys, dest[:, 0], axis=0)
    y1 = jnp.take(ys, dest[:, 1], axis=0)
    return _ln_moe(x_rows, y0, y1, gates, ln_g, ln_b, tm=tm_ln, name=f"ln_moe_{layer}")


def kernel(x_prompt, x_sample, cache_win_k, cache_win_v, state_conv, state_hgrn, w_in_ab, rel_bias, conv_w,
           conv_b, conv_norm_g, conv_norm_b, w_out_ab, w_in_c, hgrn_lb, hgrn_norm_g, w_out_c, ln_g, ln_b,
           router_w, router_b, w_gate_up, w_down):
    bp, sp, d = x_prompt.shape
    bs, ts, _ = x_sample.shape
    n_p, n_s = bp * sp, bs * ts
    attn_w = d // 2
    n_ah = attn_w // HEAD_DIM
    conv_ch = d - attn_w
    n_hh = d // HEAD_DIM
    keep = CONV_WIDTH - 1

    lb_soft = jax.nn.softmax(hgrn_lb.astype(F32), axis=0)
    lower_bounds = jnp.cumsum(lb_soft, axis=0) - lb_soft[0]
    perm = (jnp.arange(N_EXPERTS) % N_GROUPS) * GROUP_SIZE + jnp.arange(N_EXPERTS) // N_GROUPS
    router_wt = router_w.T[perm]
    router_bc = router_b.astype(F32)[perm].reshape(N_EXPERTS, 1)

    xp = x_prompt.reshape(n_p, d)
    xs = x_sample.reshape(n_s, d)
    xpb = xp.astype(BF16)
    xsb = xs.astype(BF16)
    outs = {}
    for l in range(DEPTH):
        if l % 2 == 0:
            a = l // 2
            n_in = w_in_ab.shape[2]
            proj_p = _matmul([xpb], w_in_ab, a, 0, n_in, tm=512, tn=512, name=f"ab_in_p{l}")
            proj_s = _matmul([xsb], w_in_ab, a, 0, n_in, tm=n_s, tn=512, name=f"ab_in_s{l}")
            pp3 = proj_p.reshape(bp, sp, n_in)
            attn_p = _attn_prompt(pp3, rel_bias, n_ah, tq=256)
            conv_p, tail_p = _conv_module(pp3, 3 * attn_w // conv_ch, jnp.zeros((bp, keep, conv_ch), F32),
                                          conv_w[a], conv_b[a], conv_norm_g[a], conv_norm_b[a],
                                          ch=conv_ch, t=128, out_dtype=BF16, name=f"conv_p{l}")
            attn_s = _attn_sample(proj_s, cache_win_k[a].reshape(bs, -1, attn_w),
                                  cache_win_v[a].reshape(bs, -1, attn_w), rel_bias, n_ah, ts, hg=4)
            conv_s, tail_s = _conv_module(proj_s.reshape(bs, ts, n_in), 3 * attn_w // conv_ch, state_conv[a],
                                          conv_w[a], conv_b[a], conv_norm_g[a], conv_norm_b[a],
                                          ch=conv_ch, t=ts, out_dtype=F32, name=f"conv_s{l}")
            m_p = _matmul([attn_p.reshape(n_p, attn_w), conv_p.reshape(n_p, conv_ch)], w_out_ab, a, 0, d,
                          tm=512, tn=512, name=f"ab_out_p{l}")
            m_s = _matmul([attn_s.astype(BF16), conv_s.reshape(n_s, conv_ch).astype(BF16)], w_out_ab, a, 0, d,
                          tm=n_s, tn=512, name=f"ab_out_s{l}")
            outs.setdefault("wk_p", []).append(proj_p[:, attn_w:2 * attn_w].reshape(bp, sp, n_ah, HEAD_DIM))
            outs.setdefault("wv_p", []).append(proj_p[:, 2 * attn_w:3 * attn_w].reshape(bp, sp, n_ah, HEAD_DIM))
            outs.setdefault("cv_p", []).append(tail_p)
            outs.setdefault("wk_s", []).append(proj_s[:, attn_w:2 * attn_w].reshape(bs, ts, n_ah, HEAD_DIM))
            outs.setdefault("wv_s", []).append(proj_s[:, 2 * attn_w:3 * attn_w].reshape(bs, ts, n_ah, HEAD_DIM))
            outs.setdefault("cv_s", []).append(tail_s)
        else:
            c = l // 2
            n_in = w_in_c.shape[2]
            proj_p = _matmul([xpb], w_in_c, c, 0, n_in, tm=512, tn=512, name=f"c_in_p{l}")
            proj_s = _matmul([xsb], w_in_c, c, 0, n_in, tm=n_s, tn=512, name=f"c_in_s{l}")
            o_p, h_p = _hgrn(proj_p.reshape(bp, sp, n_in), jnp.zeros((bp, n_hh, HEAD_DIM, HEAD_DIM), F32),
                             lower_bounds[l], hgrn_norm_g[c], n_hh, hg=4, t=256, c=64, pairwise=False,
                             out_dtype=BF16, name=f"hgrn_p{l}")
            o_s, h_s = _hgrn(proj_s.reshape(bs, ts, n_in), state_hgrn[c], lower_bounds[l], hgrn_norm_g[c],
                             n_hh, hg=4, t=ts, c=ts, pairwise=True, out_dtype=F32, name=f"hgrn_s{l}")
            m_p = _matmul([o_p.reshape(n_p, d)], w_out_c, c, 0, d, tm=512, tn=512, name=f"c_out_p{l}")
            m_s = _matmul([o_s.reshape(n_s, d).astype(BF16)], w_out_c, c, 0, d, tm=n_s, tn=512, name=f"c_out_s{l}")
            outs.setdefault("hs_p", []).append(h_p)
            outs.setdefault("hs_s", []).append(h_s)
        xp, xpb = _ln_res(xp, m_p, ln_g[l, 0], ln_b[l, 0], tm=256, name=f"ln_mix_p{l}")
        xs, xsb = _ln_res(xs, m_s, ln_g[l, 0], ln_b[l, 0], tm=n_s, name=f"ln_mix_s{l}")
        ex_p, gt_p = _router(xpb, router_wt, router_bc, tm=512, name=f"router_p{l}")
        ex_s, gt_s = _router(xsb, router_wt, router_bc, tm=n_s, name=f"router_s{l}")
        experts = jnp.concatenate([ex_p, ex_s], axis=1).T
        gates = jnp.concatenate([gt_p, gt_s], axis=1).T
        tok = jnp.concatenate([xp, xs], axis=0)
        tokb = jnp.concatenate([xpb, xsb], axis=0)
        yf, yb = _moe_layer(tok, tokb, experts, gates, w_gate_up, w_down, l, ln_g[l, 1], ln_b[l, 1],
                            tm_e=256, tm_ln=64)
        xp, xs = yf[:n_p], yf[n_p:]
        xpb, xsb = yb[:n_p], yb[n_p:]
    stack = lambda k: jnp.stack(outs[k])
    return (xp.reshape(bp, sp, d), xs.reshape(bs, ts, d), stack("wk_p"), stack("wv_p"), stack("cv_p"),
            stack("hs_p"), stack("wk_s"), stack("wv_s"), stack("cv_s"), stack("hs_s"))
```

```python
import functools
import math

import jax
import jax.numpy as jnp
from jax import lax
from jax.experimental import pallas as pl
from jax.experimental.pallas import tpu as pltpu

F32 = jnp.float32
BF16 = jnp.bfloat16

DEPTH = 2
HEAD_DIM = 128
DILATED_PATTERNS = ((128, 1), (512, 4), (2048, 16))
NUM_BUCKETS = 32
MAX_EXACT = 16
MAX_DISTANCE = 2048
CONV_WIDTH = 31
N_EXPERTS = 32
N_GROUPS = 8
GROUP_SIZE = N_EXPERTS // N_GROUPS
TOP_K = 2
ALPHA = (2.0 * DEPTH) ** 0.25
EPS = 1e-5
NEG = -1e30
VMEM_LIMIT = 56 * 1024 * 1024
CONV_HALO = 32


def _params(*sem):
    return pltpu.CompilerParams(dimension_semantics=sem, vmem_limit_bytes=VMEM_LIMIT)


def _sigmoid(x):
    return 1.0 / (1.0 + jnp.exp(-x))


def _silu(x):
    return x * _sigmoid(x)


def _dot_nt(a, b, precision=None):
    return lax.dot_general(a, b, (((1,), (1,)), ((), ())), preferred_element_type=F32, precision=precision)


def _dot_tn(a, b, precision=None):
    return lax.dot_general(a, b, (((0,), (0,)), ((), ())), preferred_element_type=F32, precision=precision)


def _dot(a, b, precision=None):
    return jnp.dot(a, b, preferred_element_type=F32, precision=precision)


def _mm_kernel(*refs, n_x):
    x_refs = refs[:n_x]
    w_ref, o_ref, wb_ref = refs[n_x:]

    @pl.when(pl.program_id(1) == 0)
    def _():
        wb_ref[...] = w_ref[...].astype(BF16)

    acc = None
    off = 0
    for xr in x_refs:
        k = xr.shape[-1]
        part = _dot(xr[...], wb_ref[off:off + k, :])
        acc = part if acc is None else acc + part
        off += k
    o_ref[...] = acc.astype(o_ref.dtype)


def _matmul(xs, w, layer, col_off, n_cols, *, tm, tn, name):
    m_rows = xs[0].shape[0]
    k_tot = sum(x.shape[1] for x in xs)
    assert w.shape[1] == k_tot and m_rows % tm == 0 and n_cols % tn == 0 and col_off % tn == 0
    coff = col_off // tn
    in_specs = [pl.BlockSpec((tm, x.shape[1]), lambda n, m: (m, 0)) for x in xs]
    in_specs.append(pl.BlockSpec((None, k_tot, tn), lambda n, m: (layer, 0, n + coff)))
    return pl.pallas_call(
        functools.partial(_mm_kernel, n_x=len(xs)),
        grid=(n_cols // tn, m_rows // tm),
        in_specs=in_specs,
        out_specs=pl.BlockSpec((tm, tn), lambda n, m: (m, n)),
        out_shape=jax.ShapeDtypeStruct((m_rows, n_cols), F32),
        scratch_shapes=[pltpu.VMEM((k_tot, tn), BF16)],
        compiler_params=_params("arbitrary", "arbitrary"),
        name=name,
    )(*xs, w)


def _layer_norm_rows(z, g, b):
    mu = jnp.mean(z, -1, keepdims=True)
    zc = z - mu
    var = jnp.mean(zc * zc, -1, keepdims=True)
    return zc * lax.rsqrt(var + EPS) * g + b


def _ln_res_kernel(x_ref, m_ref, g_ref, b_ref, of_ref, ob_ref):
    y = _layer_norm_rows(ALPHA * x_ref[...] + m_ref[...], g_ref[...], b_ref[...])
    of_ref[...] = y
    ob_ref[...] = y.astype(BF16)


def _ln_res(x, m, g, b, *, tm, name):
    rows, d = x.shape
    row_spec = pl.BlockSpec((tm, d), lambda i: (i, 0))
    vec_spec = pl.BlockSpec((1, d), lambda i: (0, 0))
    return pl.pallas_call(
        _ln_res_kernel,
        grid=(rows // tm,),
        in_specs=[row_spec, row_spec, vec_spec, vec_spec],
        out_specs=[row_spec, row_spec],
        out_shape=[jax.ShapeDtypeStruct((rows, d), F32), jax.ShapeDtypeStruct((rows, d), BF16)],
        compiler_params=_params("arbitrary"),
        name=name,
    )(x, m, g.reshape(1, d), b.reshape(1, d))


def _ln_moe_kernel(x_ref, y0_ref, y1_ref, gt_ref, g_ref, b_ref, of_ref, ob_ref):
    gt = gt_ref[...]
    ff = gt[:, 0:1] * y0_ref[...].astype(F32) + gt[:, 1:2] * y1_ref[...].astype(F32)
    y = _layer_norm_rows(ALPHA * x_ref[...] + ff, g_ref[...], b_ref[...])
    of_ref[...] = y
    ob_ref[...] = y.astype(BF16)


def _ln_moe(x, y0, y1, gates, g, b, *, tm, name):
    rows, d = x.shape
    row_spec = pl.BlockSpec((tm, d), lambda i: (i, 0))
    vec_spec = pl.BlockSpec((1, d), lambda i: (0, 0))
    return pl.pallas_call(
        _ln_moe_kernel,
        grid=(rows // tm,),
        in_specs=[row_spec, row_spec, row_spec, pl.BlockSpec((tm, TOP_K), lambda i: (i, 0)), vec_spec, vec_spec],
        out_specs=[row_spec, row_spec],
        out_shape=[jax.ShapeDtypeStruct((rows, d), F32), jax.ShapeDtypeStruct((rows, d), BF16)],
        compiler_params=_params("arbitrary"),
        name=name,
    )(x, y0, y1, gates, g.reshape(1, d), b.reshape(1, d))


def _rel_bucket(dist):
    dist = dist.astype(jnp.int32)
    d = jnp.maximum(dist, 1).astype(F32)
    large = MAX_EXACT + (jnp.log(d / MAX_EXACT) / math.log(MAX_DISTANCE / MAX_EXACT)
                         * (NUM_BUCKETS - MAX_EXACT)).astype(jnp.int32)
    large = jnp.minimum(large, NUM_BUCKETS - 1)
    return jnp.where(dist < MAX_EXACT, dist, large)


def _pattern_bias(rel_bias, d):
    base = jnp.moveaxis(rel_bias[_rel_bucket(jnp.maximum(d, 0))].astype(F32), -1, 0)
    outs = []
    for w, r in DILATED_PATTERNS:
        ok = (d >= 0) & (d % r == 0) & (d <= w)
        outs.append(jnp.where(ok[None], base, NEG))
    return jnp.stack(outs)


def _merged_bias(rel_bias, d):
    count = sum(((d >= 0) & (d % r == 0) & (d <= w)).astype(F32) for w, r in DILATED_PATTERNS)
    base = jnp.moveaxis(rel_bias[_rel_bucket(jnp.maximum(d, 0))].astype(F32), -1, 0)
    return jnp.where((count > 0)[None], base + jnp.log(jnp.maximum(count, 1.0))[None], NEG)


def _toeplitz_tiles(vec, nd, tq):
    h = vec.shape[0]
    span = 2 * tq - 1
    win = jnp.stack([vec[:, k * tq:k * tq + span] for k in range(nd)], 1)
    rev = jnp.pad(win[..., ::-1], ((0, 0), (0, 0), (0, 1)))
    skew = jnp.broadcast_to(rev[:, :, None, :], (h, nd, tq, 2 * tq)).reshape(h, nd, tq * 2 * tq)
    skew = skew[..., :tq * span].reshape(h, nd, tq, span)
    return skew[..., tq - 1:span]


def _attn_prompt_kernel(q_ref, k_ref, v_ref, bias_ref, o_ref, kb_ref, vb_ref, *, tq):
    i = pl.program_id(2)

    @pl.when(i == 0)
    def _():
        kb_ref[...] = k_ref[...].astype(BF16)
        vb_ref[...] = v_ref[...].astype(BF16)

    q = (q_ref[...] * (HEAD_DIM ** -0.5)).astype(BF16)

    def body(j, carry):
        m, l, acc = carry
        start = pl.multiple_of(j * tq, tq)
        s = _dot_nt(q, kb_ref[pl.ds(start, tq), :]) + bias_ref[i - j]
        m_new = jnp.maximum(m, jnp.max(s, -1, keepdims=True))
        p = jnp.exp(s - m_new)
        a = jnp.exp(m - m_new)
        l = a * l + jnp.sum(p, -1, keepdims=True)
        acc = a * acc + _dot(p.astype(BF16), vb_ref[pl.ds(start, tq), :])
        return m_new, l, acc

    init = (jnp.full((tq, 1), NEG, F32), jnp.zeros((tq, 1), F32), jnp.zeros((tq, HEAD_DIM), F32))
    _, l, acc = lax.fori_loop(0, i + 1, body, init)
    o_ref[...] = (acc / l).astype(o_ref.dtype)


def _attn_prompt(proj, rel_bias, n_heads, *, tq):
    b, s, _ = proj.shape
    nd = s // tq
    table = _toeplitz_tiles(_merged_bias(rel_bias, jnp.arange(-(tq - 1), s)), nd, tq)
    return pl.pallas_call(
        functools.partial(_attn_prompt_kernel, tq=tq),
        grid=(n_heads, b, nd),
        in_specs=[
            pl.BlockSpec((None, tq, HEAD_DIM), lambda h, bb, i: (bb, i, h)),
            pl.BlockSpec((None, s, HEAD_DIM), lambda h, bb, i: (bb, 0, n_heads + h)),
            pl.BlockSpec((None, s, HEAD_DIM), lambda h, bb, i: (bb, 0, 2 * n_heads + h)),
            pl.BlockSpec((None, nd, tq, tq), lambda h, bb, i: (h, 0, 0, 0)),
        ],
        out_specs=pl.BlockSpec((None, tq, HEAD_DIM), lambda h, bb, i: (bb, i, h)),
        out_shape=jax.ShapeDtypeStruct((b, s, n_heads * HEAD_DIM), BF16),
        scratch_shapes=[pltpu.VMEM((s, HEAD_DIM), BF16), pltpu.VMEM((s, HEAD_DIM), BF16)],
        compiler_params=_params("arbitrary", "arbitrary", "arbitrary"),
        name="attn_prompt",
    )(proj, proj, proj, table)


QUERY_PAD = 16


def _attn_sample_kernel(q_ref, kn_ref, vn_ref, kc_ref, vc_ref, bias_ref, o_ref, kx_ref, vx_ref, *, hg, t):
    p = kc_ref.shape[0]
    n_keys, w = kx_ref.shape
    zrow = jnp.zeros((n_keys - p - t, w), F32)
    kx_ref[0:p, :] = kc_ref[...].astype(BF16)
    vx_ref[0:p, :] = vc_ref[...].astype(BF16)
    kx_ref[p:n_keys, :] = jnp.concatenate([kn_ref[...], zrow], 0).astype(BF16)
    vx_ref[p:n_keys, :] = jnp.concatenate([vn_ref[...], zrow], 0).astype(BF16)
    n_pat = bias_ref.shape[0]
    add = lambda a, b: a + b
    for h in range(hg):
        cs = slice(h * HEAD_DIM, (h + 1) * HEAD_DIM)
        q16 = jnp.concatenate([q_ref[:, cs], jnp.zeros((QUERY_PAD - t, HEAD_DIM), F32)], 0).astype(BF16)
        s = _dot_nt(q16, kx_ref[:, cs])[0:t, :] * (HEAD_DIM ** -0.5)
        outs, lses = [], []
        for g in range(n_pat):
            lg = s + bias_ref[g, h]
            m = jnp.max(lg, -1, keepdims=True)
            e = jnp.exp(lg - m)
            den = jnp.sum(e, -1, keepdims=True)
            p16 = jnp.concatenate([e / den, jnp.zeros((QUERY_PAD - t, n_keys), F32)], 0).astype(BF16)
            outs.append(_dot(p16, vx_ref[:, cs])[0:t, :])
            lses.append(m + jnp.log(den))
        top = functools.reduce(jnp.maximum, lses)
        ws = [jnp.exp(l - top) for l in lses]
        tot = functools.reduce(add, ws)
        o_ref[:, cs] = functools.reduce(add, [(wg / tot) * og for wg, og in zip(ws, outs)])


def _attn_sample(proj, cache_k, cache_v, rel_bias, n_heads, t, *, hg):
    bt = proj.shape[0]
    b = bt // t
    p = cache_k.shape[1]
    w = hg * HEAD_DIM
    nhg = n_heads // hg
    assert t <= QUERY_PAD and p % QUERY_PAD == 0
    n_keys = p + QUERY_PAD
    key = jnp.arange(n_keys)
    dist = jnp.where(key[None, :] < p + t, p + jnp.arange(t)[:, None] - key[None, :], -1)
    bias = _pattern_bias(rel_bias, dist)
    n_pat = bias.shape[0]
    return pl.pallas_call(
        functools.partial(_attn_sample_kernel, hg=hg, t=t),
        grid=(b, nhg),
        in_specs=[
            pl.BlockSpec((t, w), lambda bb, g: (bb, g)),
            pl.BlockSpec((t, w), lambda bb, g: (bb, nhg + g)),
            pl.BlockSpec((t, w), lambda bb, g: (bb, 2 * nhg + g)),
            pl.BlockSpec((None, p, w), lambda bb, g: (bb, 0, g)),
            pl.BlockSpec((None, p, w), lambda bb, g: (bb, 0, g)),
            pl.BlockSpec((n_pat, hg, t, n_keys), lambda bb, g: (0, g, 0, 0)),
        ],
        out_specs=pl.BlockSpec((t, w), lambda bb, g: (bb, g)),
        out_shape=jax.ShapeDtypeStruct((bt, n_heads * HEAD_DIM), F32),
        scratch_shapes=[pltpu.VMEM((n_keys, w), BF16), pltpu.VMEM((n_keys, w), BF16)],
        compiler_params=_params("arbitrary", "arbitrary"),
        name="attn_sample",
    )(proj, proj, proj, cache_k, cache_v, bias)


def _round_bf16(x):
    return x.astype(BF16).astype(F32)


def _conv_kernel(a_ref, g_ref, st_ref, w_ref, cb_ref, ng_ref, nb_ref, o_ref, tail_ref, ext_ref, y_ref,
                 extr_ref, *, t, rc, cc):
    ti = pl.program_id(1)
    ch = a_ref.shape[-1]

    @pl.when(ti == 0)
    def _():
        ext_ref[0:CONV_HALO, :] = st_ref[...]

    ext_ref[CONV_HALO:CONV_HALO + t, :] = a_ref[...] * _sigmoid(g_ref[...])
    extr_ref[...] = _round_bf16(ext_ref[...])
    first = CONV_HALO - (CONV_WIDTH - 1)
    for r0 in range(0, t, rc):
        for c0 in range(0, ch, cc):
            acc = jnp.zeros((rc, cc), F32)
            for j in range(CONV_WIDTH):
                acc = acc + w_ref[j:j + 1, c0:c0 + cc] * extr_ref[first + r0 + j:first + r0 + j + rc, c0:c0 + cc]
            y_ref[r0:r0 + rc, c0:c0 + cc] = acc + cb_ref[:, c0:c0 + cc]
    for r0 in range(0, t, rc):
        y = _layer_norm_rows(y_ref[r0:r0 + rc, :], ng_ref[...], nb_ref[...])
        o_ref[r0:r0 + rc, :] = _silu(y).astype(o_ref.dtype)

    @pl.when(ti == pl.num_programs(1) - 1)
    def _():
        tail_ref[...] = ext_ref[t + first:t + CONV_HALO, :]

    ext_ref[0:CONV_HALO, :] = ext_ref[t:t + CONV_HALO, :]


def _conv_module(proj, a_blk, state, conv_w, conv_b, ng, nb, *, ch, t, out_dtype, name):
    b, length, _ = proj.shape
    keep = CONV_WIDTH - 1
    st = jnp.pad(state, ((0, 0), (CONV_HALO - keep, 0), (0, 0)))
    rc = min(t, 32)
    vec_spec = pl.BlockSpec((1, ch), lambda bb, i: (0, 0))
    return pl.pallas_call(
        functools.partial(_conv_kernel, t=t, rc=rc, cc=512),
        grid=(b, length // t),
        in_specs=[
            pl.BlockSpec((None, t, ch), lambda bb, i: (bb, i, a_blk)),
            pl.BlockSpec((None, t, ch), lambda bb, i: (bb, i, a_blk + 1)),
            pl.BlockSpec((None, CONV_HALO, ch), lambda bb, i: (bb, 0, 0)),
            pl.BlockSpec((CONV_WIDTH, ch), lambda bb, i: (0, 0)),
            vec_spec, vec_spec, vec_spec,
        ],
        out_specs=[
            pl.BlockSpec((None, t, ch), lambda bb, i: (bb, i, 0)),
            pl.BlockSpec((None, keep, ch), lambda bb, i: (bb, 0, 0)),
        ],
        out_shape=[jax.ShapeDtypeStruct((b, length, ch), out_dtype),
                   jax.ShapeDtypeStruct((b, keep, ch), F32)],
        scratch_shapes=[pltpu.VMEM((CONV_HALO + t, ch), F32), pltpu.VMEM((t, ch), F32),
                        pltpu.VMEM((CONV_HALO + t, ch), F32)],
        compiler_params=_params("arbitrary", "arbitrary"),
        name=name,
    )(proj, proj, st, conv_w, conv_b.reshape(1, ch), ng.reshape(1, ch), nb.reshape(1, ch))


def _split3(x):
    a1 = x.astype(BF16)
    r1 = x - a1.astype(F32)
    a2 = r1.astype(BF16)
    a3 = (r1 - a2.astype(F32)).astype(BF16)
    return a1, a2, a3


def _hgrn_kernel(q_ref, f_ref, i_ref, g_ref, lb_ref, ng_ref, s0_ref, o_ref, sfin_ref, st_ref,
                 *, hg, t, c, pairwise):
    ti = pl.program_id(2)
    dk = HEAD_DIM
    hi = lax.Precision.HIGHEST

    @pl.when(ti == 0)
    def _():
        for h in range(hg):
            st_ref[h] = s0_ref[h].T

    lb = lb_ref[...]
    ng = ng_ref[...]
    causal = lax.broadcasted_iota(jnp.int32, (c, c), 0) >= lax.broadcasted_iota(jnp.int32, (c, c), 1)
    step = lax.broadcasted_iota(jnp.int32, (c, 1), 0)
    for c0 in range(0, t, c):
        rows = slice(c0, c0 + c)
        fg = lb + (1.0 - lb) * _sigmoid(f_ref[rows, :])
        kk = 1.0 - fg
        qq = _silu(q_ref[rows, :])
        logf = jnp.log(fg)
        if pairwise:
            acc = [logf[0:1, :]]
            for r in range(1, c):
                acc.append(acc[-1] + logf[r:r + 1, :])
            cum = jnp.concatenate(acc, 0)
        else:
            l1, l2, l3 = _split3(logf)
            tri = causal.astype(BF16)
            cum = _dot(tri, l1) + _dot(tri, l2) + _dot(tri, l3)
            mid = cum[c // 2 - 1:c // 2, :]
            q_md = (qq * jnp.exp(cum - mid)).astype(BF16)
            k_md = (kk * jnp.exp(mid - cum)).astype(BF16)
        last = cum[c - 1:c, :]
        q_in = qq * jnp.exp(cum)
        k_end = kk * jnp.exp(last - cum)
        dec = jnp.exp(last)
        for h in range(hg):
            cs = slice(h * dk, (h + 1) * dk)
            st = st_ref[h]
            if pairwise:
                v = _round_bf16(i_ref[rows, cs])
                o_rows = []
                for r in range(c):
                    pair = qq[r:r + 1, cs] * jnp.exp(cum[r:r + 1, cs] - cum[:, cs]) * kk[:, cs]
                    sc = jnp.where(step <= r, jnp.sum(pair, -1, keepdims=True), 0.0)
                    o_rows.append(jnp.sum(_round_bf16(sc) * v, 0, keepdims=True))
                o = _dot_nt(_round_bf16(q_in[:, cs]), _round_bf16(st), hi) + jnp.concatenate(o_rows, 0)
                st_ref[h] = st * dec[:, cs] + _dot_tn(v, _round_bf16(k_end[:, cs]), hi)
            else:
                v = i_ref[rows, cs].astype(BF16)
                sc = jnp.where(causal, _dot_nt(q_md[:, cs], k_md[:, cs]), 0.0)
                o = _dot_nt(q_in[:, cs].astype(BF16), st.astype(BF16)) + _dot(sc.astype(BF16), v)
                st_ref[h] = st * dec[:, cs] + _dot_tn(v, k_end[:, cs].astype(BF16))
            normed = o * lax.rsqrt(jnp.mean(o * o, -1, keepdims=True) + EPS) * ng
            o_ref[rows, cs] = (normed * _silu(g_ref[rows, cs])).astype(o_ref.dtype)

    @pl.when(ti == pl.num_programs(2) - 1)
    def _():
        for h in range(hg):
            sfin_ref[h] = st_ref[h].T


def _hgrn(proj, s0, lb, ng, n_heads, *, hg, t, c, pairwise, out_dtype, name):
    b, length, _ = proj.shape
    w = hg * HEAD_DIM
    nhg = n_heads // hg
    col = lambda k: pl.BlockSpec((None, t, w), lambda bb, g, i: (bb, i, k * nhg + g))
    st_spec = pl.BlockSpec((None, hg, HEAD_DIM, HEAD_DIM), lambda bb, g, i: (bb, g, 0, 0))
    return pl.pallas_call(
        functools.partial(_hgrn_kernel, hg=hg, t=t, c=c, pairwise=pairwise),
        grid=(b, nhg, length // t),
        in_specs=[col(0), col(1), col(2), col(3),
                  pl.BlockSpec((1, w), lambda bb, g, i: (0, g)),
                  pl.BlockSpec((1, HEAD_DIM), lambda bb, g, i: (0, 0)),
                  st_spec],
        out_specs=[pl.BlockSpec((None, t, w), lambda bb, g, i: (bb, i, g)), st_spec],
        out_shape=[jax.ShapeDtypeStruct((b, length, n_heads * HEAD_DIM), out_dtype),
                   jax.ShapeDtypeStruct(s0.shape, F32)],
        scratch_shapes=[pltpu.VMEM((hg, HEAD_DIM, HEAD_DIM), F32)],
        compiler_params=_params("arbitrary", "arbitrary", "arbitrary"),
        name=name,
    )(proj, proj, proj, proj, lb.reshape(1, -1), ng.reshape(1, HEAD_DIM), s0)


def _top2(vals):
    n = len(vals)
    m1 = functools.reduce(jnp.maximum, vals)
    i1 = jnp.full(m1.shape, n - 1, jnp.int32)
    for k in range(n - 2, -1, -1):
        i1 = jnp.where(vals[k] == m1, k, i1)
    rest = [jnp.where(i1 == k, -1.0, vals[k]) for k in range(n)]
    m2 = functools.reduce(jnp.maximum, rest)
    i2 = jnp.full(m1.shape, n - 1, jnp.int32)
    for k in range(n - 2, -1, -1):
        i2 = jnp.where(rest[k] == m2, k, i2)
    return m1, i1, m2, i2


def _router_kernel(x_ref, w_ref, b_ref, ex_ref, gt_ref):
    logits = _dot_nt(w_ref[...].astype(BF16), x_ref[...]) + b_ref[...]
    e = jnp.exp(logits - jnp.max(logits, 0, keepdims=True))
    p = e / jnp.sum(e, 0, keepdims=True)
    members = [p[k * N_GROUPS:(k + 1) * N_GROUPS, :] for k in range(GROUP_SIZE)]
    m1, i1, m2, i2 = _top2(members)
    score = m1 + m2
    gid = lax.broadcasted_iota(jnp.int32, score.shape, 0)
    best = jnp.max(score, 0, keepdims=True)
    gsel = jnp.min(jnp.where(score == best, gid, N_GROUPS), 0, keepdims=True)
    sel = gid == gsel
    pick_f = lambda a: jnp.sum(jnp.where(sel, a, 0.0), 0, keepdims=True)
    pick_i = lambda a: jnp.sum(jnp.where(sel, a, 0), 0, keepdims=True)
    p1, p2 = pick_f(m1), pick_f(m2)
    ex_ref[0:1, :] = gsel * GROUP_SIZE + pick_i(i1)
    ex_ref[1:2, :] = gsel * GROUP_SIZE + pick_i(i2)
    gt_ref[0:1, :] = p1 / (p1 + p2)
    gt_ref[1:2, :] = p2 / (p1 + p2)


def _router(x, w_t, b_col, *, tm, name):
    n, d = x.shape
    out_spec = pl.BlockSpec((TOP_K, tm), lambda i: (0, i))
    return pl.pallas_call(
        _router_kernel,
        grid=(n // tm,),
        in_specs=[pl.BlockSpec((tm, d), lambda i: (i, 0)),
                  pl.BlockSpec((N_EXPERTS, d), lambda i: (0, 0)),
                  pl.BlockSpec((N_EXPERTS, 1), lambda i: (0, 0))],
        out_specs=[out_spec, out_spec],
        out_shape=[jax.ShapeDtypeStruct((TOP_K, n), jnp.int32), jax.ShapeDtypeStruct((TOP_K, n), F32)],
        compiler_params=_params("arbitrary"),
        name=name,
    )(x, w_t, b_col)


MOE_SLOT_ROWS = 640


def _slot_index(s, j, nu_ref, nj):
    used = s < nu_ref[0]
    return jnp.minimum(s, nu_ref[0] - 1), jnp.where(used, j, nj - 1)


def _gate_up_kernel(se_ref, ss_ref, sr_ref, nu_ref, tok_ref, x_hbm, wg_ref, wu_ref, h_ref,
                    xg_ref, xb_ref, sem):
    s = pl.program_id(0)
    j = pl.program_id(1)
    n_used = nu_ref[0]

    def row_copy(slot, r, buf):
        tok = tok_ref[ss_ref[slot] + r]
        return pltpu.make_async_copy(x_hbm.at[pl.ds(tok, 1), :], xg_ref.at[buf, pl.ds(r, 1), :], sem.at[buf])

    def for_rows(slot, fn):
        def body(r, carry):
            fn(r)
            return carry
        lax.fori_loop(0, sr_ref[slot], body, 0)

    @pl.when(jnp.logical_and(s == 0, j == 0))
    def _():
        xg_ref[...] = jnp.zeros(xg_ref.shape, xg_ref.dtype)
        for_rows(0, lambda r: row_copy(0, r, 0).start())

    @pl.when(jnp.logical_and(s < n_used, j == 0))
    def _():
        buf = s % 2
        for_rows(s, lambda r: row_copy(s, r, buf).wait())
        xb_ref[...] = xg_ref[buf].astype(BF16)

        @pl.when(s + 1 < n_used)
        def _():
            for_rows(s + 1, lambda r: row_copy(s + 1, r, 1 - buf).start())

    @pl.when(s < n_used)
    def _():
        x = xb_ref[...]
        gate = _dot(x, wg_ref[...].astype(BF16))
        up = _dot(x, wu_ref[...].astype(BF16))
        h_ref[...] = (_silu(gate) * up).astype(h_ref.dtype)


def _down_kernel(se_ref, nu_ref, h_ref, wd_ref, y_ref):
    @pl.when(pl.program_id(0) < nu_ref[0])
    def _():
        y_ref[...] = _dot(h_ref[...], wd_ref[...].astype(BF16))


def _expert_ffn(x_rows, slot_expert, slot_start, slot_rows, n_used, tok_sorted, w_gate_up, w_down, layer,
                *, th, tn):
    n, d = x_rows.shape
    d_exp = w_down.shape[2]
    n_slots = slot_expert.shape[0]
    rows = n_slots * MOE_SLOT_ROWS
    nj = d_exp // th

    def w_map(col0):
        def index(s, j, se, ss, sr, nu, tok):
            se_, je_ = _slot_index(s, j, nu, nj)
            return (layer, se[se_], 0, col0 + je_)
        return index

    def h_map(s, j, se, ss, sr, nu, tok):
        return _slot_index(s, j, nu, nj)

    h = pl.pallas_call(
        _gate_up_kernel,
        grid_spec=pltpu.PrefetchScalarGridSpec(
            num_scalar_prefetch=5,
            grid=(n_slots, nj),
            in_specs=[
                pl.BlockSpec(memory_space=pl.ANY),
                pl.BlockSpec((None, None, d, th), w_map(0)),
                pl.BlockSpec((None, None, d, th), w_map(nj)),
            ],
            out_specs=pl.BlockSpec((MOE_SLOT_ROWS, th), h_map),
            scratch_shapes=[pltpu.VMEM((2, MOE_SLOT_ROWS, d), F32), pltpu.VMEM((MOE_SLOT_ROWS, d), BF16),
                            pltpu.SemaphoreType.DMA((2,))],
        ),
        out_shape=jax.ShapeDtypeStruct((rows, d_exp), BF16),
        compiler_params=_params("arbitrary", "arbitrary"),
        name=f"moe_gate_up_{layer}",
    )(slot_expert, slot_start, slot_rows, n_used, tok_sorted, x_rows, w_gate_up, w_gate_up)

    nj2 = d // tn

    def wd_map(s, j, se, nu):
        se_, je_ = _slot_index(s, j, nu, nj2)
        return (layer, se[se_], 0, je_)

    return pl.pallas_call(
        _down_kernel,
        grid_spec=pltpu.PrefetchScalarGridSpec(
            num_scalar_prefetch=2,
            grid=(n_slots, nj2),
            in_specs=[
                pl.BlockSpec((MOE_SLOT_ROWS, d_exp), lambda s, j, se, nu: (_slot_index(s, j, nu, nj2)[0], 0)),
                pl.BlockSpec((None, None, d_exp, tn), wd_map),
            ],
            out_specs=pl.BlockSpec((MOE_SLOT_ROWS, tn), lambda s, j, se, nu: _slot_index(s, j, nu, nj2)),
        ),
        out_shape=jax.ShapeDtypeStruct((rows, d), F32),
        compiler_params=_params("arbitrary", "arbitrary"),
        name=f"moe_down_{layer}",
    )(slot_expert, n_used, h, w_down)


def _moe_layer(x_rows, experts, gates, w_gate_up, w_down, layer, ln_g, ln_b, *, tm_ln):
    n, d = x_rows.shape
    n_assign = n * TOP_K
    r = MOE_SLOT_ROWS
    flat_e = experts.reshape(-1)
    order = jnp.argsort(flat_e).astype(jnp.int32)
    e_sorted = flat_e[order]
    tok_sorted = order // TOP_K
    counts = jnp.bincount(flat_e, length=N_EXPERTS).astype(jnp.int32)
    starts = jnp.cumsum(counts) - counts
    slots_per = (counts + r - 1) // r
    slot_end = jnp.cumsum(slots_per)
    slot_first = slot_end - slots_per
    n_slots = -(-n_assign // r) + N_EXPERTS
    sid = jnp.arange(n_slots, dtype=jnp.int32)
    slot_expert = jnp.minimum(jnp.searchsorted(slot_end, sid, side='right'), N_EXPERTS - 1).astype(jnp.int32)
    local = sid - slot_first[slot_expert]
    slot_start = (starts[slot_expert] + local * r).astype(jnp.int32)
    slot_rows = jnp.where(sid < slot_end[-1], jnp.clip(counts[slot_expert] - local * r, 0, r), 0).astype(jnp.int32)
    n_used = slot_end[-1].astype(jnp.int32).reshape(1)
    rank = jnp.arange(n_assign, dtype=jnp.int32) - starts[e_sorted]
    dest_sorted = (slot_first[e_sorted] + rank // r) * r + rank % r
    dest = jnp.zeros((n_assign,), jnp.int32).at[order].set(dest_sorted).reshape(n, TOP_K)
    ys = _expert_ffn(x_rows, slot_expert, slot_start, slot_rows, n_used, tok_sorted, w_gate_up, w_down, layer,
                     th=256, tn=1024)
    y0 = jnp.take(ys, dest[:, 0], axis=0, mode="clip")
    y1 = jnp.take(ys, dest[:, 1], axis=0, mode="clip")
    return _ln_moe(x_rows, y0, y1, gates, ln_g, ln_b, tm=tm_ln, name=f"ln_moe_{layer}")


def kernel(x_prompt, x_sample, cache_win_k, cache_win_v, state_conv, state_hgrn, w_in_ab, rel_bias, conv_w,
           conv_b, conv_norm_g, conv_norm_b, w_out_ab, w_in_c, hgrn_lb, hgrn_norm_g, w_out_c, ln_g, ln_b,
           router_w, router_b, w_gate_up, w_down):
    bp, sp, d = x_prompt.shape
    bs, ts, _ = x_sample.shape
    n_p, n_s = bp * sp, bs * ts
    attn_w = d // 2
    n_ah = attn_w // HEAD_DIM
    conv_ch = d - attn_w
    n_hh = d // HEAD_DIM
    keep = CONV_WIDTH - 1

    lb_soft = jax.nn.softmax(hgrn_lb.astype(F32), axis=0)
    lower_bounds = jnp.cumsum(lb_soft, axis=0) - lb_soft[0]
    perm = (jnp.arange(N_EXPERTS) % N_GROUPS) * GROUP_SIZE + jnp.arange(N_EXPERTS) // N_GROUPS
    router_wt = router_w.T[perm]
    router_bc = router_b.astype(F32)[perm].reshape(N_EXPERTS, 1)

    xp = x_prompt.reshape(n_p, d)
    xs = x_sample.reshape(n_s, d)
    xpb = xp.astype(BF16)
    xsb = xs.astype(BF16)
    outs = {}
    for l in range(DEPTH):
        if l % 2 == 0:
            a = l // 2
            n_in = w_in_ab.shape[2]
            proj_p = _matmul([xpb], w_in_ab, a, 0, n_in, tm=512, tn=512, name=f"ab_in_p{l}")
            proj_s = _matmul([xsb], w_in_ab, a, 0, n_in, tm=n_s, tn=512, name=f"ab_in_s{l}")
            pp3 = proj_p.reshape(bp, sp, n_in)
            attn_p = _attn_prompt(pp3, rel_bias, n_ah, tq=256)
            conv_p, tail_p = _conv_module(pp3, 3 * attn_w // conv_ch, jnp.zeros((bp, keep, conv_ch), F32),
                                          conv_w[a], conv_b[a], conv_norm_g[a], conv_norm_b[a],
                                          ch=conv_ch, t=128, out_dtype=BF16, name=f"conv_p{l}")
            attn_s = _attn_sample(proj_s, cache_win_k[a].reshape(bs, -1, attn_w),
                                  cache_win_v[a].reshape(bs, -1, attn_w), rel_bias, n_ah, ts, hg=4)
            conv_s, tail_s = _conv_module(proj_s.reshape(bs, ts, n_in), 3 * attn_w // conv_ch, state_conv[a],
                                          conv_w[a], conv_b[a], conv_norm_g[a], conv_norm_b[a],
                                          ch=conv_ch, t=ts, out_dtype=F32, name=f"conv_s{l}")
            m_p = _matmul([attn_p.reshape(n_p, attn_w), conv_p.reshape(n_p, conv_ch)], w_out_ab, a, 0, d,
                          tm=512, tn=512, name=f"ab_out_p{l}")
            m_s = _matmul([attn_s.astype(BF16), conv_s.reshape(n_s, conv_ch).astype(BF16)], w_out_ab, a, 0, d,
                          tm=n_s, tn=512, name=f"ab_out_s{l}")
            outs.setdefault("wk_p", []).append(proj_p[:, attn_w:2 * attn_w].reshape(bp, sp, n_ah, HEAD_DIM))
            outs.setdefault("wv_p", []).append(proj_p[:, 2 * attn_w:3 * attn_w].reshape(bp, sp, n_ah, HEAD_DIM))
            outs.setdefault("cv_p", []).append(tail_p)
            outs.setdefault("wk_s", []).append(proj_s[:, attn_w:2 * attn_w].reshape(bs, ts, n_ah, HEAD_DIM))
            outs.setdefault("wv_s", []).append(proj_s[:, 2 * attn_w:3 * attn_w].reshape(bs, ts, n_ah, HEAD_DIM))
            outs.setdefault("cv_s", []).append(tail_s)
        else:
            c = l // 2
            n_in = w_in_c.shape[2]
            proj_p = _matmul([xpb], w_in_c, c, 0, n_in, tm=512, tn=512, name=f"c_in_p{l}")
            proj_s = _matmul([xsb], w_in_c, c, 0, n_in, tm=n_s, tn=512, name=f"c_in_s{l}")
            o_p, h_p = _hgrn(proj_p.reshape(bp, sp, n_in), jnp.zeros((bp, n_hh, HEAD_DIM, HEAD_DIM), F32),
                             lower_bounds[l], hgrn_norm_g[c], n_hh, hg=4, t=256, c=64, pairwise=False,
                             out_dtype=BF16, name=f"hgrn_p{l}")
            o_s, h_s = _hgrn(proj_s.reshape(bs, ts, n_in), state_hgrn[c], lower_bounds[l], hgrn_norm_g[c],
                             n_hh, hg=4, t=ts, c=ts, pairwise=True, out_dtype=F32, name=f"hgrn_s{l}")
            m_p = _matmul([o_p.reshape(n_p, d)], w_out_c, c, 0, d, tm=512, tn=512, name=f"c_out_p{l}")
            m_s = _matmul([o_s.reshape(n_s, d).astype(BF16)], w_out_c, c, 0, d, tm=n_s, tn=512, name=f"c_out_s{l}")
            outs.setdefault("hs_p", []).append(h_p)
            outs.setdefault("hs_s", []).append(h_s)
        xp, xpb = _ln_res(xp, m_p, ln_g[l, 0], ln_b[l, 0], tm=256, name=f"ln_mix_p{l}")
        xs, xsb = _ln_res(xs, m_s, ln_g[l, 0], ln_b[l, 0], tm=n_s, name=f"ln_mix_s{l}")
        ex_p, gt_p = _router(xpb, router_wt, router_bc, tm=512, name=f"router_p{l}")
        ex_s, gt_s = _router(xsb, router_wt, router_bc, tm=n_s, name=f"router_s{l}")
        experts = jnp.concatenate([ex_p, ex_s], axis=1).T
        gates = jnp.concatenate([gt_p, gt_s], axis=1).T
        tok = jnp.concatenate([xp, xs], axis=0)
        yf, yb = _moe_layer(tok, experts, gates, w_gate_up, w_down, l, ln_g[l, 1], ln_b[l, 1], tm_ln=64)
        xp, xs = yf[:n_p], yf[n_p:]
        xpb, xsb = yb[:n_p], yb[n_p:]
    stack = lambda k: jnp.stack(outs[k])
    return (xp.reshape(bp, sp, d), xs.reshape(bs, ts, d), stack("wk_p"), stack("wv_p"), stack("cv_p"),
            stack("hs_p"), stack("wk_s"), stack("wv_s"), stack("cv_s"), stack("hs_s"))
```

```python
import functools
import math

import jax
import jax.numpy as jnp
from jax import lax
from jax.experimental import pallas as pl
from jax.experimental.pallas import tpu as pltpu

F32 = jnp.float32
BF16 = jnp.bfloat16

DEPTH = 2
HEAD_DIM = 128
DILATED_PATTERNS = ((128, 1), (512, 4), (2048, 16))
NUM_BUCKETS = 32
MAX_EXACT = 16
MAX_DISTANCE = 2048
CONV_WIDTH = 31
N_EXPERTS = 32
N_GROUPS = 8
GROUP_SIZE = N_EXPERTS // N_GROUPS
TOP_K = 2
ALPHA = (2.0 * DEPTH) ** 0.25
EPS = 1e-5
NEG = -1e30
VMEM_LIMIT = 56 * 1024 * 1024
CONV_HALO = 32


def _params(*sem):
    return pltpu.CompilerParams(dimension_semantics=sem, vmem_limit_bytes=VMEM_LIMIT)


def _sigmoid(x):
    return 1.0 / (1.0 + jnp.exp(-x))


def _silu(x):
    return x * _sigmoid(x)


def _dot_nt(a, b, precision=None):
    return lax.dot_general(a, b, (((1,), (1,)), ((), ())), preferred_element_type=F32, precision=precision)


def _dot_tn(a, b, precision=None):
    return lax.dot_general(a, b, (((0,), (0,)), ((), ())), preferred_element_type=F32, precision=precision)


def _dot(a, b, precision=None):
    return jnp.dot(a, b, preferred_element_type=F32, precision=precision)


def _mm_kernel(*refs, n_x):
    x_refs = refs[:n_x]
    w_ref, o_ref, wb_ref = refs[n_x:]

    @pl.when(pl.program_id(1) == 0)
    def _():
        wb_ref[...] = w_ref[...].astype(BF16)

    acc = None
    off = 0
    for xr in x_refs:
        k = xr.shape[-1]
        part = _dot(xr[...], wb_ref[off:off + k, :])
        acc = part if acc is None else acc + part
        off += k
    o_ref[...] = acc.astype(o_ref.dtype)


def _matmul(xs, w, layer, col_off, n_cols, *, tm, tn, name):
    m_rows = xs[0].shape[0]
    k_tot = sum(x.shape[1] for x in xs)
    assert w.shape[1] == k_tot and m_rows % tm == 0 and n_cols % tn == 0 and col_off % tn == 0
    coff = col_off // tn
    in_specs = [pl.BlockSpec((tm, x.shape[1]), lambda n, m: (m, 0)) for x in xs]
    in_specs.append(pl.BlockSpec((None, k_tot, tn), lambda n, m: (layer, 0, n + coff)))
    return pl.pallas_call(
        functools.partial(_mm_kernel, n_x=len(xs)),
        grid=(n_cols // tn, m_rows // tm),
        in_specs=in_specs,
        out_specs=pl.BlockSpec((tm, tn), lambda n, m: (m, n)),
        out_shape=jax.ShapeDtypeStruct((m_rows, n_cols), F32),
        scratch_shapes=[pltpu.VMEM((k_tot, tn), BF16)],
        compiler_params=_params("arbitrary", "arbitrary"),
        name=name,
    )(*xs, w)


def _layer_norm_rows(z, g, b):
    mu = jnp.mean(z, -1, keepdims=True)
    zc = z - mu
    var = jnp.mean(zc * zc, -1, keepdims=True)
    return zc * lax.rsqrt(var + EPS) * g + b


def _ln_res_kernel(x_ref, m_ref, g_ref, b_ref, of_ref, ob_ref):
    y = _layer_norm_rows(ALPHA * x_ref[...] + m_ref[...], g_ref[...], b_ref[...])
    of_ref[...] = y
    ob_ref[...] = y.astype(BF16)


def _ln_res(x, m, g, b, *, tm, name):
    rows, d = x.shape
    row_spec = pl.BlockSpec((tm, d), lambda i: (i, 0))
    vec_spec = pl.BlockSpec((1, d), lambda i: (0, 0))
    return pl.pallas_call(
        _ln_res_kernel,
        grid=(rows // tm,),
        in_specs=[row_spec, row_spec, vec_spec, vec_spec],
        out_specs=[row_spec, row_spec],
        out_shape=[jax.ShapeDtypeStruct((rows, d), F32), jax.ShapeDtypeStruct((rows, d), BF16)],
        compiler_params=_params("arbitrary"),
        name=name,
    )(x, m, g.reshape(1, d), b.reshape(1, d))


def _ln_moe_kernel(x_ref, y0_ref, y1_ref, gt_ref, g_ref, b_ref, of_ref, ob_ref):
    gt = gt_ref[...]
    ff = gt[:, 0:1] * y0_ref[...].astype(F32) + gt[:, 1:2] * y1_ref[...].astype(F32)
    y = _layer_norm_rows(ALPHA * x_ref[...] + ff, g_ref[...], b_ref[...])
    of_ref[...] = y
    ob_ref[...] = y.astype(BF16)


def _ln_moe(x, y0, y1, gates, g, b, *, tm, name):
    rows, d = x.shape
    row_spec = pl.BlockSpec((tm, d), lambda i: (i, 0))
    vec_spec = pl.BlockSpec((1, d), lambda i: (0, 0))
    return pl.pallas_call(
        _ln_moe_kernel,
        grid=(rows // tm,),
        in_specs=[row_spec, row_spec, row_spec, pl.BlockSpec((tm, TOP_K), lambda i: (i, 0)), vec_spec, vec_spec],
        out_specs=[row_spec, row_spec],
        out_shape=[jax.ShapeDtypeStruct((rows, d), F32), jax.ShapeDtypeStruct((rows, d), BF16)],
        compiler_params=_params("arbitrary"),
        name=name,
    )(x, y0, y1, gates, g.reshape(1, d), b.reshape(1, d))


def _rel_bucket(dist):
    dist = dist.astype(jnp.int32)
    d = jnp.maximum(dist, 1).astype(F32)
    large = MAX_EXACT + (jnp.log(d / MAX_EXACT) / math.log(MAX_DISTANCE / MAX_EXACT)
                         * (NUM_BUCKETS - MAX_EXACT)).astype(jnp.int32)
    large = jnp.minimum(large, NUM_BUCKETS - 1)
    return jnp.where(dist < MAX_EXACT, dist, large)


def _pattern_bias(rel_bias, d):
    base = jnp.moveaxis(rel_bias[_rel_bucket(jnp.maximum(d, 0))].astype(F32), -1, 0)
    outs = []
    for w, r in DILATED_PATTERNS:
        ok = (d >= 0) & (d % r == 0) & (d <= w)
        outs.append(jnp.where(ok[None], base, NEG))
    return jnp.stack(outs)


def _merged_bias(rel_bias, d):
    count = sum(((d >= 0) & (d % r == 0) & (d <= w)).astype(F32) for w, r in DILATED_PATTERNS)
    base = jnp.moveaxis(rel_bias[_rel_bucket(jnp.maximum(d, 0))].astype(F32), -1, 0)
    return jnp.where((count > 0)[None], base + jnp.log(jnp.maximum(count, 1.0))[None], NEG)


def _toeplitz_tiles(vec, nd, tq):
    h = vec.shape[0]
    span = 2 * tq - 1
    win = jnp.stack([vec[:, k * tq:k * tq + span] for k in range(nd)], 1)
    rev = jnp.pad(win[..., ::-1], ((0, 0), (0, 0), (0, 1)))
    skew = jnp.broadcast_to(rev[:, :, None, :], (h, nd, tq, 2 * tq)).reshape(h, nd, tq * 2 * tq)
    skew = skew[..., :tq * span].reshape(h, nd, tq, span)
    return skew[..., tq - 1:span]


def _attn_prompt_kernel(q_ref, k_ref, v_ref, bias_ref, o_ref, kb_ref, vb_ref, *, tq):
    i = pl.program_id(1)
    nb = q_ref.shape[0]

    @pl.when(i == 0)
    def _():
        kb_ref[...] = k_ref[...].astype(BF16)
        vb_ref[...] = v_ref[...].astype(BF16)

    qs = [(q_ref[b] * (HEAD_DIM ** -0.5)).astype(BF16) for b in range(nb)]

    def body(j, carry):
        start = pl.multiple_of(j * tq, tq)
        bias = bias_ref[i - j]
        out = []
        for b in range(nb):
            m, l, acc = carry[b]
            s = _dot_nt(qs[b], kb_ref[b, pl.ds(start, tq), :]) + bias
            m_new = jnp.maximum(m, jnp.max(s, -1, keepdims=True))
            p = jnp.exp(s - m_new)
            a = jnp.exp(m - m_new)
            l = a * l + jnp.sum(p, -1, keepdims=True)
            acc = a * acc + _dot(p.astype(BF16), vb_ref[b, pl.ds(start, tq), :])
            out.append((m_new, l, acc))
        return tuple(out)

    one = (jnp.full((tq, 1), NEG, F32), jnp.zeros((tq, 1), F32), jnp.zeros((tq, HEAD_DIM), F32))
    res = lax.fori_loop(0, i + 1, body, (one,) * nb)
    for b in range(nb):
        _, l, acc = res[b]
        o_ref[b] = (acc / l).astype(o_ref.dtype)


def _attn_prompt(proj, rel_bias, n_heads, *, tq):
    b, s, _ = proj.shape
    nd = s // tq
    table = _toeplitz_tiles(_merged_bias(rel_bias, jnp.arange(-(tq - 1), s)), nd, tq)
    return pl.pallas_call(
        functools.partial(_attn_prompt_kernel, tq=tq),
        grid=(n_heads, nd),
        in_specs=[
            pl.BlockSpec((b, tq, HEAD_DIM), lambda h, i: (0, i, h)),
            pl.BlockSpec((b, s, HEAD_DIM), lambda h, i: (0, 0, n_heads + h)),
            pl.BlockSpec((b, s, HEAD_DIM), lambda h, i: (0, 0, 2 * n_heads + h)),
            pl.BlockSpec((None, nd, tq, tq), lambda h, i: (h, 0, 0, 0)),
        ],
        out_specs=pl.BlockSpec((b, tq, HEAD_DIM), lambda h, i: (0, i, h)),
        out_shape=jax.ShapeDtypeStruct((b, s, n_heads * HEAD_DIM), BF16),
        scratch_shapes=[pltpu.VMEM((b, s, HEAD_DIM), BF16), pltpu.VMEM((b, s, HEAD_DIM), BF16)],
        compiler_params=_params("arbitrary", "arbitrary"),
        name="attn_prompt",
    )(proj, proj, proj, table)


QUERY_PAD = 16


def _attn_sample_kernel(q_ref, kn_ref, vn_ref, kc_ref, vc_ref, bias_ref, o_ref, kx_ref, vx_ref, *, hg, t):
    p = kc_ref.shape[0]
    n_keys, w = kx_ref.shape
    zrow = jnp.zeros((n_keys - p - t, w), F32)
    kx_ref[0:p, :] = kc_ref[...].astype(BF16)
    vx_ref[0:p, :] = vc_ref[...].astype(BF16)
    kx_ref[p:n_keys, :] = jnp.concatenate([kn_ref[...], zrow], 0).astype(BF16)
    vx_ref[p:n_keys, :] = jnp.concatenate([vn_ref[...], zrow], 0).astype(BF16)
    n_pat = bias_ref.shape[0]
    add = lambda a, b: a + b
    for h in range(hg):
        cs = slice(h * HEAD_DIM, (h + 1) * HEAD_DIM)
        q16 = jnp.concatenate([q_ref[:, cs], jnp.zeros((QUERY_PAD - t, HEAD_DIM), F32)], 0).astype(BF16)
        s = _dot_nt(q16, kx_ref[:, cs])[0:t, :] * (HEAD_DIM ** -0.5)
        outs, lses = [], []
        for g in range(n_pat):
            lg = s + bias_ref[g, h]
            m = jnp.max(lg, -1, keepdims=True)
            e = jnp.exp(lg - m)
            den = jnp.sum(e, -1, keepdims=True)
            p16 = jnp.concatenate([e / den, jnp.zeros((QUERY_PAD - t, n_keys), F32)], 0).astype(BF16)
            outs.append(_dot(p16, vx_ref[:, cs])[0:t, :])
            lses.append(m + jnp.log(den))
        top = functools.reduce(jnp.maximum, lses)
        ws = [jnp.exp(l - top) for l in lses]
        tot = functools.reduce(add, ws)
        o_ref[:, cs] = functools.reduce(add, [(wg / tot) * og for wg, og in zip(ws, outs)])


def _attn_sample(proj, cache_k, cache_v, rel_bias, n_heads, t, *, hg):
    bt = proj.shape[0]
    b = bt // t
    p = cache_k.shape[1]
    w = hg * HEAD_DIM
    nhg = n_heads // hg
    assert t <= QUERY_PAD and p % QUERY_PAD == 0
    n_keys = p + QUERY_PAD
    key = jnp.arange(n_keys)
    dist = jnp.where(key[None, :] < p + t, p + jnp.arange(t)[:, None] - key[None, :], -1)
    bias = _pattern_bias(rel_bias, dist)
    n_pat = bias.shape[0]
    return pl.pallas_call(
        functools.partial(_attn_sample_kernel, hg=hg, t=t),
        grid=(b, nhg),
        in_specs=[
            pl.BlockSpec((t, w), lambda bb, g: (bb, g)),
            pl.BlockSpec((t, w), lambda bb, g: (bb, nhg + g)),
            pl.BlockSpec((t, w), lambda bb, g: (bb, 2 * nhg + g)),
            pl.BlockSpec((None, p, w), lambda bb, g: (bb, 0, g)),
            pl.BlockSpec((None, p, w), lambda bb, g: (bb, 0, g)),
            pl.BlockSpec((n_pat, hg, t, n_keys), lambda bb, g: (0, g, 0, 0)),
        ],
        out_specs=pl.BlockSpec((t, w), lambda bb, g: (bb, g)),
        out_shape=jax.ShapeDtypeStruct((bt, n_heads * HEAD_DIM), F32),
        scratch_shapes=[pltpu.VMEM((n_keys, w), BF16), pltpu.VMEM((n_keys, w), BF16)],
        compiler_params=_params("arbitrary", "arbitrary"),
        name="attn_sample",
    )(proj, proj, proj, cache_k, cache_v, bias)


def _round_bf16(x):
    return x.astype(BF16).astype(F32)


def _conv_kernel(a_ref, g_ref, st_ref, w_ref, cb_ref, ng_ref, nb_ref, o_ref, tail_ref, ext_ref, y_ref,
                 extr_ref, *, t, rc, cc):
    ti = pl.program_id(1)
    ch = a_ref.shape[-1]

    @pl.when(ti == 0)
    def _():
        ext_ref[0:CONV_HALO, :] = st_ref[...]

    ext_ref[CONV_HALO:CONV_HALO + t, :] = a_ref[...] * _sigmoid(g_ref[...])
    extr_ref[...] = _round_bf16(ext_ref[...])
    first = CONV_HALO - (CONV_WIDTH - 1)
    for r0 in range(0, t, rc):
        for c0 in range(0, ch, cc):
            acc = jnp.zeros((rc, cc), F32)
            for j in range(CONV_WIDTH):
                acc = acc + w_ref[j:j + 1, c0:c0 + cc] * extr_ref[first + r0 + j:first + r0 + j + rc, c0:c0 + cc]
            y_ref[r0:r0 + rc, c0:c0 + cc] = acc + cb_ref[:, c0:c0 + cc]
    for r0 in range(0, t, rc):
        y = _layer_norm_rows(y_ref[r0:r0 + rc, :], ng_ref[...], nb_ref[...])
        o_ref[r0:r0 + rc, :] = _silu(y).astype(o_ref.dtype)

    @pl.when(ti == pl.num_programs(1) - 1)
    def _():
        tail_ref[...] = ext_ref[t + first:t + CONV_HALO, :]

    ext_ref[0:CONV_HALO, :] = ext_ref[t:t + CONV_HALO, :]


def _conv_module(proj, a_blk, state, conv_w, conv_b, ng, nb, *, ch, t, out_dtype, name):
    b, length, _ = proj.shape
    keep = CONV_WIDTH - 1
    st = jnp.pad(state, ((0, 0), (CONV_HALO - keep, 0), (0, 0)))
    rc = min(t, 32)
    vec_spec = pl.BlockSpec((1, ch), lambda bb, i: (0, 0))
    return pl.pallas_call(
        functools.partial(_conv_kernel, t=t, rc=rc, cc=512),
        grid=(b, length // t),
        in_specs=[
            pl.BlockSpec((None, t, ch), lambda bb, i: (bb, i, a_blk)),
            pl.BlockSpec((None, t, ch), lambda bb, i: (bb, i, a_blk + 1)),
            pl.BlockSpec((None, CONV_HALO, ch), lambda bb, i: (bb, 0, 0)),
            pl.BlockSpec((CONV_WIDTH, ch), lambda bb, i: (0, 0)),
            vec_spec, vec_spec, vec_spec,
        ],
        out_specs=[
            pl.BlockSpec((None, t, ch), lambda bb, i: (bb, i, 0)),
            pl.BlockSpec((None, keep, ch), lambda bb, i: (bb, 0, 0)),
        ],
        out_shape=[jax.ShapeDtypeStruct((b, length, ch), out_dtype),
                   jax.ShapeDtypeStruct((b, keep, ch), F32)],
        scratch_shapes=[pltpu.VMEM((CONV_HALO + t, ch), F32), pltpu.VMEM((t, ch), F32),
                        pltpu.VMEM((CONV_HALO + t, ch), F32)],
        compiler_params=_params("arbitrary", "arbitrary"),
        name=name,
    )(proj, proj, st, conv_w, conv_b.reshape(1, ch), ng.reshape(1, ch), nb.reshape(1, ch))


def _split3(x):
    a1 = x.astype(BF16)
    r1 = x - a1.astype(F32)
    a2 = r1.astype(BF16)
    a3 = (r1 - a2.astype(F32)).astype(BF16)
    return a1, a2, a3


def _hgrn_kernel(q_ref, f_ref, i_ref, g_ref, lb_ref, ng_ref, s0_ref, o_ref, sfin_ref, st_ref,
                 *, hg, t, c, pairwise):
    ti = pl.program_id(2)
    dk = HEAD_DIM
    hi = lax.Precision.HIGHEST

    @pl.when(ti == 0)
    def _():
        for h in range(hg):
            st_ref[h] = s0_ref[h].T

    lb = lb_ref[...]
    ng = ng_ref[...]
    causal = lax.broadcasted_iota(jnp.int32, (c, c), 0) >= lax.broadcasted_iota(jnp.int32, (c, c), 1)
    step = lax.broadcasted_iota(jnp.int32, (c, 1), 0)
    for c0 in range(0, t, c):
        rows = slice(c0, c0 + c)
        fg = lb + (1.0 - lb) * _sigmoid(f_ref[rows, :])
        kk = 1.0 - fg
        qq = _silu(q_ref[rows, :])
        logf = jnp.log(fg)
        if pairwise:
            acc = [logf[0:1, :]]
            for r in range(1, c):
                acc.append(acc[-1] + logf[r:r + 1, :])
            cum = jnp.concatenate(acc, 0)
        else:
            l1, l2, l3 = _split3(logf)
            tri = causal.astype(BF16)
            cum = _dot(tri, l1) + _dot(tri, l2) + _dot(tri, l3)
            mid = cum[c // 2 - 1:c // 2, :]
            q_md = (qq * jnp.exp(cum - mid)).astype(BF16)
            k_md = (kk * jnp.exp(mid - cum)).astype(BF16)
        last = cum[c - 1:c, :]
        q_in = qq * jnp.exp(cum)
        k_end = kk * jnp.exp(last - cum)
        dec = jnp.exp(last)
        for h in range(hg):
            cs = slice(h * dk, (h + 1) * dk)
            st = st_ref[h]
            if pairwise:
                v = _round_bf16(i_ref[rows, cs])
                o_rows = []
                for r in range(c):
                    pair = qq[r:r + 1, cs] * jnp.exp(cum[r:r + 1, cs] - cum[:, cs]) * kk[:, cs]
                    sc = jnp.where(step <= r, jnp.sum(pair, -1, keepdims=True), 0.0)
                    o_rows.append(jnp.sum(_round_bf16(sc) * v, 0, keepdims=True))
                o = _dot_nt(_round_bf16(q_in[:, cs]), _round_bf16(st), hi) + jnp.concatenate(o_rows, 0)
                st_ref[h] = st * dec[:, cs] + _dot_tn(v, _round_bf16(k_end[:, cs]), hi)
            else:
                v = i_ref[rows, cs].astype(BF16)
                sc = jnp.where(causal, _dot_nt(q_md[:, cs], k_md[:, cs]), 0.0)
                o = _dot_nt(q_in[:, cs].astype(BF16), st.astype(BF16)) + _dot(sc.astype(BF16), v)
                st_ref[h] = st * dec[:, cs] + _dot_tn(v, k_end[:, cs].astype(BF16))
            normed = o * lax.rsqrt(jnp.mean(o * o, -1, keepdims=True) + EPS) * ng
            o_ref[rows, cs] = (normed * _silu(g_ref[rows, cs])).astype(o_ref.dtype)

    @pl.when(ti == pl.num_programs(2) - 1)
    def _():
        for h in range(hg):
            sfin_ref[h] = st_ref[h].T


def _hgrn(proj, s0, lb, ng, n_heads, *, hg, t, c, pairwise, out_dtype, name):
    b, length, _ = proj.shape
    w = hg * HEAD_DIM
    nhg = n_heads // hg
    col = lambda k: pl.BlockSpec((None, t, w), lambda bb, g, i: (bb, i, k * nhg + g))
    st_spec = pl.BlockSpec((None, hg, HEAD_DIM, HEAD_DIM), lambda bb, g, i: (bb, g, 0, 0))
    return pl.pallas_call(
        functools.partial(_hgrn_kernel, hg=hg, t=t, c=c, pairwise=pairwise),
        grid=(b, nhg, length // t),
        in_specs=[col(0), col(1), col(2), col(3),
                  pl.BlockSpec((1, w), lambda bb, g, i: (0, g)),
                  pl.BlockSpec((1, HEAD_DIM), lambda bb, g, i: (0, 0)),
                  st_spec],
        out_specs=[pl.BlockSpec((None, t, w), lambda bb, g, i: (bb, i, g)), st_spec],
        out_shape=[jax.ShapeDtypeStruct((b, length, n_heads * HEAD_DIM), out_dtype),
                   jax.ShapeDtypeStruct(s0.shape, F32)],
        scratch_shapes=[pltpu.VMEM((hg, HEAD_DIM, HEAD_DIM), F32)],
        compiler_params=_params("arbitrary", "arbitrary", "arbitrary"),
        name=name,
    )(proj, proj, proj, proj, lb.reshape(1, -1), ng.reshape(1, HEAD_DIM), s0)


def _top2(vals):
    n = len(vals)
    m1 = functools.reduce(jnp.maximum, vals)
    i1 = jnp.full(m1.shape, n - 1, jnp.int32)
    for k in range(n - 2, -1, -1):
        i1 = jnp.where(vals[k] == m1, k, i1)
    rest = [jnp.where(i1 == k, -1.0, vals[k]) for k in range(n)]
    m2 = functools.reduce(jnp.maximum, rest)
    i2 = jnp.full(m1.shape, n - 1, jnp.int32)
    for k in range(n - 2, -1, -1):
        i2 = jnp.where(rest[k] == m2, k, i2)
    return m1, i1, m2, i2


def _router_kernel(x_ref, w_ref, b_ref, ex_ref, gt_ref):
    logits = _dot_nt(w_ref[...].astype(BF16), x_ref[...]) + b_ref[...]
    e = jnp.exp(logits - jnp.max(logits, 0, keepdims=True))
    p = e / jnp.sum(e, 0, keepdims=True)
    members = [p[k * N_GROUPS:(k + 1) * N_GROUPS, :] for k in range(GROUP_SIZE)]
    m1, i1, m2, i2 = _top2(members)
    score = m1 + m2
    gid = lax.broadcasted_iota(jnp.int32, score.shape, 0)
    best = jnp.max(score, 0, keepdims=True)
    gsel = jnp.min(jnp.where(score == best, gid, N_GROUPS), 0, keepdims=True)
    sel = gid == gsel
    pick_f = lambda a: jnp.sum(jnp.where(sel, a, 0.0), 0, keepdims=True)
    pick_i = lambda a: jnp.sum(jnp.where(sel, a, 0), 0, keepdims=True)
    p1, p2 = pick_f(m1), pick_f(m2)
    ex_ref[0:1, :] = gsel * GROUP_SIZE + pick_i(i1)
    ex_ref[1:2, :] = gsel * GROUP_SIZE + pick_i(i2)
    gt_ref[0:1, :] = p1 / (p1 + p2)
    gt_ref[1:2, :] = p2 / (p1 + p2)


def _router(x, w_t, b_col, *, tm, name):
    n, d = x.shape
    out_spec = pl.BlockSpec((TOP_K, tm), lambda i: (0, i))
    return pl.pallas_call(
        _router_kernel,
        grid=(n // tm,),
        in_specs=[pl.BlockSpec((tm, d), lambda i: (i, 0)),
                  pl.BlockSpec((N_EXPERTS, d), lambda i: (0, 0)),
                  pl.BlockSpec((N_EXPERTS, 1), lambda i: (0, 0))],
        out_specs=[out_spec, out_spec],
        out_shape=[jax.ShapeDtypeStruct((TOP_K, n), jnp.int32), jax.ShapeDtypeStruct((TOP_K, n), F32)],
        compiler_params=_params("arbitrary"),
        name=name,
    )(x, w_t, b_col)


MOE_SLOT_ROWS = 576


def _slot_index(s, j, nu_ref, nj):
    used = s < nu_ref[0]
    return jnp.minimum(s, nu_ref[0] - 1), jnp.where(used, j, nj - 1)


def _gate_up_kernel(se_ref, ss_ref, nu_ref, tok_ref, x_hbm, wg_ref, wu_ref, h_ref, xg_ref, xb_ref, sem,
                    *, n_slots, nj):
    s = pl.program_id(0)
    j = pl.program_id(1)
    n_used = nu_ref[0]
    rows = xb_ref.shape[0]
    part = rows // nj

    def start_rows(slot, first, count):
        buf = slot % 2
        base = ss_ref[slot] + first
        for r in range(count):
            tok = tok_ref[base + r]
            pltpu.make_async_copy(x_hbm.at[pl.ds(tok, 1), :], xg_ref.at[buf, pl.ds(first + r, 1), :],
                                  sem.at[buf]).start()

    def wait_buffer(slot):
        buf = slot % 2
        pltpu.make_async_copy(x_hbm.at[pl.ds(0, rows), :], xg_ref.at[buf], sem.at[buf]).wait()

    @pl.when(jnp.logical_and(s == 0, j == 0))
    def _():
        start_rows(0, 0, rows)

    @pl.when(jnp.logical_and(s <= n_used, j == 0))
    def _():
        wait_buffer(s)

    @pl.when(jnp.logical_and(s < n_used, j == 0))
    def _():
        xb_ref[...] = xg_ref[s % 2].astype(BF16)

    @pl.when(s < n_used)
    def _():
        start_rows(s + 1, j * part, part)
        x = xb_ref[...]
        gate = _dot(x, wg_ref[...].astype(BF16))
        up = _dot(x, wu_ref[...].astype(BF16))
        h_ref[...] = (_silu(gate) * up).astype(h_ref.dtype)

    @pl.when(jnp.logical_and(jnp.logical_and(s == n_slots - 1, j == nj - 1), n_used == n_slots))
    def _():
        wait_buffer(n_slots)


def _down_kernel(se_ref, nu_ref, h_ref, wd_ref, y_ref):
    @pl.when(pl.program_id(0) < nu_ref[0])
    def _():
        y_ref[...] = _dot(h_ref[...], wd_ref[...].astype(BF16))


def _expert_ffn(x_rows, slot_expert, slot_start, n_used, tok_sorted, w_gate_up, w_down, layer, *, th, tn):
    n, d = x_rows.shape
    d_exp = w_down.shape[2]
    n_slots = slot_expert.shape[0] - 1
    rows = n_slots * MOE_SLOT_ROWS
    nj = d_exp // th

    def w_map(col0):
        def index(s, j, se, ss, nu, tok):
            se_, je_ = _slot_index(s, j, nu, nj)
            return (layer, se[se_], 0, col0 + je_)
        return index

    def h_map(s, j, se, ss, nu, tok):
        return _slot_index(s, j, nu, nj)

    h = pl.pallas_call(
        functools.partial(_gate_up_kernel, n_slots=n_slots, nj=nj),
        grid_spec=pltpu.PrefetchScalarGridSpec(
            num_scalar_prefetch=4,
            grid=(n_slots, nj),
            in_specs=[
                pl.BlockSpec(memory_space=pl.ANY),
                pl.BlockSpec((None, None, d, th), w_map(0)),
                pl.BlockSpec((None, None, d, th), w_map(nj)),
            ],
            out_specs=pl.BlockSpec((MOE_SLOT_ROWS, th), h_map),
            scratch_shapes=[pltpu.VMEM((2, MOE_SLOT_ROWS, d), F32), pltpu.VMEM((MOE_SLOT_ROWS, d), BF16),
                            pltpu.SemaphoreType.DMA((2,))],
        ),
        out_shape=jax.ShapeDtypeStruct((rows, d_exp), BF16),
        compiler_params=_params("arbitrary", "arbitrary"),
        name=f"moe_gate_up_{layer}",
    )(slot_expert, slot_start, n_used, tok_sorted, x_rows, w_gate_up, w_gate_up)

    nj2 = d // tn

    def wd_map(s, j, se, nu):
        se_, je_ = _slot_index(s, j, nu, nj2)
        return (layer, se[se_], 0, je_)

    return pl.pallas_call(
        _down_kernel,
        grid_spec=pltpu.PrefetchScalarGridSpec(
            num_scalar_prefetch=2,
            grid=(n_slots, nj2),
            in_specs=[
                pl.BlockSpec((MOE_SLOT_ROWS, d_exp), lambda s, j, se, nu: (_slot_index(s, j, nu, nj2)[0], 0)),
                pl.BlockSpec((None, None, d_exp, tn), wd_map),
            ],
            out_specs=pl.BlockSpec((MOE_SLOT_ROWS, tn), lambda s, j, se, nu: _slot_index(s, j, nu, nj2)),
        ),
        out_shape=jax.ShapeDtypeStruct((rows, d), F32),
        compiler_params=_params("arbitrary", "arbitrary"),
        name=f"moe_down_{layer}",
    )(slot_expert, n_used, h, w_down)


def _moe_layer(x_rows, experts, gates, w_gate_up, w_down, layer, ln_g, ln_b, *, tm_ln):
    n, d = x_rows.shape
    n_assign = n * TOP_K
    r = MOE_SLOT_ROWS
    flat_e = experts.reshape(-1)
    order = jnp.argsort(flat_e).astype(jnp.int32)
    e_sorted = flat_e[order]
    tok_sorted = order // TOP_K
    counts = jnp.bincount(flat_e, length=N_EXPERTS).astype(jnp.int32)
    starts = jnp.cumsum(counts) - counts
    slots_per = (counts + r - 1) // r
    slot_end = jnp.cumsum(slots_per)
    slot_first = slot_end - slots_per
    n_slots = -(-n_assign // r) + N_EXPERTS
    sid = jnp.arange(n_slots + 1, dtype=jnp.int32)
    used = sid < slot_end[-1]
    slot_expert = jnp.minimum(jnp.searchsorted(slot_end, sid, side='right'), N_EXPERTS - 1).astype(jnp.int32)
    local = sid - slot_first[slot_expert]
    slot_start = jnp.where(used, starts[slot_expert] + local * r, 0).astype(jnp.int32)
    n_used = slot_end[-1].astype(jnp.int32).reshape(1)
    tok_sorted = jnp.pad(tok_sorted, (0, r))
    rank = jnp.arange(n_assign, dtype=jnp.int32) - starts[e_sorted]
    dest_sorted = (slot_first[e_sorted] + rank // r) * r + rank % r
    dest = jnp.zeros((n_assign,), jnp.int32).at[order].set(dest_sorted).reshape(n, TOP_K)
    ys = _expert_ffn(x_rows, slot_expert, slot_start, n_used, tok_sorted, w_gate_up, w_down, layer, th=256, tn=1024)
    y0 = jnp.take(ys, dest[:, 0], axis=0, mode="clip")
    y1 = jnp.take(ys, dest[:, 1], axis=0, mode="clip")
    return _ln_moe(x_rows, y0, y1, gates, ln_g, ln_b, tm=tm_ln, name=f"ln_moe_{layer}")


def kernel(x_prompt, x_sample, cache_win_k, cache_win_v, state_conv, state_hgrn, w_in_ab, rel_bias, conv_w,
           conv_b, conv_norm_g, conv_norm_b, w_out_ab, w_in_c, hgrn_lb, hgrn_norm_g, w_out_c, ln_g, ln_b,
           router_w, router_b, w_gate_up, w_down):
    bp, sp, d = x_prompt.shape
    bs, ts, _ = x_sample.shape
    n_p, n_s = bp * sp, bs * ts
    attn_w = d // 2
    n_ah = attn_w // HEAD_DIM
    conv_ch = d - attn_w
    n_hh = d // HEAD_DIM
    keep = CONV_WIDTH - 1

    lb_soft = jax.nn.softmax(hgrn_lb.astype(F32), axis=0)
    lower_bounds = jnp.cumsum(lb_soft, axis=0) - lb_soft[0]
    perm = (jnp.arange(N_EXPERTS) % N_GROUPS) * GROUP_SIZE + jnp.arange(N_EXPERTS) // N_GROUPS
    router_wt = router_w.T[perm]
    router_bc = router_b.astype(F32)[perm].reshape(N_EXPERTS, 1)

    xp = x_prompt.reshape(n_p, d)
    xs = x_sample.reshape(n_s, d)
    xpb = xp.astype(BF16)
    xsb = xs.astype(BF16)
    outs = {}
    for l in range(DEPTH):
        if l % 2 == 0:
            a = l // 2
            n_in = w_in_ab.shape[2]
            proj_p = _matmul([xpb], w_in_ab, a, 0, n_in, tm=512, tn=512, name=f"ab_in_p{l}")
            proj_s = _matmul([xsb], w_in_ab, a, 0, n_in, tm=n_s, tn=512, name=f"ab_in_s{l}")
            pp3 = proj_p.reshape(bp, sp, n_in)
            attn_p = _attn_prompt(pp3, rel_bias, n_ah, tq=256)
            conv_p, tail_p = _conv_module(pp3, 3 * attn_w // conv_ch, jnp.zeros((bp, keep, conv_ch), F32),
                                          conv_w[a], conv_b[a], conv_norm_g[a], conv_norm_b[a],
                                          ch=conv_ch, t=128, out_dtype=BF16, name=f"conv_p{l}")
            attn_s = _attn_sample(proj_s, cache_win_k[a].reshape(bs, -1, attn_w),
                                  cache_win_v[a].reshape(bs, -1, attn_w), rel_bias, n_ah, ts, hg=4)
            conv_s, tail_s = _conv_module(proj_s.reshape(bs, ts, n_in), 3 * attn_w // conv_ch, state_conv[a],
                                          conv_w[a], conv_b[a], conv_norm_g[a], conv_norm_b[a],
                                          ch=conv_ch, t=ts, out_dtype=F32, name=f"conv_s{l}")
            m_p = _matmul([attn_p.reshape(n_p, attn_w), conv_p.reshape(n_p, conv_ch)], w_out_ab, a, 0, d,
                          tm=512, tn=512, name=f"ab_out_p{l}")
            m_s = _matmul([attn_s.astype(BF16), conv_s.reshape(n_s, conv_ch).astype(BF16)], w_out_ab, a, 0, d,
                          tm=n_s, tn=512, name=f"ab_out_s{l}")
            outs.setdefault("wk_p", []).append(proj_p[:, attn_w:2 * attn_w].reshape(bp, sp, n_ah, HEAD_DIM))
            outs.setdefault("wv_p", []).append(proj_p[:, 2 * attn_w:3 * attn_w].reshape(bp, sp, n_ah, HEAD_DIM))
            outs.setdefault("cv_p", []).append(tail_p)
            outs.setdefault("wk_s", []).append(proj_s[:, attn_w:2 * attn_w].reshape(bs, ts, n_ah, HEAD_DIM))
            outs.setdefault("wv_s", []).append(proj_s[:, 2 * attn_w:3 * attn_w].reshape(bs, ts, n_ah, HEAD_DIM))
            outs.setdefault("cv_s", []).append(tail_s)
        else:
            c = l // 2
            n_in = w_in_c.shape[2]
            proj_p = _matmul([xpb], w_in_c, c, 0, n_in, tm=512, tn=512, name=f"c_in_p{l}")
            proj_s = _matmul([xsb], w_in_c, c, 0, n_in, tm=n_s, tn=512, name=f"c_in_s{l}")
            o_p, h_p = _hgrn(proj_p.reshape(bp, sp, n_in), jnp.zeros((bp, n_hh, HEAD_DIM, HEAD_DIM), F32),
                             lower_bounds[l], hgrn_norm_g[c], n_hh, hg=4, t=256, c=64, pairwise=False,
                             out_dtype=BF16, name=f"hgrn_p{l}")
            o_s, h_s = _hgrn(proj_s.reshape(bs, ts, n_in), state_hgrn[c], lower_bounds[l], hgrn_norm_g[c],
                             n_hh, hg=4, t=ts, c=ts, pairwise=True, out_dtype=F32, name=f"hgrn_s{l}")
            m_p = _matmul([o_p.reshape(n_p, d)], w_out_c, c, 0, d, tm=512, tn=512, name=f"c_out_p{l}")
            m_s = _matmul([o_s.reshape(n_s, d).astype(BF16)], w_out_c, c, 0, d, tm=n_s, tn=512, name=f"c_out_s{l}")
            outs.setdefault("hs_p", []).append(h_p)
            outs.setdefault("hs_s", []).append(h_s)
        xp, xpb = _ln_res(xp, m_p, ln_g[l, 0], ln_b[l, 0], tm=256, name=f"ln_mix_p{l}")
        xs, xsb = _ln_res(xs, m_s, ln_g[l, 0], ln_b[l, 0], tm=n_s, name=f"ln_mix_s{l}")
        ex_p, gt_p = _router(xpb, router_wt, router_bc, tm=512, name=f"router_p{l}")
        ex_s, gt_s = _router(xsb, router_wt, router_bc, tm=n_s, name=f"router_s{l}")
        experts = jnp.concatenate([ex_p, ex_s], axis=1).T
        gates = jnp.concatenate([gt_p, gt_s], axis=1).T
        tok = jnp.concatenate([xp, xs], axis=0)
        yf, yb = _moe_layer(tok, experts, gates, w_gate_up, w_down, l, ln_g[l, 1], ln_b[l, 1], tm_ln=64)
        xp, xs = yf[:n_p], yf[n_p:]
        xpb, xsb = yb[:n_p], yb[n_p:]
    stack = lambda k: jnp.stack(outs[k])
    return (xp.reshape(bp, sp, d), xs.reshape(bs, ts, d), stack("wk_p"), stack("wv_p"), stack("cv_p"),
            stack("hs_p"), stack("wk_s"), stack("wv_s"), stack("cv_s"), stack("hs_s"))
```

```python
import functools
import math

import jax
import jax.numpy as jnp
from jax import lax
from jax.experimental import pallas as pl
from jax.experimental.pallas import tpu as pltpu

F32 = jnp.float32
BF16 = jnp.bfloat16

DEPTH = 2
HEAD_DIM = 128
DILATED_PATTERNS = ((128, 1), (512, 4), (2048, 16))
NUM_BUCKETS = 32
MAX_EXACT = 16
MAX_DISTANCE = 2048
CONV_WIDTH = 31
N_EXPERTS = 32
N_GROUPS = 8
GROUP_SIZE = N_EXPERTS // N_GROUPS
TOP_K = 2
ALPHA = (2.0 * DEPTH) ** 0.25
EPS = 1e-5
NEG = -1e30
VMEM_LIMIT = 56 * 1024 * 1024
CONV_HALO = 32


def _params(*sem):
    return pltpu.CompilerParams(dimension_semantics=sem, vmem_limit_bytes=VMEM_LIMIT)


def _sigmoid(x):
    return 1.0 / (1.0 + jnp.exp(-x))


def _silu(x):
    return x * _sigmoid(x)


def _dot_nt(a, b, precision=None):
    return lax.dot_general(a, b, (((1,), (1,)), ((), ())), preferred_element_type=F32, precision=precision)


def _dot_tn(a, b, precision=None):
    return lax.dot_general(a, b, (((0,), (0,)), ((), ())), preferred_element_type=F32, precision=precision)


def _dot(a, b, precision=None):
    return jnp.dot(a, b, preferred_element_type=F32, precision=precision)


def _mm_kernel(*refs, n_x):
    x_refs = refs[:n_x]
    w_ref = refs[n_x]
    o_ref, wb_ref = refs[-2:]

    @pl.when(pl.program_id(1) == 0)
    def _():
        wb_ref[...] = w_ref[...].astype(BF16)

    acc = None
    off = 0
    for xr in x_refs:
        k = xr.shape[-1]
        part = _dot(xr[...], wb_ref[off:off + k, :])
        acc = part if acc is None else acc + part
        off += k
    o_ref[...] = acc.astype(o_ref.dtype)


def _matmul(xs, w, layer, n_cols, *, tm, tn, name, rows, x_row0=0, out_rows=None, out_row0=0, into=None):
    k_tot = sum(x.shape[1] for x in xs)
    assert w.shape[1] == k_tot and rows % tm == 0 and n_cols % tn == 0
    assert x_row0 % tm == 0 and out_row0 % tm == 0
    xoff, ooff = x_row0 // tm, out_row0 // tm
    out_rows = out_rows or rows
    in_specs = [pl.BlockSpec((tm, x.shape[1]), lambda n, m: (m + xoff, 0)) for x in xs]
    in_specs.append(pl.BlockSpec((None, k_tot, tn), lambda n, m: (layer, 0, n)))
    args = list(xs) + [w]
    aliases = {}
    if into is not None:
        assert into.shape == (out_rows, n_cols)
        in_specs.append(pl.BlockSpec(memory_space=pl.ANY))
        aliases = {len(args): 0}
        args.append(into)
    return pl.pallas_call(
        functools.partial(_mm_kernel, n_x=len(xs)),
        grid=(n_cols // tn, rows // tm),
        in_specs=in_specs,
        out_specs=pl.BlockSpec((tm, tn), lambda n, m: (m + ooff, n)),
        out_shape=jax.ShapeDtypeStruct((out_rows, n_cols), F32),
        scratch_shapes=[pltpu.VMEM((k_tot, tn), BF16)],
        input_output_aliases=aliases,
        compiler_params=_params("arbitrary", "arbitrary"),
        name=name,
    )(*args)


def _layer_norm_rows(z, g, b):
    mu = jnp.mean(z, -1, keepdims=True)
    zc = z - mu
    var = jnp.mean(zc * zc, -1, keepdims=True)
    return zc * lax.rsqrt(var + EPS) * g + b


LN_ROW_CHUNK = 32


def _ln_res_kernel(x_ref, m_ref, g_ref, b_ref, of_ref, ob_ref):
    for r0 in range(0, x_ref.shape[0], LN_ROW_CHUNK):
        rows = slice(r0, r0 + LN_ROW_CHUNK)
        y = _layer_norm_rows(ALPHA * x_ref[rows, :] + m_ref[rows, :], g_ref[...], b_ref[...])
        of_ref[rows, :] = y
        ob_ref[rows, :] = y.astype(BF16)


def _ln_res(x, m, g, b, *, tm, name):
    rows, d = x.shape
    row_spec = pl.BlockSpec((tm, d), lambda i: (i, 0))
    vec_spec = pl.BlockSpec((1, d), lambda i: (0, 0))
    return pl.pallas_call(
        _ln_res_kernel,
        grid=(pl.cdiv(rows, tm),),
        in_specs=[row_spec, row_spec, vec_spec, vec_spec],
        out_specs=[row_spec, row_spec],
        out_shape=[jax.ShapeDtypeStruct((rows, d), F32), jax.ShapeDtypeStruct((rows, d), BF16)],
        compiler_params=_params("arbitrary"),
        name=name,
    )(x, m, g.reshape(1, d), b.reshape(1, d))


def _ln_moe_kernel(dest_ref, x_ref, gt_ref, g_ref, b_ref, ys_hbm, of_ref, ob_ref, yg_ref, sem, *, tm, n_steps):
    i = pl.program_id(0)

    def start_rows(step):
        buf = step % 2
        base = step * (tm * TOP_K)
        for r in range(tm):
            for k in range(TOP_K):
                row = dest_ref[base + r * TOP_K + k]
                pltpu.make_async_copy(ys_hbm.at[pl.ds(row, 1), :], yg_ref.at[buf, pl.ds(k * tm + r, 1), :],
                                      sem.at[buf]).start()

    def wait_buffer(step):
        buf = step % 2
        pltpu.make_async_copy(ys_hbm.at[pl.ds(0, TOP_K * tm), :], yg_ref.at[buf], sem.at[buf]).wait()

    @pl.when(i == 0)
    def _():
        start_rows(0)

    wait_buffer(i)
    start_rows(i + 1)
    buf = i % 2
    for r0 in range(0, tm, LN_ROW_CHUNK):
        rows = slice(r0, r0 + LN_ROW_CHUNK)
        gt = gt_ref[rows, :]
        ff = (gt[:, 0:1] * yg_ref[buf, r0:r0 + LN_ROW_CHUNK, :]
              + gt[:, 1:2] * yg_ref[buf, tm + r0:tm + r0 + LN_ROW_CHUNK, :])
        y = _layer_norm_rows(ALPHA * x_ref[rows, :] + ff, g_ref[...], b_ref[...])
        of_ref[rows, :] = y
        ob_ref[rows, :] = y.astype(BF16)

    @pl.when(i == n_steps - 1)
    def _():
        wait_buffer(i + 1)


def _ln_moe(x, ys, dest, gates, g, b, *, tm, name):
    rows, d = x.shape
    n_steps = pl.cdiv(rows, tm)
    dest_flat = jnp.pad(dest.reshape(-1), (0, (n_steps + 1) * tm * TOP_K - rows * TOP_K))
    row_spec = pl.BlockSpec((tm, d), lambda i, dr: (i, 0))
    vec_spec = pl.BlockSpec((1, d), lambda i, dr: (0, 0))
    return pl.pallas_call(
        functools.partial(_ln_moe_kernel, tm=tm, n_steps=n_steps),
        grid_spec=pltpu.PrefetchScalarGridSpec(
            num_scalar_prefetch=1,
            grid=(n_steps,),
            in_specs=[row_spec, pl.BlockSpec((tm, TOP_K), lambda i, dr: (i, 0)), vec_spec, vec_spec,
                      pl.BlockSpec(memory_space=pl.ANY)],
            out_specs=[row_spec, row_spec],
            scratch_shapes=[pltpu.VMEM((2, TOP_K * tm, d), F32), pltpu.SemaphoreType.DMA((2,))],
        ),
        out_shape=[jax.ShapeDtypeStruct((rows, d), F32), jax.ShapeDtypeStruct((rows, d), BF16)],
        compiler_params=_params("arbitrary"),
        name=name,
    )(dest_flat, x, gates, g.reshape(1, d), b.reshape(1, d), ys)


def _rel_bucket(dist):
    dist = dist.astype(jnp.int32)
    d = jnp.maximum(dist, 1).astype(F32)
    large = MAX_EXACT + (jnp.log(d / MAX_EXACT) / math.log(MAX_DISTANCE / MAX_EXACT)
                         * (NUM_BUCKETS - MAX_EXACT)).astype(jnp.int32)
    large = jnp.minimum(large, NUM_BUCKETS - 1)
    return jnp.where(dist < MAX_EXACT, dist, large)


def _pattern_bias(rel_bias, d):
    base = jnp.moveaxis(rel_bias[_rel_bucket(jnp.maximum(d, 0))].astype(F32), -1, 0)
    outs = []
    for w, r in DILATED_PATTERNS:
        ok = (d >= 0) & (d % r == 0) & (d <= w)
        outs.append(jnp.where(ok[None], base, NEG))
    return jnp.stack(outs)


def _merged_bias(rel_bias, d):
    count = sum(((d >= 0) & (d % r == 0) & (d <= w)).astype(F32) for w, r in DILATED_PATTERNS)
    base = jnp.moveaxis(rel_bias[_rel_bucket(jnp.maximum(d, 0))].astype(F32), -1, 0)
    return jnp.where((count > 0)[None], base + jnp.log(jnp.maximum(count, 1.0))[None], NEG)


def _toeplitz_tiles(vec, nd, tq):
    h = vec.shape[0]
    span = 2 * tq - 1
    win = jnp.stack([vec[:, k * tq:k * tq + span] for k in range(nd)], 1)
    rev = jnp.pad(win[..., ::-1], ((0, 0), (0, 0), (0, 1)))
    skew = jnp.broadcast_to(rev[:, :, None, :], (h, nd, tq, 2 * tq)).reshape(h, nd, tq * 2 * tq)
    skew = skew[..., :tq * span].reshape(h, nd, tq, span)
    return skew[..., tq - 1:span]


def _attn_prompt_kernel(q_ref, k_ref, v_ref, bias_ref, o_ref, kb_ref, vb_ref, *, tq):
    i = pl.program_id(1)
    nb = q_ref.shape[0]

    @pl.when(i == 0)
    def _():
        kb_ref[...] = k_ref[...].astype(BF16)
        vb_ref[...] = v_ref[...].astype(BF16)

    qs = [(q_ref[b] * (HEAD_DIM ** -0.5)).astype(BF16) for b in range(nb)]

    def body(j, carry):
        start = pl.multiple_of(j * tq, tq)
        bias = bias_ref[i - j]
        out = []
        for b in range(nb):
            m, l, acc = carry[b]
            s = _dot_nt(qs[b], kb_ref[b, pl.ds(start, tq), :]) + bias
            m_new = jnp.maximum(m, jnp.max(s, -1, keepdims=True))
            p = jnp.exp(s - m_new)
            a = jnp.exp(m - m_new)
            l = a * l + jnp.sum(p, -1, keepdims=True)
            acc = a * acc + _dot(p.astype(BF16), vb_ref[b, pl.ds(start, tq), :])
            out.append((m_new, l, acc))
        return tuple(out)

    one = (jnp.full((tq, 1), NEG, F32), jnp.zeros((tq, 1), F32), jnp.zeros((tq, HEAD_DIM), F32))
    res = lax.fori_loop(0, i + 1, body, (one,) * nb)
    for b in range(nb):
        _, l, acc = res[b]
        o_ref[b] = (acc / l).astype(o_ref.dtype)


def _attn_prompt(proj, rel_bias, n_heads, *, tq):
    b, s, _ = proj.shape
    nd = s // tq
    table = _toeplitz_tiles(_merged_bias(rel_bias, jnp.arange(-(tq - 1), s)), nd, tq)
    return pl.pallas_call(
        functools.partial(_attn_prompt_kernel, tq=tq),
        grid=(n_heads, nd),
        in_specs=[
            pl.BlockSpec((b, tq, HEAD_DIM), lambda h, i: (0, i, h)),
            pl.BlockSpec((b, s, HEAD_DIM), lambda h, i: (0, 0, n_heads + h)),
            pl.BlockSpec((b, s, HEAD_DIM), lambda h, i: (0, 0, 2 * n_heads + h)),
            pl.BlockSpec((None, nd, tq, tq), lambda h, i: (h, 0, 0, 0)),
        ],
        out_specs=pl.BlockSpec((b, tq, HEAD_DIM), lambda h, i: (0, i, h)),
        out_shape=jax.ShapeDtypeStruct((b, s, n_heads * HEAD_DIM), BF16),
        scratch_shapes=[pltpu.VMEM((b, s, HEAD_DIM), BF16), pltpu.VMEM((b, s, HEAD_DIM), BF16)],
        compiler_params=_params("arbitrary", "arbitrary"),
        name="attn_prompt",
    )(proj, proj, proj, table)


QUERY_PAD = 16


def _attn_sample_kernel(q_ref, kn_ref, vn_ref, kc_ref, vc_ref, bias_ref, o_ref, kx_ref, vx_ref, *, hg, t):
    p = kc_ref.shape[0]
    n_keys, w = kx_ref.shape
    zrow = jnp.zeros((n_keys - p - t, w), F32)
    kx_ref[0:p, :] = kc_ref[...].astype(BF16)
    vx_ref[0:p, :] = vc_ref[...].astype(BF16)
    kx_ref[p:n_keys, :] = jnp.concatenate([kn_ref[...], zrow], 0).astype(BF16)
    vx_ref[p:n_keys, :] = jnp.concatenate([vn_ref[...], zrow], 0).astype(BF16)
    n_pat = bias_ref.shape[0]
    add = lambda a, b: a + b
    for h in range(hg):
        cs = slice(h * HEAD_DIM, (h + 1) * HEAD_DIM)
        q16 = jnp.concatenate([q_ref[:, cs], jnp.zeros((QUERY_PAD - t, HEAD_DIM), F32)], 0).astype(BF16)
        s = _dot_nt(q16, kx_ref[:, cs])[0:t, :] * (HEAD_DIM ** -0.5)
        outs, lses = [], []
        for g in range(n_pat):
            lg = s + bias_ref[g, h]
            m = jnp.max(lg, -1, keepdims=True)
            e = jnp.exp(lg - m)
            den = jnp.sum(e, -1, keepdims=True)
            p16 = jnp.concatenate([e / den, jnp.zeros((QUERY_PAD - t, n_keys), F32)], 0).astype(BF16)
            outs.append(_dot(p16, vx_ref[:, cs])[0:t, :])
            lses.append(m + jnp.log(den))
        top = functools.reduce(jnp.maximum, lses)
        ws = [jnp.exp(l - top) for l in lses]
        tot = functools.reduce(add, ws)
        o_ref[:, cs] = functools.reduce(add, [(wg / tot) * og for wg, og in zip(ws, outs)])


def _attn_sample(proj, cache_k, cache_v, rel_bias, n_heads, t, *, hg):
    bt = proj.shape[0]
    b = bt // t
    p = cache_k.shape[1]
    w = hg * HEAD_DIM
    nhg = n_heads // hg
    assert t <= QUERY_PAD and p % QUERY_PAD == 0
    n_keys = p + QUERY_PAD
    key = jnp.arange(n_keys)
    dist = jnp.where(key[None, :] < p + t, p + jnp.arange(t)[:, None] - key[None, :], -1)
    bias = _pattern_bias(rel_bias, dist)
    n_pat = bias.shape[0]
    return pl.pallas_call(
        functools.partial(_attn_sample_kernel, hg=hg, t=t),
        grid=(b, nhg),
        in_specs=[
            pl.BlockSpec((t, w), lambda bb, g: (bb, g)),
            pl.BlockSpec((t, w), lambda bb, g: (bb, nhg + g)),
            pl.BlockSpec((t, w), lambda bb, g: (bb, 2 * nhg + g)),
            pl.BlockSpec((None, p, w), lambda bb, g: (bb, 0, g)),
            pl.BlockSpec((None, p, w), lambda bb, g: (bb, 0, g)),
            pl.BlockSpec((n_pat, hg, t, n_keys), lambda bb, g: (0, g, 0, 0)),
        ],
        out_specs=pl.BlockSpec((t, w), lambda bb, g: (bb, g)),
        out_shape=jax.ShapeDtypeStruct((bt, n_heads * HEAD_DIM), F32),
        scratch_shapes=[pltpu.VMEM((n_keys, w), BF16), pltpu.VMEM((n_keys, w), BF16)],
        compiler_params=_params("arbitrary", "arbitrary"),
        name="attn_sample",
    )(proj, proj, proj, cache_k, cache_v, bias)


def _round_bf16(x):
    return x.astype(BF16).astype(F32)


def _conv_kernel(a_ref, g_ref, st_ref, w_ref, cb_ref, ng_ref, nb_ref, o_ref, tail_ref, ext_ref, y_ref,
                 extr_ref, *, t, rc, cc):
    ti = pl.program_id(1)
    ch = a_ref.shape[-1]

    @pl.when(ti == 0)
    def _():
        ext_ref[0:CONV_HALO, :] = st_ref[...]

    ext_ref[CONV_HALO:CONV_HALO + t, :] = a_ref[...] * _sigmoid(g_ref[...])
    extr_ref[...] = _round_bf16(ext_ref[...])
    first = CONV_HALO - (CONV_WIDTH - 1)
    for r0 in range(0, t, rc):
        for c0 in range(0, ch, cc):
            acc = jnp.zeros((rc, cc), F32)
            for j in range(CONV_WIDTH):
                acc = acc + w_ref[j:j + 1, c0:c0 + cc] * extr_ref[first + r0 + j:first + r0 + j + rc, c0:c0 + cc]
            y_ref[r0:r0 + rc, c0:c0 + cc] = acc + cb_ref[:, c0:c0 + cc]
    for r0 in range(0, t, rc):
        y = _layer_norm_rows(y_ref[r0:r0 + rc, :], ng_ref[...], nb_ref[...])
        o_ref[r0:r0 + rc, :] = _silu(y).astype(o_ref.dtype)

    @pl.when(ti == pl.num_programs(1) - 1)
    def _():
        tail_ref[...] = ext_ref[t + first:t + CONV_HALO, :]

    ext_ref[0:CONV_HALO, :] = ext_ref[t:t + CONV_HALO, :]


def _conv_module(proj, a_blk, state, conv_w, conv_b, ng, nb, *, ch, t, out_dtype, name):
    b, length, _ = proj.shape
    keep = CONV_WIDTH - 1
    st = jnp.pad(state, ((0, 0), (CONV_HALO - keep, 0), (0, 0)))
    rc = min(t, 32)
    vec_spec = pl.BlockSpec((1, ch), lambda bb, i: (0, 0))
    return pl.pallas_call(
        functools.partial(_conv_kernel, t=t, rc=rc, cc=512),
        grid=(b, length // t),
        in_specs=[
            pl.BlockSpec((None, t, ch), lambda bb, i: (bb, i, a_blk)),
            pl.BlockSpec((None, t, ch), lambda bb, i: (bb, i, a_blk + 1)),
            pl.BlockSpec((None, CONV_HALO, ch), lambda bb, i: (bb, 0, 0)),
            pl.BlockSpec((CONV_WIDTH, ch), lambda bb, i: (0, 0)),
            vec_spec, vec_spec, vec_spec,
        ],
        out_specs=[
            pl.BlockSpec((None, t, ch), lambda bb, i: (bb, i, 0)),
            pl.BlockSpec((None, keep, ch), lambda bb, i: (bb, 0, 0)),
        ],
        out_shape=[jax.ShapeDtypeStruct((b, length, ch), out_dtype),
                   jax.ShapeDtypeStruct((b, keep, ch), F32)],
        scratch_shapes=[pltpu.VMEM((CONV_HALO + t, ch), F32), pltpu.VMEM((t, ch), F32),
                        pltpu.VMEM((CONV_HALO + t, ch), F32)],
        compiler_params=_params("arbitrary", "arbitrary"),
        name=name,
    )(proj, proj, st, conv_w, conv_b.reshape(1, ch), ng.reshape(1, ch), nb.reshape(1, ch))


def _split3(x):
    a1 = x.astype(BF16)
    r1 = x - a1.astype(F32)
    a2 = r1.astype(BF16)
    a3 = (r1 - a2.astype(F32)).astype(BF16)
    return a1, a2, a3


def _hgrn_kernel(q_ref, f_ref, i_ref, g_ref, lb_ref, ng_ref, s0_ref, o_ref, sfin_ref, st_ref,
                 *, hg, t, c, pairwise):
    ti = pl.program_id(2)
    dk = HEAD_DIM
    hi = lax.Precision.HIGHEST

    @pl.when(ti == 0)
    def _():
        for h in range(hg):
            st_ref[h] = s0_ref[h].T

    lb = lb_ref[...]
    ng = ng_ref[...]
    causal = lax.broadcasted_iota(jnp.int32, (c, c), 0) >= lax.broadcasted_iota(jnp.int32, (c, c), 1)
    step = lax.broadcasted_iota(jnp.int32, (c, 1), 0)
    for c0 in range(0, t, c):
        rows = slice(c0, c0 + c)
        fg = lb + (1.0 - lb) * _sigmoid(f_ref[rows, :])
        kk = 1.0 - fg
        qq = _silu(q_ref[rows, :])
        logf = jnp.log(fg)
        if pairwise:
            acc = [logf[0:1, :]]
            for r in range(1, c):
                acc.append(acc[-1] + logf[r:r + 1, :])
            cum = jnp.concatenate(acc, 0)
        else:
            l1, l2, l3 = _split3(logf)
            tri = causal.astype(BF16)
            cum = _dot(tri, l1) + _dot(tri, l2) + _dot(tri, l3)
            mid = cum[c // 2 - 1:c // 2, :]
            q_md = (qq * jnp.exp(cum - mid)).astype(BF16)
            k_md = (kk * jnp.exp(mid - cum)).astype(BF16)
        last = cum[c - 1:c, :]
        q_in = qq * jnp.exp(cum)
        k_end = kk * jnp.exp(last - cum)
        dec = jnp.exp(last)
        for h in range(hg):
            cs = slice(h * dk, (h + 1) * dk)
            st = st_ref[h]
            if pairwise:
                v = _round_bf16(i_ref[rows, cs])
                o_rows = []
                for r in range(c):
                    pair = qq[r:r + 1, cs] * jnp.exp(cum[r:r + 1, cs] - cum[:, cs]) * kk[:, cs]
                    sc = jnp.where(step <= r, jnp.sum(pair, -1, keepdims=True), 0.0)
                    o_rows.append(jnp.sum(_round_bf16(sc) * v, 0, keepdims=True))
                o = _dot_nt(_round_bf16(q_in[:, cs]), _round_bf16(st), hi) + jnp.concatenate(o_rows, 0)
                st_ref[h] = st * dec[:, cs] + _dot_tn(v, _round_bf16(k_end[:, cs]), hi)
            else:
                v = i_ref[rows, cs].astype(BF16)
                sc = jnp.where(causal, _dot_nt(q_md[:, cs], k_md[:, cs]), 0.0)
                o = _dot_nt(q_in[:, cs].astype(BF16), st.astype(BF16)) + _dot(sc.astype(BF16), v)
                st_ref[h] = st * dec[:, cs] + _dot_tn(v, k_end[:, cs].astype(BF16))
            normed = o * lax.rsqrt(jnp.mean(o * o, -1, keepdims=True) + EPS) * ng
            o_ref[rows, cs] = (normed * _silu(g_ref[rows, cs])).astype(o_ref.dtype)

    @pl.when(ti == pl.num_programs(2) - 1)
    def _():
        for h in range(hg):
            sfin_ref[h] = st_ref[h].T


def _hgrn(proj, s0, lb, ng, n_heads, *, hg, t, c, pairwise, out_dtype, name):
    b, length, _ = proj.shape
    w = hg * HEAD_DIM
    nhg = n_heads // hg
    col = lambda k: pl.BlockSpec((None, t, w), lambda bb, g, i: (bb, i, k * nhg + g))
    st_spec = pl.BlockSpec((None, hg, HEAD_DIM, HEAD_DIM), lambda bb, g, i: (bb, g, 0, 0))
    return pl.pallas_call(
        functools.partial(_hgrn_kernel, hg=hg, t=t, c=c, pairwise=pairwise),
        grid=(b, nhg, length // t),
        in_specs=[col(0), col(1), col(2), col(3),
                  pl.BlockSpec((1, w), lambda bb, g, i: (0, g)),
                  pl.BlockSpec((1, HEAD_DIM), lambda bb, g, i: (0, 0)),
                  st_spec],
        out_specs=[pl.BlockSpec((None, t, w), lambda bb, g, i: (bb, i, g)), st_spec],
        out_shape=[jax.ShapeDtypeStruct((b, length, n_heads * HEAD_DIM), out_dtype),
                   jax.ShapeDtypeStruct(s0.shape, F32)],
        scratch_shapes=[pltpu.VMEM((hg, HEAD_DIM, HEAD_DIM), F32)],
        compiler_params=_params("arbitrary", "arbitrary", "arbitrary"),
        name=name,
    )(proj, proj, proj, proj, lb.reshape(1, -1), ng.reshape(1, HEAD_DIM), s0)


def _top2(vals):
    n = len(vals)
    m1 = functools.reduce(jnp.maximum, vals)
    i1 = jnp.full(m1.shape, n - 1, jnp.int32)
    for k in range(n - 2, -1, -1):
        i1 = jnp.where(vals[k] == m1, k, i1)
    rest = [jnp.where(i1 == k, -1.0, vals[k]) for k in range(n)]
    m2 = functools.reduce(jnp.maximum, rest)
    i2 = jnp.full(m1.shape, n - 1, jnp.int32)
    for k in range(n - 2, -1, -1):
        i2 = jnp.where(rest[k] == m2, k, i2)
    return m1, i1, m2, i2


def _router_kernel(x_ref, w_ref, b_ref, ex_ref, gt_ref):
    logits = _dot_nt(w_ref[...].astype(BF16), x_ref[...]) + b_ref[...]
    e = jnp.exp(logits - jnp.max(logits, 0, keepdims=True))
    p = e / jnp.sum(e, 0, keepdims=True)
    members = [p[k * N_GROUPS:(k + 1) * N_GROUPS, :] for k in range(GROUP_SIZE)]
    m1, i1, m2, i2 = _top2(members)
    score = m1 + m2
    gid = lax.broadcasted_iota(jnp.int32, score.shape, 0)
    best = jnp.max(score, 0, keepdims=True)
    gsel = jnp.min(jnp.where(score == best, gid, N_GROUPS), 0, keepdims=True)
    sel = gid == gsel
    pick_f = lambda a: jnp.sum(jnp.where(sel, a, 0.0), 0, keepdims=True)
    pick_i = lambda a: jnp.sum(jnp.where(sel, a, 0), 0, keepdims=True)
    p1, p2 = pick_f(m1), pick_f(m2)
    ex_ref[0:1, :] = gsel * GROUP_SIZE + pick_i(i1)
    ex_ref[1:2, :] = gsel * GROUP_SIZE + pick_i(i2)
    gt_ref[0:1, :] = p1 / (p1 + p2)
    gt_ref[1:2, :] = p2 / (p1 + p2)


def _router(x, w_t, b_col, *, tm, name):
    n, d = x.shape
    out_spec = pl.BlockSpec((TOP_K, tm), lambda i: (0, i))
    return pl.pallas_call(
        _router_kernel,
        grid=(pl.cdiv(n, tm),),
        in_specs=[pl.BlockSpec((tm, d), lambda i: (i, 0)),
                  pl.BlockSpec((N_EXPERTS, d), lambda i: (0, 0)),
                  pl.BlockSpec((N_EXPERTS, 1), lambda i: (0, 0))],
        out_specs=[out_spec, out_spec],
        out_shape=[jax.ShapeDtypeStruct((TOP_K, n), jnp.int32), jax.ShapeDtypeStruct((TOP_K, n), F32)],
        compiler_params=_params("arbitrary"),
        name=name,
    )(x, w_t, b_col)


MOE_SLOT_ROWS = 576


def _slot_index(s, j, nu_ref, nj):
    used = s < nu_ref[0]
    return jnp.minimum(s, nu_ref[0] - 1), jnp.where(used, j, nj - 1)


def _gate_up_kernel(se_ref, ss_ref, nu_ref, tok_ref, x_hbm, wg_ref, wu_ref, h_ref, xg_ref, xb_ref, sem,
                    *, n_slots, nj):
    s = pl.program_id(0)
    j = pl.program_id(1)
    n_used = nu_ref[0]
    rows = xb_ref.shape[0]
    part = rows // nj

    def start_rows(slot, first, count):
        buf = slot % 2
        base = ss_ref[slot] + first
        for r in range(count):
            tok = tok_ref[base + r]
            pltpu.make_async_copy(x_hbm.at[pl.ds(tok, 1), :], xg_ref.at[buf, pl.ds(first + r, 1), :],
                                  sem.at[buf]).start()

    def wait_buffer(slot):
        buf = slot % 2
        pltpu.make_async_copy(x_hbm.at[pl.ds(0, rows), :], xg_ref.at[buf], sem.at[buf]).wait()

    @pl.when(jnp.logical_and(s == 0, j == 0))
    def _():
        start_rows(0, 0, rows)

    @pl.when(jnp.logical_and(s <= n_used, j == 0))
    def _():
        wait_buffer(s)

    @pl.when(jnp.logical_and(s < n_used, j == 0))
    def _():
        xb_ref[...] = xg_ref[s % 2].astype(BF16)

    @pl.when(s < n_used)
    def _():
        start_rows(s + 1, j * part, part)
        x = xb_ref[...]
        gate = _dot(x, wg_ref[...].astype(BF16))
        up = _dot(x, wu_ref[...].astype(BF16))
        h_ref[...] = (_silu(gate) * up).astype(h_ref.dtype)

    @pl.when(jnp.logical_and(jnp.logical_and(s == n_slots - 1, j == nj - 1), n_used == n_slots))
    def _():
        wait_buffer(n_slots)


def _down_kernel(se_ref, nu_ref, h_ref, wd_ref, y_ref):
    @pl.when(pl.program_id(0) < nu_ref[0])
    def _():
        y_ref[...] = _dot(h_ref[...], wd_ref[...].astype(BF16))


def _expert_ffn(x_rows, slot_expert, slot_start, n_used, tok_sorted, w_gate_up, w_down, layer, *, th, tn):
    n, d = x_rows.shape
    d_exp = w_down.shape[2]
    n_slots = slot_expert.shape[0] - 1
    rows = n_slots * MOE_SLOT_ROWS
    nj = d_exp // th

    def w_map(col0):
        def index(s, j, se, ss, nu, tok):
            se_, je_ = _slot_index(s, j, nu, nj)
            return (layer, se[se_], 0, col0 + je_)
        return index

    def h_map(s, j, se, ss, nu, tok):
        return _slot_index(s, j, nu, nj)

    h = pl.pallas_call(
        functools.partial(_gate_up_kernel, n_slots=n_slots, nj=nj),
        grid_spec=pltpu.PrefetchScalarGridSpec(
            num_scalar_prefetch=4,
            grid=(n_slots, nj),
            in_specs=[
                pl.BlockSpec(memory_space=pl.ANY),
                pl.BlockSpec((None, None, d, th), w_map(0)),
                pl.BlockSpec((None, None, d, th), w_map(nj)),
            ],
            out_specs=pl.BlockSpec((MOE_SLOT_ROWS, th), h_map),
            scratch_shapes=[pltpu.VMEM((2, MOE_SLOT_ROWS, d), F32), pltpu.VMEM((MOE_SLOT_ROWS, d), BF16),
                            pltpu.SemaphoreType.DMA((2,))],
        ),
        out_shape=jax.ShapeDtypeStruct((rows, d_exp), BF16),
        compiler_params=_params("arbitrary", "arbitrary"),
        name=f"moe_gate_up_{layer}",
    )(slot_expert, slot_start, n_used, tok_sorted, x_rows, w_gate_up, w_gate_up)

    nj2 = d // tn

    def wd_map(s, j, se, nu):
        se_, je_ = _slot_index(s, j, nu, nj2)
        return (layer, se[se_], 0, je_)

    return pl.pallas_call(
        _down_kernel,
        grid_spec=pltpu.PrefetchScalarGridSpec(
            num_scalar_prefetch=2,
            grid=(n_slots, nj2),
            in_specs=[
                pl.BlockSpec((MOE_SLOT_ROWS, d_exp), lambda s, j, se, nu: (_slot_index(s, j, nu, nj2)[0], 0)),
                pl.BlockSpec((None, None, d_exp, tn), wd_map),
            ],
            out_specs=pl.BlockSpec((MOE_SLOT_ROWS, tn), lambda s, j, se, nu: _slot_index(s, j, nu, nj2)),
        ),
        out_shape=jax.ShapeDtypeStruct((rows, d), F32),
        compiler_params=_params("arbitrary", "arbitrary"),
        name=f"moe_down_{layer}",
    )(slot_expert, n_used, h, w_down)


def _moe_layer(x_rows, experts, gates, w_gate_up, w_down, layer, ln_g, ln_b, *, tm_ln):
    n, d = x_rows.shape
    n_assign = n * TOP_K
    r = MOE_SLOT_ROWS
    flat_e = experts.reshape(-1)
    order = jnp.argsort(flat_e).astype(jnp.int32)
    e_sorted = flat_e[order]
    tok_sorted = order // TOP_K
    counts = jnp.bincount(flat_e, length=N_EXPERTS).astype(jnp.int32)
    starts = jnp.cumsum(counts) - counts
    slots_per = (counts + r - 1) // r
    slot_end = jnp.cumsum(slots_per)
    slot_first = slot_end - slots_per
    n_slots = -(-n_assign // r) + N_EXPERTS
    sid = jnp.arange(n_slots + 1, dtype=jnp.int32)
    used = sid < slot_end[-1]
    slot_expert = jnp.minimum(jnp.searchsorted(slot_end, sid, side='right'), N_EXPERTS - 1).astype(jnp.int32)
    local = sid - slot_first[slot_expert]
    slot_start = jnp.where(used, starts[slot_expert] + local * r, 0).astype(jnp.int32)
    n_used = slot_end[-1].astype(jnp.int32).reshape(1)
    tok_sorted = jnp.pad(tok_sorted, (0, r))
    rank = jnp.arange(n_assign, dtype=jnp.int32) - starts[e_sorted]
    dest_sorted = (slot_first[e_sorted] + rank // r) * r + rank % r
    dest = jnp.zeros((n_assign,), jnp.int32).at[order].set(dest_sorted).reshape(n, TOP_K)
    ys = _expert_ffn(x_rows, slot_expert, slot_start, n_used, tok_sorted, w_gate_up, w_down, layer, th=256, tn=1024)
    return _ln_moe(x_rows, ys, dest, gates, ln_g, ln_b, tm=tm_ln, name=f"ln_moe_{layer}")


def kernel(x_prompt, x_sample, cache_win_k, cache_win_v, state_conv, state_hgrn, w_in_ab, rel_bias, conv_w,
           conv_b, conv_norm_g, conv_norm_b, w_out_ab, w_in_c, hgrn_lb, hgrn_norm_g, w_out_c, ln_g, ln_b,
           router_w, router_b, w_gate_up, w_down):
    bp, sp, d = x_prompt.shape
    bs, ts, _ = x_sample.shape
    n_p, n_s = bp * sp, bs * ts
    attn_w = d // 2
    n_ah = attn_w // HEAD_DIM
    conv_ch = d - attn_w
    n_hh = d // HEAD_DIM
    keep = CONV_WIDTH - 1

    lb_soft = jax.nn.softmax(hgrn_lb.astype(F32), axis=0)
    lower_bounds = jnp.cumsum(lb_soft, axis=0) - lb_soft[0]
    perm = (jnp.arange(N_EXPERTS) % N_GROUPS) * GROUP_SIZE + jnp.arange(N_EXPERTS) // N_GROUPS
    router_wt = router_w.T[perm]
    router_bc = router_b.astype(F32)[perm].reshape(N_EXPERTS, 1)

    n_tot = n_p + n_s
    x = jnp.concatenate([x_prompt.reshape(n_p, d), x_sample.reshape(n_s, d)], axis=0)
    xb = x.astype(BF16)
    outs = {}
    for l in range(DEPTH):
        if l % 2 == 0:
            a = l // 2
            n_in = w_in_ab.shape[2]
            proj_p = _matmul([xb], w_in_ab, a, n_in, tm=512, tn=512, rows=n_p, name=f"ab_in_p{l}")
            proj_s = _matmul([xb], w_in_ab, a, n_in, tm=n_s, tn=512, rows=n_s, x_row0=n_p, name=f"ab_in_s{l}")
            pp3 = proj_p.reshape(bp, sp, n_in)
            attn_p = _attn_prompt(pp3, rel_bias, n_ah, tq=256)
            conv_p, tail_p = _conv_module(pp3, 3 * attn_w // conv_ch, jnp.zeros((bp, keep, conv_ch), F32),
                                          conv_w[a], conv_b[a], conv_norm_g[a], conv_norm_b[a],
                                          ch=conv_ch, t=128, out_dtype=BF16, name=f"conv_p{l}")
            attn_s = _attn_sample(proj_s, cache_win_k[a].reshape(bs, -1, attn_w),
                                  cache_win_v[a].reshape(bs, -1, attn_w), rel_bias, n_ah, ts, hg=4)
            conv_s, tail_s = _conv_module(proj_s.reshape(bs, ts, n_in), 3 * attn_w // conv_ch, state_conv[a],
                                          conv_w[a], conv_b[a], conv_norm_g[a], conv_norm_b[a],
                                          ch=conv_ch, t=ts, out_dtype=F32, name=f"conv_s{l}")
            acts_p = [attn_p.reshape(n_p, attn_w), conv_p.reshape(n_p, conv_ch)]
            acts_s = [attn_s.astype(BF16), conv_s.reshape(n_s, conv_ch).astype(BF16)]
            w_out, lw = w_out_ab, a
            outs.setdefault("wk_p", []).append(proj_p[:, attn_w:2 * attn_w].reshape(bp, sp, n_ah, HEAD_DIM))
            outs.setdefault("wv_p", []).append(proj_p[:, 2 * attn_w:3 * attn_w].reshape(bp, sp, n_ah, HEAD_DIM))
            outs.setdefault("cv_p", []).append(tail_p)
            outs.setdefault("wk_s", []).append(proj_s[:, attn_w:2 * attn_w].reshape(bs, ts, n_ah, HEAD_DIM))
            outs.setdefault("wv_s", []).append(proj_s[:, 2 * attn_w:3 * attn_w].reshape(bs, ts, n_ah, HEAD_DIM))
            outs.setdefault("cv_s", []).append(tail_s)
        else:
            c = l // 2
            n_in = w_in_c.shape[2]
            proj_p = _matmul([xb], w_in_c, c, n_in, tm=512, tn=512, rows=n_p, name=f"c_in_p{l}")
            proj_s = _matmul([xb], w_in_c, c, n_in, tm=n_s, tn=512, rows=n_s, x_row0=n_p, name=f"c_in_s{l}")
            o_p, h_p = _hgrn(proj_p.reshape(bp, sp, n_in), jnp.zeros((bp, n_hh, HEAD_DIM, HEAD_DIM), F32),
                             lower_bounds[l], hgrn_norm_g[c], n_hh, hg=4, t=256, c=64, pairwise=False,
                             out_dtype=BF16, name=f"hgrn_p{l}")
            o_s, h_s = _hgrn(proj_s.reshape(bs, ts, n_in), state_hgrn[c], lower_bounds[l], hgrn_norm_g[c],
                             n_hh, hg=4, t=ts, c=ts, pairwise=True, out_dtype=F32, name=f"hgrn_s{l}")
            acts_p = [o_p.reshape(n_p, d)]
            acts_s = [o_s.reshape(n_s, d).astype(BF16)]
            w_out, lw = w_out_c, c
            outs.setdefault("hs_p", []).append(h_p)
            outs.setdefault("hs_s", []).append(h_s)
        m = _matmul(acts_p, w_out, lw, d, tm=512, tn=512, rows=n_p, out_rows=n_tot, name=f"mix_out_p{l}")
        m = _matmul(acts_s, w_out, lw, d, tm=n_s, tn=512, rows=n_s, out_rows=n_tot, out_row0=n_p, into=m,
                    name=f"mix_out_s{l}")
        x, xb = _ln_res(x, m, ln_g[l, 0], ln_b[l, 0], tm=256, name=f"ln_mix{l}")
        experts, gates = _router(xb, router_wt, router_bc, tm=512, name=f"router{l}")
        x, xb = _moe_layer(x, experts.T, gates.T, w_gate_up, w_down, l, ln_g[l, 1], ln_b[l, 1], tm_ln=128)
    stack = lambda k: jnp.stack(outs[k])
    return (x[:n_p].reshape(bp, sp, d), x[n_p:].reshape(bs, ts, d), stack("wk_p"), stack("wv_p"), stack("cv_p"),
            stack("hs_p"), stack("wk_s"), stack("wv_s"), stack("cv_s"), stack("hs_s"))
```

```python
import functools
import math

import jax
import jax.numpy as jnp
from jax import lax
from jax.experimental import pallas as pl
from jax.experimental.pallas import tpu as pltpu

F32 = jnp.float32
BF16 = jnp.bfloat16

DEPTH = 2
HEAD_DIM = 128
DILATED_PATTERNS = ((128, 1), (512, 4), (2048, 16))
NUM_BUCKETS = 32
MAX_EXACT = 16
MAX_DISTANCE = 2048
CONV_WIDTH = 31
N_EXPERTS = 32
N_GROUPS = 8
GROUP_SIZE = N_EXPERTS // N_GROUPS
TOP_K = 2
ALPHA = (2.0 * DEPTH) ** 0.25
EPS = 1e-5
NEG = -1e30
VMEM_LIMIT = 56 * 1024 * 1024
CONV_HALO = 32


def _params(*sem):
    return pltpu.CompilerParams(dimension_semantics=sem, vmem_limit_bytes=VMEM_LIMIT)


def _sigmoid(x):
    return 1.0 / (1.0 + jnp.exp(-x))


def _silu(x):
    return x * _sigmoid(x)


def _dot_nt(a, b, precision=None):
    return lax.dot_general(a, b, (((1,), (1,)), ((), ())), preferred_element_type=F32, precision=precision)


def _dot_tn(a, b, precision=None):
    return lax.dot_general(a, b, (((0,), (0,)), ((), ())), preferred_element_type=F32, precision=precision)


def _dot(a, b, precision=None):
    return jnp.dot(a, b, preferred_element_type=F32, precision=precision)


def _mm_kernel(*refs, n_x):
    x_refs = refs[:n_x]
    w_ref = refs[n_x]
    o_ref, wb_ref = refs[-2:]

    @pl.when(pl.program_id(1) == 0)
    def _():
        wb_ref[...] = w_ref[...].astype(BF16)

    acc = None
    off = 0
    for xr in x_refs:
        k = xr.shape[-1]
        part = _dot(xr[...], wb_ref[off:off + k, :])
        acc = part if acc is None else acc + part
        off += k
    o_ref[...] = acc.astype(o_ref.dtype)


def _matmul(xs, w, layer, n_cols, *, tm, tn, name, rows, x_row0=0, out_rows=None, out_row0=0, into=None):
    k_tot = sum(x.shape[1] for x in xs)
    assert w.shape[1] == k_tot and rows % tm == 0 and n_cols % tn == 0
    assert x_row0 % tm == 0 and out_row0 % tm == 0
    xoff, ooff = x_row0 // tm, out_row0 // tm
    out_rows = out_rows or rows
    in_specs = [pl.BlockSpec((tm, x.shape[1]), lambda n, m: (m + xoff, 0)) for x in xs]
    in_specs.append(pl.BlockSpec((None, k_tot, tn), lambda n, m: (layer, 0, n)))
    args = list(xs) + [w]
    aliases = {}
    if into is not None:
        assert into.shape == (out_rows, n_cols)
        in_specs.append(pl.BlockSpec(memory_space=pl.ANY))
        aliases = {len(args): 0}
        args.append(into)
    return pl.pallas_call(
        functools.partial(_mm_kernel, n_x=len(xs)),
        grid=(n_cols // tn, rows // tm),
        in_specs=in_specs,
        out_specs=pl.BlockSpec((tm, tn), lambda n, m: (m + ooff, n)),
        out_shape=jax.ShapeDtypeStruct((out_rows, n_cols), F32),
        scratch_shapes=[pltpu.VMEM((k_tot, tn), BF16)],
        input_output_aliases=aliases,
        compiler_params=_params("arbitrary", "arbitrary"),
        name=name,
    )(*args)


def _layer_norm_rows(z, g, b):
    mu = jnp.mean(z, -1, keepdims=True)
    zc = z - mu
    var = jnp.mean(zc * zc, -1, keepdims=True)
    return zc * lax.rsqrt(var + EPS) * g + b


LN_ROW_CHUNK = 32


def _ln_res_kernel(x_ref, m_ref, g_ref, b_ref, of_ref, ob_ref):
    for r0 in range(0, x_ref.shape[0], LN_ROW_CHUNK):
        rows = slice(r0, r0 + LN_ROW_CHUNK)
        y = _layer_norm_rows(ALPHA * x_ref[rows, :] + m_ref[rows, :], g_ref[...], b_ref[...])
        of_ref[rows, :] = y
        ob_ref[rows, :] = y.astype(BF16)


def _ln_res(x, m, g, b, *, tm, name):
    rows, d = x.shape
    row_spec = pl.BlockSpec((tm, d), lambda i: (i, 0))
    vec_spec = pl.BlockSpec((1, d), lambda i: (0, 0))
    return pl.pallas_call(
        _ln_res_kernel,
        grid=(pl.cdiv(rows, tm),),
        in_specs=[row_spec, row_spec, vec_spec, vec_spec],
        out_specs=[row_spec, row_spec],
        out_shape=[jax.ShapeDtypeStruct((rows, d), F32), jax.ShapeDtypeStruct((rows, d), BF16)],
        compiler_params=_params("arbitrary"),
        name=name,
    )(x, m, g.reshape(1, d), b.reshape(1, d))


def _ln_moe_kernel(dest_ref, x_ref, gt_ref, g_ref, b_ref, ys_hbm, of_ref, ob_ref, yg_ref, sem, *, tm, n_steps):
    i = pl.program_id(0)

    def start_rows(step):
        buf = step % 2
        base = step * (tm * TOP_K)
        for r in range(tm):
            for k in range(TOP_K):
                row = dest_ref[base + r * TOP_K + k]
                pltpu.make_async_copy(ys_hbm.at[pl.ds(row, 1), :], yg_ref.at[buf, pl.ds(k * tm + r, 1), :],
                                      sem.at[buf]).start()

    def wait_buffer(step):
        buf = step % 2
        pltpu.make_async_copy(ys_hbm.at[pl.ds(0, TOP_K * tm), :], yg_ref.at[buf], sem.at[buf]).wait()

    @pl.when(i == 0)
    def _():
        start_rows(0)

    wait_buffer(i)
    start_rows(i + 1)
    buf = i % 2
    for r0 in range(0, tm, LN_ROW_CHUNK):
        rows = slice(r0, r0 + LN_ROW_CHUNK)
        gt = gt_ref[rows, :]
        ff = (gt[:, 0:1] * yg_ref[buf, r0:r0 + LN_ROW_CHUNK, :]
              + gt[:, 1:2] * yg_ref[buf, tm + r0:tm + r0 + LN_ROW_CHUNK, :])
        y = _layer_norm_rows(ALPHA * x_ref[rows, :] + ff, g_ref[...], b_ref[...])
        of_ref[rows, :] = y
        ob_ref[rows, :] = y.astype(BF16)

    @pl.when(i == n_steps - 1)
    def _():
        wait_buffer(i + 1)


def _ln_moe(x, ys, dest, gates, g, b, *, tm, name):
    rows, d = x.shape
    n_steps = pl.cdiv(rows, tm)
    dest_flat = jnp.pad(dest.reshape(-1), (0, (n_steps + 1) * tm * TOP_K - rows * TOP_K))
    row_spec = pl.BlockSpec((tm, d), lambda i, dr: (i, 0))
    vec_spec = pl.BlockSpec((1, d), lambda i, dr: (0, 0))
    return pl.pallas_call(
        functools.partial(_ln_moe_kernel, tm=tm, n_steps=n_steps),
        grid_spec=pltpu.PrefetchScalarGridSpec(
            num_scalar_prefetch=1,
            grid=(n_steps,),
            in_specs=[row_spec, pl.BlockSpec((tm, TOP_K), lambda i, dr: (i, 0)), vec_spec, vec_spec,
                      pl.BlockSpec(memory_space=pl.ANY)],
            out_specs=[row_spec, row_spec],
            scratch_shapes=[pltpu.VMEM((2, TOP_K * tm, d), F32), pltpu.SemaphoreType.DMA((2,))],
        ),
        out_shape=[jax.ShapeDtypeStruct((rows, d), F32), jax.ShapeDtypeStruct((rows, d), BF16)],
        compiler_params=_params("arbitrary"),
        name=name,
    )(dest_flat, x, gates, g.reshape(1, d), b.reshape(1, d), ys)


def _rel_bucket(dist):
    dist = dist.astype(jnp.int32)
    d = jnp.maximum(dist, 1).astype(F32)
    large = MAX_EXACT + (jnp.log(d / MAX_EXACT) / math.log(MAX_DISTANCE / MAX_EXACT)
                         * (NUM_BUCKETS - MAX_EXACT)).astype(jnp.int32)
    large = jnp.minimum(large, NUM_BUCKETS - 1)
    return jnp.where(dist < MAX_EXACT, dist, large)


def _pattern_bias(rel_bias, d):
    base = jnp.moveaxis(rel_bias[_rel_bucket(jnp.maximum(d, 0))].astype(F32), -1, 0)
    outs = []
    for w, r in DILATED_PATTERNS:
        ok = (d >= 0) & (d % r == 0) & (d <= w)
        outs.append(jnp.where(ok[None], base, NEG))
    return jnp.stack(outs)


def _merged_bias(rel_bias, d):
    count = sum(((d >= 0) & (d % r == 0) & (d <= w)).astype(F32) for w, r in DILATED_PATTERNS)
    base = jnp.moveaxis(rel_bias[_rel_bucket(jnp.maximum(d, 0))].astype(F32), -1, 0)
    return jnp.where((count > 0)[None], base + jnp.log(jnp.maximum(count, 1.0))[None], NEG)


def _toeplitz_tiles(vec, nd, tq):
    h = vec.shape[0]
    span = 2 * tq - 1
    win = jnp.stack([vec[:, k * tq:k * tq + span] for k in range(nd)], 1)
    rev = jnp.pad(win[..., ::-1], ((0, 0), (0, 0), (0, 1)))
    skew = jnp.broadcast_to(rev[:, :, None, :], (h, nd, tq, 2 * tq)).reshape(h, nd, tq * 2 * tq)
    skew = skew[..., :tq * span].reshape(h, nd, tq, span)
    return skew[..., tq - 1:span]


def _attn_prompt_kernel(q_ref, k_ref, v_ref, bias_ref, o_ref, kb_ref, vb_ref, *, tq):
    i = pl.program_id(1)
    nb = q_ref.shape[0]

    @pl.when(i == 0)
    def _():
        kb_ref[...] = k_ref[...].astype(BF16)
        vb_ref[...] = v_ref[...].astype(BF16)

    qs = [(q_ref[b] * (HEAD_DIM ** -0.5)).astype(BF16) for b in range(nb)]

    def body(j, carry):
        start = pl.multiple_of(j * tq, tq)
        bias = bias_ref[i - j]
        out = []
        for b in range(nb):
            m, l, acc = carry[b]
            s = _dot_nt(qs[b], kb_ref[b, pl.ds(start, tq), :]) + bias
            m_new = jnp.maximum(m, jnp.max(s, -1, keepdims=True))
            p = jnp.exp(s - m_new)
            a = jnp.exp(m - m_new)
            l = a * l + jnp.sum(p, -1, keepdims=True)
            acc = a * acc + _dot(p.astype(BF16), vb_ref[b, pl.ds(start, tq), :])
            out.append((m_new, l, acc))
        return tuple(out)

    one = (jnp.full((tq, 1), NEG, F32), jnp.zeros((tq, 1), F32), jnp.zeros((tq, HEAD_DIM), F32))
    res = lax.fori_loop(0, i + 1, body, (one,) * nb)
    for b in range(nb):
        _, l, acc = res[b]
        o_ref[b] = (acc / l).astype(o_ref.dtype)


def _attn_prompt(proj, rel_bias, n_heads, *, tq):
    b, s, _ = proj.shape
    nd = s // tq
    table = _toeplitz_tiles(_merged_bias(rel_bias, jnp.arange(-(tq - 1), s)), nd, tq)
    return pl.pallas_call(
        functools.partial(_attn_prompt_kernel, tq=tq),
        grid=(n_heads, nd),
        in_specs=[
            pl.BlockSpec((b, tq, HEAD_DIM), lambda h, i: (0, i, h)),
            pl.BlockSpec((b, s, HEAD_DIM), lambda h, i: (0, 0, n_heads + h)),
            pl.BlockSpec((b, s, HEAD_DIM), lambda h, i: (0, 0, 2 * n_heads + h)),
            pl.BlockSpec((None, nd, tq, tq), lambda h, i: (h, 0, 0, 0)),
        ],
        out_specs=pl.BlockSpec((b, tq, HEAD_DIM), lambda h, i: (0, i, h)),
        out_shape=jax.ShapeDtypeStruct((b, s, n_heads * HEAD_DIM), BF16),
        scratch_shapes=[pltpu.VMEM((b, s, HEAD_DIM), BF16), pltpu.VMEM((b, s, HEAD_DIM), BF16)],
        compiler_params=_params("arbitrary", "arbitrary"),
        name="attn_prompt",
    )(proj, proj, proj, table)


QUERY_PAD = 16


def _attn_sample_kernel(q_ref, kn_ref, vn_ref, kc_ref, vc_ref, bias_ref, o_ref, kx_ref, vx_ref, *, hg, t):
    p = kc_ref.shape[0]
    n_keys, w = kx_ref.shape
    zrow = jnp.zeros((n_keys - p - t, w), F32)
    kx_ref[0:p, :] = kc_ref[...].astype(BF16)
    vx_ref[0:p, :] = vc_ref[...].astype(BF16)
    kx_ref[p:n_keys, :] = jnp.concatenate([kn_ref[...], zrow], 0).astype(BF16)
    vx_ref[p:n_keys, :] = jnp.concatenate([vn_ref[...], zrow], 0).astype(BF16)
    n_pat = bias_ref.shape[0]
    add = lambda a, b: a + b
    for h in range(hg):
        cs = slice(h * HEAD_DIM, (h + 1) * HEAD_DIM)
        q16 = jnp.concatenate([q_ref[:, cs], jnp.zeros((QUERY_PAD - t, HEAD_DIM), F32)], 0).astype(BF16)
        s = _dot_nt(q16, kx_ref[:, cs])[0:t, :] * (HEAD_DIM ** -0.5)
        outs, lses = [], []
        for g in range(n_pat):
            lg = s + bias_ref[g, h]
            m = jnp.max(lg, -1, keepdims=True)
            e = jnp.exp(lg - m)
            den = jnp.sum(e, -1, keepdims=True)
            p16 = jnp.concatenate([e / den, jnp.zeros((QUERY_PAD - t, n_keys), F32)], 0).astype(BF16)
            outs.append(_dot(p16, vx_ref[:, cs])[0:t, :])
            lses.append(m + jnp.log(den))
        top = functools.reduce(jnp.maximum, lses)
        ws = [jnp.exp(l - top) for l in lses]
        tot = functools.reduce(add, ws)
        o_ref[:, cs] = functools.reduce(add, [(wg / tot) * og for wg, og in zip(ws, outs)])


def _attn_sample(proj, cache_k, cache_v, rel_bias, n_heads, t, *, hg):
    bt = proj.shape[0]
    b = bt // t
    p = cache_k.shape[1]
    w = hg * HEAD_DIM
    nhg = n_heads // hg
    assert t <= QUERY_PAD and p % QUERY_PAD == 0
    n_keys = p + QUERY_PAD
    key = jnp.arange(n_keys)
    dist = jnp.where(key[None, :] < p + t, p + jnp.arange(t)[:, None] - key[None, :], -1)
    bias = _pattern_bias(rel_bias, dist)
    n_pat = bias.shape[0]
    return pl.pallas_call(
        functools.partial(_attn_sample_kernel, hg=hg, t=t),
        grid=(b, nhg),
        in_specs=[
            pl.BlockSpec((t, w), lambda bb, g: (bb, g)),
            pl.BlockSpec((t, w), lambda bb, g: (bb, nhg + g)),
            pl.BlockSpec((t, w), lambda bb, g: (bb, 2 * nhg + g)),
            pl.BlockSpec((None, p, w), lambda bb, g: (bb, 0, g)),
            pl.BlockSpec((None, p, w), lambda bb, g: (bb, 0, g)),
            pl.BlockSpec((n_pat, hg, t, n_keys), lambda bb, g: (0, g, 0, 0)),
        ],
        out_specs=pl.BlockSpec((t, w), lambda bb, g: (bb, g)),
        out_shape=jax.ShapeDtypeStruct((bt, n_heads * HEAD_DIM), F32),
        scratch_shapes=[pltpu.VMEM((n_keys, w), BF16), pltpu.VMEM((n_keys, w), BF16)],
        compiler_params=_params("arbitrary", "arbitrary"),
        name="attn_sample",
    )(proj, proj, proj, cache_k, cache_v, bias)


def _round_bf16(x):
    return x.astype(BF16).astype(F32)


CONV_ALIGN = 8


def _conv_kernel(a_ref, g_ref, st_ref, w_ref, cb_ref, ng_ref, nb_ref, o_ref, tail_ref, ext_ref, y_ref,
                 extr_ref, sh_ref, *, t, rc, cc):
    ti = pl.program_id(1)
    ch = a_ref.shape[-1]
    n_ext = CONV_HALO + t

    @pl.when(ti == 0)
    def _():
        ext_ref[0:CONV_HALO, :] = st_ref[...]
        extr_ref[n_ext:n_ext + CONV_ALIGN, :] = jnp.zeros((CONV_ALIGN, ch), F32)

    ext_ref[CONV_HALO:n_ext, :] = a_ref[...] * _sigmoid(g_ref[...])
    extr_ref[0:n_ext, :] = _round_bf16(ext_ref[...])
    for s in range(CONV_ALIGN):
        sh_ref[s] = extr_ref[s:s + n_ext, :]
    first = CONV_HALO - (CONV_WIDTH - 1)
    for r0 in range(0, t, rc):
        for c0 in range(0, ch, cc):
            acc = jnp.zeros((rc, cc), F32)
            for j in range(CONV_WIDTH):
                s, base = (first + j) % CONV_ALIGN, (first + j) // CONV_ALIGN * CONV_ALIGN
                acc = acc + w_ref[j:j + 1, c0:c0 + cc] * sh_ref[s, base + r0:base + r0 + rc, c0:c0 + cc]
            y_ref[r0:r0 + rc, c0:c0 + cc] = acc + cb_ref[:, c0:c0 + cc]
    for r0 in range(0, t, rc):
        y = _layer_norm_rows(y_ref[r0:r0 + rc, :], ng_ref[...], nb_ref[...])
        o_ref[r0:r0 + rc, :] = _silu(y).astype(o_ref.dtype)

    @pl.when(ti == pl.num_programs(1) - 1)
    def _():
        tail_ref[...] = ext_ref[t + first:n_ext, :]

    ext_ref[0:CONV_HALO, :] = ext_ref[t:n_ext, :]


def _conv_module(proj, a_blk, state, conv_w, conv_b, ng, nb, *, ch, t, out_dtype, name):
    b, length, _ = proj.shape
    keep = CONV_WIDTH - 1
    st = jnp.pad(state, ((0, 0), (CONV_HALO - keep, 0), (0, 0)))
    rc = min(t, 32)
    vec_spec = pl.BlockSpec((1, ch), lambda bb, i: (0, 0))
    return pl.pallas_call(
        functools.partial(_conv_kernel, t=t, rc=rc, cc=512),
        grid=(b, length // t),
        in_specs=[
            pl.BlockSpec((None, t, ch), lambda bb, i: (bb, i, a_blk)),
            pl.BlockSpec((None, t, ch), lambda bb, i: (bb, i, a_blk + 1)),
            pl.BlockSpec((None, CONV_HALO, ch), lambda bb, i: (bb, 0, 0)),
            pl.BlockSpec((CONV_WIDTH, ch), lambda bb, i: (0, 0)),
            vec_spec, vec_spec, vec_spec,
        ],
        out_specs=[
            pl.BlockSpec((None, t, ch), lambda bb, i: (bb, i, 0)),
            pl.BlockSpec((None, keep, ch), lambda bb, i: (bb, 0, 0)),
        ],
        out_shape=[jax.ShapeDtypeStruct((b, length, ch), out_dtype),
                   jax.ShapeDtypeStruct((b, keep, ch), F32)],
        scratch_shapes=[pltpu.VMEM((CONV_HALO + t, ch), F32), pltpu.VMEM((t, ch), F32),
                        pltpu.VMEM((CONV_HALO + t + CONV_ALIGN, ch), F32),
                        pltpu.VMEM((CONV_ALIGN, CONV_HALO + t, ch), F32)],
        compiler_params=_params("arbitrary", "arbitrary"),
        name=name,
    )(proj, proj, st, conv_w, conv_b.reshape(1, ch), ng.reshape(1, ch), nb.reshape(1, ch))


def _split3(x):
    a1 = x.astype(BF16)
    r1 = x - a1.astype(F32)
    a2 = r1.astype(BF16)
    a3 = (r1 - a2.astype(F32)).astype(BF16)
    return a1, a2, a3


def _hgrn_gates(q, f, lb):
    fg = lb + (1.0 - lb) * _sigmoid(f)
    return _silu(q), 1.0 - fg, jnp.log(fg)


def _hgrn_finish(o, g, ng, dtype):
    normed = o * lax.rsqrt(jnp.mean(o * o, -1, keepdims=True) + EPS) * ng
    return (normed * _silu(g)).astype(dtype)


def _hgrn_chunk_kernel(q_ref, f_ref, i_ref, g_ref, lb_ref, ng_ref, s0_ref, o_ref, sfin_ref, st_ref, *, hg, t, c):
    ti = pl.program_id(2)
    dk = HEAD_DIM

    @pl.when(ti == 0)
    def _():
        for h in range(hg):
            st_ref[h] = s0_ref[h].T

    lb = lb_ref[...]
    ng = ng_ref[...]
    causal = lax.broadcasted_iota(jnp.int32, (c, c), 0) >= lax.broadcasted_iota(jnp.int32, (c, c), 1)
    tri = causal.astype(BF16)
    heads = [slice(h * dk, (h + 1) * dk) for h in range(hg)]
    chunks = [slice(c0, c0 + c) for c0 in range(0, t, c)]
    q_in, decs, intra, incr = [], [], [], []
    for rows in chunks:
        qq, kk, logf = _hgrn_gates(q_ref[rows, :], f_ref[rows, :], lb)
        l1, l2, l3 = _split3(logf)
        cum = _dot(tri, l1) + _dot(tri, l2) + _dot(tri, l3)
        mid = cum[c // 2 - 1:c // 2, :]
        last = cum[c - 1:c, :]
        q_md = (qq * jnp.exp(cum - mid)).astype(BF16)
        k_md = (kk * jnp.exp(mid - cum)).astype(BF16)
        k_end = (kk * jnp.exp(last - cum)).astype(BF16)
        q_in.append((qq * jnp.exp(cum)).astype(BF16))
        decs.append(jnp.exp(last))
        vs = [i_ref[rows, cs].astype(BF16) for cs in heads]
        scores = [jnp.where(causal, _dot_nt(q_md[:, cs], k_md[:, cs]), 0.0).astype(BF16) for cs in heads]
        intra.append([_dot(sc, v) for sc, v in zip(scores, vs)])
        incr.append([_dot_tn(v, k_end[:, cs]) for v, cs in zip(vs, heads)])
    states = [st_ref[h] for h in range(hg)]
    for n, rows in enumerate(chunks):
        for h, cs in enumerate(heads):
            o = intra[n][h] + _dot_nt(q_in[n][:, cs], states[h].astype(BF16))
            states[h] = states[h] * decs[n][:, cs] + incr[n][h]
            o_ref[rows, cs] = _hgrn_finish(o, g_ref[rows, cs], ng, o_ref.dtype)
    for h in range(hg):
        st_ref[h] = states[h]

    @pl.when(ti == pl.num_programs(2) - 1)
    def _():
        for h in range(hg):
            sfin_ref[h] = st_ref[h].T


def _hgrn_step_kernel(q_ref, f_ref, i_ref, g_ref, lb_ref, ng_ref, s0_ref, o_ref, sfin_ref, *, hg, c):
    dk = HEAD_DIM
    hi = lax.Precision.HIGHEST
    ng = ng_ref[...]
    step = lax.broadcasted_iota(jnp.int32, (c, 1), 0)
    qq, kk, logf = _hgrn_gates(q_ref[...], f_ref[...], lb_ref[...])
    acc = [logf[0:1, :]]
    for r in range(1, c):
        acc.append(acc[-1] + logf[r:r + 1, :])
    cum = jnp.concatenate(acc, 0)
    last = cum[c - 1:c, :]
    q_in = qq * jnp.exp(cum)
    k_end = kk * jnp.exp(last - cum)
    dec = jnp.exp(last)
    for h in range(hg):
        cs = slice(h * dk, (h + 1) * dk)
        st = s0_ref[h].T
        v = _round_bf16(i_ref[:, cs])
        o_rows = []
        for r in range(c):
            pair = qq[r:r + 1, cs] * jnp.exp(cum[r:r + 1, cs] - cum[:, cs]) * kk[:, cs]
            sc = jnp.where(step <= r, jnp.sum(pair, -1, keepdims=True), 0.0)
            o_rows.append(jnp.sum(_round_bf16(sc) * v, 0, keepdims=True))
        o = _dot_nt(_round_bf16(q_in[:, cs]), _round_bf16(st), hi) + jnp.concatenate(o_rows, 0)
        sfin_ref[h] = (st * dec[:, cs] + _dot_tn(v, _round_bf16(k_end[:, cs]), hi)).T
        o_ref[:, cs] = _hgrn_finish(o, g_ref[:, cs], ng, o_ref.dtype)


def _hgrn(proj, s0, lb, ng, n_heads, *, hg, t, c, out_dtype, name):
    b, length, _ = proj.shape
    w = hg * HEAD_DIM
    nhg = n_heads // hg
    col = lambda k: pl.BlockSpec((None, t, w), lambda bb, g, i: (bb, i, k * nhg + g))
    st_spec = pl.BlockSpec((None, hg, HEAD_DIM, HEAD_DIM), lambda bb, g, i: (bb, g, 0, 0))
    if t == length == c:
        body, scratch = functools.partial(_hgrn_step_kernel, hg=hg, c=c), []
    else:
        body = functools.partial(_hgrn_chunk_kernel, hg=hg, t=t, c=c)
        scratch = [pltpu.VMEM((hg, HEAD_DIM, HEAD_DIM), F32)]
    return pl.pallas_call(
        body,
        grid=(b, nhg, length // t),
        in_specs=[col(0), col(1), col(2), col(3),
                  pl.BlockSpec((1, w), lambda bb, g, i: (0, g)),
                  pl.BlockSpec((1, HEAD_DIM), lambda bb, g, i: (0, 0)),
                  st_spec],
        out_specs=[pl.BlockSpec((None, t, w), lambda bb, g, i: (bb, i, g)), st_spec],
        out_shape=[jax.ShapeDtypeStruct((b, length, n_heads * HEAD_DIM), out_dtype),
                   jax.ShapeDtypeStruct(s0.shape, F32)],
        scratch_shapes=scratch,
        compiler_params=_params("arbitrary", "arbitrary", "arbitrary"),
        name=name,
    )(proj, proj, proj, proj, lb.reshape(1, -1), ng.reshape(1, HEAD_DIM), s0)


def _top2(vals):
    n = len(vals)
    m1 = functools.reduce(jnp.maximum, vals)
    i1 = jnp.full(m1.shape, n - 1, jnp.int32)
    for k in range(n - 2, -1, -1):
        i1 = jnp.where(vals[k] == m1, k, i1)
    rest = [jnp.where(i1 == k, -1.0, vals[k]) for k in range(n)]
    m2 = functools.reduce(jnp.maximum, rest)
    i2 = jnp.full(m1.shape, n - 1, jnp.int32)
    for k in range(n - 2, -1, -1):
        i2 = jnp.where(rest[k] == m2, k, i2)
    return m1, i1, m2, i2


def _router_kernel(x_ref, w_ref, b_ref, ex_ref, gt_ref):
    logits = _dot_nt(w_ref[...].astype(BF16), x_ref[...]) + b_ref[...]
    e = jnp.exp(logits - jnp.max(logits, 0, keepdims=True))
    p = e / jnp.sum(e, 0, keepdims=True)
    members = [p[k * N_GROUPS:(k + 1) * N_GROUPS, :] for k in range(GROUP_SIZE)]
    m1, i1, m2, i2 = _top2(members)
    score = m1 + m2
    gid = lax.broadcasted_iota(jnp.int32, score.shape, 0)
    best = jnp.max(score, 0, keepdims=True)
    gsel = jnp.min(jnp.where(score == best, gid, N_GROUPS), 0, keepdims=True)
    sel = gid == gsel
    pick_f = lambda a: jnp.sum(jnp.where(sel, a, 0.0), 0, keepdims=True)
    pick_i = lambda a: jnp.sum(jnp.where(sel, a, 0), 0, keepdims=True)
    p1, p2 = pick_f(m1), pick_f(m2)
    ex_ref[0:1, :] = gsel * GROUP_SIZE + pick_i(i1)
    ex_ref[1:2, :] = gsel * GROUP_SIZE + pick_i(i2)
    gt_ref[0:1, :] = p1 / (p1 + p2)
    gt_ref[1:2, :] = p2 / (p1 + p2)


def _router(x, w_t, b_col, *, tm, name):
    n, d = x.shape
    out_spec = pl.BlockSpec((TOP_K, tm), lambda i: (0, i))
    return pl.pallas_call(
        _router_kernel,
        grid=(pl.cdiv(n, tm),),
        in_specs=[pl.BlockSpec((tm, d), lambda i: (i, 0)),
                  pl.BlockSpec((N_EXPERTS, d), lambda i: (0, 0)),
                  pl.BlockSpec((N_EXPERTS, 1), lambda i: (0, 0))],
        out_specs=[out_spec, out_spec],
        out_shape=[jax.ShapeDtypeStruct((TOP_K, n), jnp.int32), jax.ShapeDtypeStruct((TOP_K, n), F32)],
        compiler_params=_params("arbitrary"),
        name=name,
    )(x, w_t, b_col)


MOE_SLOT_ROWS = 576


def _slot_index(s, j, nu_ref, nj):
    used = s < nu_ref[0]
    return jnp.minimum(s, nu_ref[0] - 1), jnp.where(used, j, nj - 1)


def _gate_up_kernel(se_ref, ss_ref, nu_ref, tok_ref, x_hbm, wg_ref, wu_ref, h_ref, xg_ref, xb_ref, sem,
                    *, n_slots, nj):
    s = pl.program_id(0)
    j = pl.program_id(1)
    n_used = nu_ref[0]
    rows = xb_ref.shape[0]
    part = rows // nj

    def start_rows(slot, first, count):
        buf = slot % 2
        base = ss_ref[slot] + first
        for r in range(count):
            tok = tok_ref[base + r]
            pltpu.make_async_copy(x_hbm.at[pl.ds(tok, 1), :], xg_ref.at[buf, pl.ds(first + r, 1), :],
                                  sem.at[buf]).start()

    def wait_buffer(slot):
        buf = slot % 2
        pltpu.make_async_copy(x_hbm.at[pl.ds(0, rows), :], xg_ref.at[buf], sem.at[buf]).wait()

    @pl.when(jnp.logical_and(s == 0, j == 0))
    def _():
        start_rows(0, 0, rows)

    @pl.when(jnp.logical_and(s <= n_used, j == 0))
    def _():
        wait_buffer(s)

    @pl.when(jnp.logical_and(s < n_used, j == 0))
    def _():
        xb_ref[...] = xg_ref[s % 2].astype(BF16)

    @pl.when(s < n_used)
    def _():
        start_rows(s + 1, j * part, part)
        x = xb_ref[...]
        gate = _dot(x, wg_ref[...].astype(BF16))
        up = _dot(x, wu_ref[...].astype(BF16))
        h_ref[...] = (_silu(gate) * up).astype(h_ref.dtype)

    @pl.when(jnp.logical_and(jnp.logical_and(s == n_slots - 1, j == nj - 1), n_used == n_slots))
    def _():
        wait_buffer(n_slots)


def _down_kernel(se_ref, nu_ref, h_ref, wd_ref, y_ref):
    @pl.when(pl.program_id(0) < nu_ref[0])
    def _():
        y_ref[...] = _dot(h_ref[...], wd_ref[...].astype(BF16))


def _expert_ffn(x_rows, slot_expert, slot_start, n_used, tok_sorted, w_gate_up, w_down, layer, *, th, tn):
    n, d = x_rows.shape
    d_exp = w_down.shape[2]
    n_slots = slot_expert.shape[0] - 1
    rows = n_slots * MOE_SLOT_ROWS
    nj = d_exp // th

    def w_map(col0):
        def index(s, j, se, ss, nu, tok):
            se_, je_ = _slot_index(s, j, nu, nj)
            return (layer, se[se_], 0, col0 + je_)
        return index

    def h_map(s, j, se, ss, nu, tok):
        return _slot_index(s, j, nu, nj)

    h = pl.pallas_call(
        functools.partial(_gate_up_kernel, n_slots=n_slots, nj=nj),
        grid_spec=pltpu.PrefetchScalarGridSpec(
            num_scalar_prefetch=4,
            grid=(n_slots, nj),
            in_specs=[
                pl.BlockSpec(memory_space=pl.ANY),
                pl.BlockSpec((None, None, d, th), w_map(0)),
                pl.BlockSpec((None, None, d, th), w_map(nj)),
            ],
            out_specs=pl.BlockSpec((MOE_SLOT_ROWS, th), h_map),
            scratch_shapes=[pltpu.VMEM((2, MOE_SLOT_ROWS, d), F32), pltpu.VMEM((MOE_SLOT_ROWS, d), BF16),
                            pltpu.SemaphoreType.DMA((2,))],
        ),
        out_shape=jax.ShapeDtypeStruct((rows, d_exp), BF16),
        compiler_params=_params("arbitrary", "arbitrary"),
        name=f"moe_gate_up_{layer}",
    )(slot_expert, slot_start, n_used, tok_sorted, x_rows, w_gate_up, w_gate_up)

    nj2 = d // tn

    def wd_map(s, j, se, nu):
        se_, je_ = _slot_index(s, j, nu, nj2)
        return (layer, se[se_], 0, je_)

    return pl.pallas_call(
        _down_kernel,
        grid_spec=pltpu.PrefetchScalarGridSpec(
            num_scalar_prefetch=2,
            grid=(n_slots, nj2),
            in_specs=[
                pl.BlockSpec((MOE_SLOT_ROWS, d_exp), lambda s, j, se, nu: (_slot_index(s, j, nu, nj2)[0], 0)),
                pl.BlockSpec((None, None, d_exp, tn), wd_map),
            ],
            out_specs=pl.BlockSpec((MOE_SLOT_ROWS, tn), lambda s, j, se, nu: _slot_index(s, j, nu, nj2)),
        ),
        out_shape=jax.ShapeDtypeStruct((rows, d), F32),
        compiler_params=_params("arbitrary", "arbitrary"),
        name=f"moe_down_{layer}",
    )(slot_expert, n_used, h, w_down)


def _moe_layer(x_rows, experts, gates, w_gate_up, w_down, layer, ln_g, ln_b, *, tm_ln):
    n, d = x_rows.shape
    n_assign = n * TOP_K
    r = MOE_SLOT_ROWS
    flat_e = experts.reshape(-1)
    order = jnp.argsort(flat_e).astype(jnp.int32)
    e_sorted = flat_e[order]
    tok_sorted = order // TOP_K
    counts = jnp.bincount(flat_e, length=N_EXPERTS).astype(jnp.int32)
    starts = jnp.cumsum(counts) - counts
    slots_per = (counts + r - 1) // r
    slot_end = jnp.cumsum(slots_per)
    slot_first = slot_end - slots_per
    n_slots = -(-n_assign // r) + N_EXPERTS
    sid = jnp.arange(n_slots + 1, dtype=jnp.int32)
    used = sid < slot_end[-1]
    slot_expert = jnp.minimum(jnp.searchsorted(slot_end, sid, side='right'), N_EXPERTS - 1).astype(jnp.int32)
    local = sid - slot_first[slot_expert]
    slot_start = jnp.where(used, starts[slot_expert] + local * r, 0).astype(jnp.int32)
    n_used = slot_end[-1].astype(jnp.int32).reshape(1)
    tok_sorted = jnp.pad(tok_sorted, (0, r))
    rank = jnp.arange(n_assign, dtype=jnp.int32) - starts[e_sorted]
    dest_sorted = (slot_first[e_sorted] + rank // r) * r + rank % r
    dest = jnp.zeros((n_assign,), jnp.int32).at[order].set(dest_sorted).reshape(n, TOP_K)
    ys = _expert_ffn(x_rows, slot_expert, slot_start, n_used, tok_sorted, w_gate_up, w_down, layer, th=256, tn=1024)
    return _ln_moe(x_rows, ys, dest, gates, ln_g, ln_b, tm=tm_ln, name=f"ln_moe_{layer}")


def kernel(x_prompt, x_sample, cache_win_k, cache_win_v, state_conv, state_hgrn, w_in_ab, rel_bias, conv_w,
           conv_b, conv_norm_g, conv_norm_b, w_out_ab, w_in_c, hgrn_lb, hgrn_norm_g, w_out_c, ln_g, ln_b,
           router_w, router_b, w_gate_up, w_down):
    bp, sp, d = x_prompt.shape
    bs, ts, _ = x_sample.shape
    n_p, n_s = bp * sp, bs * ts
    attn_w = d // 2
    n_ah = attn_w // HEAD_DIM
    conv_ch = d - attn_w
    n_hh = d // HEAD_DIM
    keep = CONV_WIDTH - 1

    lb_soft = jax.nn.softmax(hgrn_lb.astype(F32), axis=0)
    lower_bounds = jnp.cumsum(lb_soft, axis=0) - lb_soft[0]
    perm = (jnp.arange(N_EXPERTS) % N_GROUPS) * GROUP_SIZE + jnp.arange(N_EXPERTS) // N_GROUPS
    router_wt = router_w.T[perm]
    router_bc = router_b.astype(F32)[perm].reshape(N_EXPERTS, 1)

    n_tot = n_p + n_s
    x = jnp.concatenate([x_prompt.reshape(n_p, d), x_sample.reshape(n_s, d)], axis=0)
    xb = x.astype(BF16)
    outs = {}
    for l in range(DEPTH):
        if l % 2 == 0:
            a = l // 2
            n_in = w_in_ab.shape[2]
            proj_p = _matmul([xb], w_in_ab, a, n_in, tm=512, tn=512, rows=n_p, name=f"ab_in_p{l}")
            proj_s = _matmul([xb], w_in_ab, a, n_in, tm=n_s, tn=512, rows=n_s, x_row0=n_p, name=f"ab_in_s{l}")
            pp3 = proj_p.reshape(bp, sp, n_in)
            attn_p = _attn_prompt(pp3, rel_bias, n_ah, tq=256)
            conv_p, tail_p = _conv_module(pp3, 3 * attn_w // conv_ch, jnp.zeros((bp, keep, conv_ch), F32),
                                          conv_w[a], conv_b[a], conv_norm_g[a], conv_norm_b[a],
                                          ch=conv_ch, t=128, out_dtype=BF16, name=f"conv_p{l}")
            attn_s = _attn_sample(proj_s, cache_win_k[a].reshape(bs, -1, attn_w),
                                  cache_win_v[a].reshape(bs, -1, attn_w), rel_bias, n_ah, ts, hg=4)
            conv_s, tail_s = _conv_module(proj_s.reshape(bs, ts, n_in), 3 * attn_w // conv_ch, state_conv[a],
                                          conv_w[a], conv_b[a], conv_norm_g[a], conv_norm_b[a],
                                          ch=conv_ch, t=ts, out_dtype=F32, name=f"conv_s{l}")
            acts_p = [attn_p.reshape(n_p, attn_w), conv_p.reshape(n_p, conv_ch)]
            acts_s = [attn_s.astype(BF16), conv_s.reshape(n_s, conv_ch).astype(BF16)]
            w_out, lw = w_out_ab, a
            outs.setdefault("wk_p", []).append(proj_p[:, attn_w:2 * attn_w].reshape(bp, sp, n_ah, HEAD_DIM))
            outs.setdefault("wv_p", []).append(proj_p[:, 2 * attn_w:3 * attn_w].reshape(bp, sp, n_ah, HEAD_DIM))
            outs.setdefault("cv_p", []).append(tail_p)
            outs.setdefault("wk_s", []).append(proj_s[:, attn_w:2 * attn_w].reshape(bs, ts, n_ah, HEAD_DIM))
            outs.setdefault("wv_s", []).append(proj_s[:, 2 * attn_w:3 * attn_w].reshape(bs, ts, n_ah, HEAD_DIM))
            outs.setdefault("cv_s", []).append(tail_s)
        else:
            c = l // 2
            n_in = w_in_c.shape[2]
            proj_p = _matmul([xb], w_in_c, c, n_in, tm=512, tn=512, rows=n_p, name=f"c_in_p{l}")
            proj_s = _matmul([xb], w_in_c, c, n_in, tm=n_s, tn=512, rows=n_s, x_row0=n_p, name=f"c_in_s{l}")
            o_p, h_p = _hgrn(proj_p.reshape(bp, sp, n_in), jnp.zeros((bp, n_hh, HEAD_DIM, HEAD_DIM), F32),
                             lower_bounds[l], hgrn_norm_g[c], n_hh, hg=4, t=256, c=64,
                             out_dtype=BF16, name=f"hgrn_p{l}")
            o_s, h_s = _hgrn(proj_s.reshape(bs, ts, n_in), state_hgrn[c], lower_bounds[l], hgrn_norm_g[c],
                             n_hh, hg=4, t=ts, c=ts, out_dtype=F32, name=f"hgrn_s{l}")
            acts_p = [o_p.reshape(n_p, d)]
            acts_s = [o_s.reshape(n_s, d).astype(BF16)]
            w_out, lw = w_out_c, c
            outs.setdefault("hs_p", []).append(h_p)
            outs.setdefault("hs_s", []).append(h_s)
        m = _matmul(acts_p, w_out, lw, d, tm=512, tn=512, rows=n_p, out_rows=n_tot, name=f"mix_out_p{l}")
        m = _matmul(acts_s, w_out, lw, d, tm=n_s, tn=512, rows=n_s, out_rows=n_tot, out_row0=n_p, into=m,
                    name=f"mix_out_s{l}")
        x, xb = _ln_res(x, m, ln_g[l, 0], ln_b[l, 0], tm=256, name=f"ln_mix{l}")
        experts, gates = _router(xb, router_wt, router_bc, tm=512, name=f"router{l}")
        x, xb = _moe_layer(x, experts.T, gates.T, w_gate_up, w_down, l, ln_g[l, 1], ln_b[l, 1], tm_ln=128)
    stack = lambda k: jnp.stack(outs[k])
    return (x[:n_p].reshape(bp, sp, d), x[n_p:].reshape(bs, ts, d), stack("wk_p"), stack("wv_p"), stack("cv_p"),
            stack("hs_p"), stack("wk_s"), stack("wv_s"), stack("cv_s"), stack("hs_s"))
```

```python
import functools
import math

import jax
import jax.numpy as jnp
from jax import lax
from jax.experimental import pallas as pl
from jax.experimental.pallas import tpu as pltpu

F32 = jnp.float32
BF16 = jnp.bfloat16

DEPTH = 2
HEAD_DIM = 128
DILATED_PATTERNS = ((128, 1), (512, 4), (2048, 16))
NUM_BUCKETS = 32
MAX_EXACT = 16
MAX_DISTANCE = 2048
CONV_WIDTH = 31
N_EXPERTS = 32
N_GROUPS = 8
GROUP_SIZE = N_EXPERTS // N_GROUPS
TOP_K = 2
ALPHA = (2.0 * DEPTH) ** 0.25
EPS = 1e-5
NEG = -1e30
VMEM_LIMIT = 56 * 1024 * 1024
CONV_HALO = 32


def _params(*sem):
    return pltpu.CompilerParams(dimension_semantics=sem, vmem_limit_bytes=VMEM_LIMIT)


def _sigmoid(x):
    return 1.0 / (1.0 + jnp.exp(-x))


def _silu(x):
    return x * _sigmoid(x)


def _dot_nt(a, b, precision=None):
    return lax.dot_general(a, b, (((1,), (1,)), ((), ())), preferred_element_type=F32, precision=precision)


def _dot_tn(a, b, precision=None):
    return lax.dot_general(a, b, (((0,), (0,)), ((), ())), preferred_element_type=F32, precision=precision)


def _dot(a, b, precision=None):
    return jnp.dot(a, b, preferred_element_type=F32, precision=precision)


def _mm_kernel(*refs, n_x, n_main):
    xm_refs, xe_refs = refs[:n_x], refs[n_x:2 * n_x]
    w_ref, om_ref, oe_ref, wb_ref = refs[2 * n_x:]
    m = pl.program_id(1)

    @pl.when(m == 0)
    def _():
        wb_ref[...] = w_ref[...].astype(BF16)

    def product(x_refs, o_ref):
        acc = None
        off = 0
        for xr in x_refs:
            k = xr.shape[-1]
            part = _dot(xr[...], wb_ref[off:off + k, :])
            acc = part if acc is None else acc + part
            off += k
        o_ref[...] = acc.astype(o_ref.dtype)

    @pl.when(m < n_main)
    def _():
        product(xm_refs, om_ref)

    @pl.when(m == n_main)
    def _():
        product(xe_refs, oe_ref)


def _matmul(xs, xs_extra, w, layer, n_cols, *, tm, tn, name, rows, extra_rows, extra_row0=0):
    k_tot = sum(x.shape[1] for x in xs)
    assert w.shape[1] == k_tot and rows % tm == 0 and n_cols % tn == 0 and extra_row0 % extra_rows == 0
    assert [x.shape[1] for x in xs] == [x.shape[1] for x in xs_extra]
    n_main = rows // tm
    eoff = extra_row0 // extra_rows
    main_row = lambda m: jnp.minimum(m, n_main - 1)
    in_specs = [pl.BlockSpec((tm, x.shape[1]), lambda n, m: (main_row(m), 0)) for x in xs]
    in_specs += [pl.BlockSpec((extra_rows, x.shape[1]), lambda n, m: (eoff, 0)) for x in xs_extra]
    in_specs.append(pl.BlockSpec((None, k_tot, tn), lambda n, m: (layer, 0, n)))
    return pl.pallas_call(
        functools.partial(_mm_kernel, n_x=len(xs), n_main=n_main),
        grid=(n_cols // tn, n_main + 1),
        in_specs=in_specs,
        out_specs=[pl.BlockSpec((tm, tn), lambda n, m: (main_row(m), n)),
                   pl.BlockSpec((extra_rows, tn), lambda n, m: (0, n))],
        out_shape=[jax.ShapeDtypeStruct((rows, n_cols), F32), jax.ShapeDtypeStruct((extra_rows, n_cols), F32)],
        scratch_shapes=[pltpu.VMEM((k_tot, tn), BF16)],
        compiler_params=_params("arbitrary", "arbitrary"),
        name=name,
    )(*xs, *xs_extra, w)


def _layer_norm_rows(z, g, b):
    mu = jnp.mean(z, -1, keepdims=True)
    zc = z - mu
    var = jnp.mean(zc * zc, -1, keepdims=True)
    return zc * lax.rsqrt(var + EPS) * g + b


LN_ROW_CHUNK = 32


def _ln_res_kernel(x_ref, m_ref, me_ref, g_ref, b_ref, of_ref, ob_ref):
    last = pl.program_id(0) == pl.num_programs(0) - 1
    n_extra = me_ref.shape[0]
    for r0 in range(0, x_ref.shape[0], LN_ROW_CHUNK):
        rows = slice(r0, r0 + LN_ROW_CHUNK)
        mix = m_ref[rows, :]
        if r0 < n_extra:
            mix = jnp.where(last, me_ref[rows, :], mix)
        y = _layer_norm_rows(ALPHA * x_ref[rows, :] + mix, g_ref[...], b_ref[...])
        of_ref[rows, :] = y
        ob_ref[rows, :] = y.astype(BF16)


def _ln_res(x, m_main, m_extra, g, b, *, tm, name):
    rows, d = x.shape
    n_main, n_extra = m_main.shape[0], m_extra.shape[0]
    assert n_main % tm == 0 and n_main + n_extra == rows and n_extra % LN_ROW_CHUNK == 0 and n_extra <= tm
    row_spec = pl.BlockSpec((tm, d), lambda i: (i, 0))
    vec_spec = pl.BlockSpec((1, d), lambda i: (0, 0))
    return pl.pallas_call(
        _ln_res_kernel,
        grid=(n_main // tm + 1,),
        in_specs=[row_spec, pl.BlockSpec((tm, d), lambda i: (jnp.minimum(i, n_main // tm - 1), 0)),
                  pl.BlockSpec((n_extra, d), lambda i: (0, 0)), vec_spec, vec_spec],
        out_specs=[row_spec, row_spec],
        out_shape=[jax.ShapeDtypeStruct((rows, d), F32), jax.ShapeDtypeStruct((rows, d), BF16)],
        compiler_params=_params("arbitrary"),
        name=name,
    )(x, m_main, m_extra, g.reshape(1, d), b.reshape(1, d))


def _ln_moe_kernel(dest_ref, x_ref, gt_ref, g_ref, b_ref, ys_hbm, oa_ref, ob_ref, yg_ref, sem,
                   *, tm, n_steps, split):
    i = pl.program_id(0)

    def start_rows(step):
        buf = step % 2
        base = step * (tm * TOP_K)
        for r in range(tm):
            for k in range(TOP_K):
                row = dest_ref[base + r * TOP_K + k]
                pltpu.make_async_copy(ys_hbm.at[pl.ds(row, 1), :], yg_ref.at[buf, pl.ds(k * tm + r, 1), :],
                                      sem.at[buf]).start()

    def wait_buffer(step):
        buf = step % 2
        pltpu.make_async_copy(ys_hbm.at[pl.ds(0, TOP_K * tm), :], yg_ref.at[buf], sem.at[buf]).wait()

    @pl.when(i == 0)
    def _():
        start_rows(0)

    wait_buffer(i)
    start_rows(i + 1)
    buf = i % 2
    for r0 in range(0, tm, LN_ROW_CHUNK):
        rows = slice(r0, r0 + LN_ROW_CHUNK)
        gt = gt_ref[rows, :]
        ff = (gt[:, 0:1] * yg_ref[buf, r0:r0 + LN_ROW_CHUNK, :]
              + gt[:, 1:2] * yg_ref[buf, tm + r0:tm + r0 + LN_ROW_CHUNK, :])
        y = _layer_norm_rows(ALPHA * x_ref[rows, :] + ff, g_ref[...], b_ref[...])
        if not split:
            oa_ref[rows, :] = y
            ob_ref[rows, :] = y.astype(BF16)
        else:
            @pl.when(i < n_steps - 1)
            def _():
                oa_ref[rows, :] = y

            if r0 < ob_ref.shape[0]:
                @pl.when(i == n_steps - 1)
                def _():
                    ob_ref[rows, :] = y

    @pl.when(i == n_steps - 1)
    def _():
        wait_buffer(i + 1)


def _ln_moe(x, ys, dest, gates, g, b, *, tm, name, split_at=None):
    rows, d = x.shape
    n_steps = pl.cdiv(rows, tm)
    dest_flat = jnp.pad(dest.reshape(-1), (0, (n_steps + 1) * tm * TOP_K - rows * TOP_K))
    row_spec = pl.BlockSpec((tm, d), lambda i, dr: (i, 0))
    vec_spec = pl.BlockSpec((1, d), lambda i, dr: (0, 0))
    if split_at is None:
        out_specs = [row_spec, row_spec]
        out_shape = [jax.ShapeDtypeStruct((rows, d), F32), jax.ShapeDtypeStruct((rows, d), BF16)]
    else:
        n_tail = rows - split_at
        assert split_at == (n_steps - 1) * tm and n_tail % LN_ROW_CHUNK == 0
        out_specs = [pl.BlockSpec((tm, d), lambda i, dr: (jnp.minimum(i, n_steps - 2), 0)),
                     pl.BlockSpec((n_tail, d), lambda i, dr: (0, 0))]
        out_shape = [jax.ShapeDtypeStruct((split_at, d), F32), jax.ShapeDtypeStruct((n_tail, d), F32)]
    return pl.pallas_call(
        functools.partial(_ln_moe_kernel, tm=tm, n_steps=n_steps, split=split_at is not None),
        grid_spec=pltpu.PrefetchScalarGridSpec(
            num_scalar_prefetch=1,
            grid=(n_steps,),
            in_specs=[row_spec, pl.BlockSpec((tm, TOP_K), lambda i, dr: (i, 0)), vec_spec, vec_spec,
                      pl.BlockSpec(memory_space=pl.ANY)],
            out_specs=out_specs,
            scratch_shapes=[pltpu.VMEM((2, TOP_K * tm, d), F32), pltpu.SemaphoreType.DMA((2,))],
        ),
        out_shape=out_shape,
        compiler_params=_params("arbitrary"),
        name=name,
    )(dest_flat, x, gates, g.reshape(1, d), b.reshape(1, d), ys)


def _rel_bucket(dist):
    dist = dist.astype(jnp.int32)
    d = jnp.maximum(dist, 1).astype(F32)
    large = MAX_EXACT + (jnp.log(d / MAX_EXACT) / math.log(MAX_DISTANCE / MAX_EXACT)
                         * (NUM_BUCKETS - MAX_EXACT)).astype(jnp.int32)
    large = jnp.minimum(large, NUM_BUCKETS - 1)
    return jnp.where(dist < MAX_EXACT, dist, large)


def _pattern_bias(rel_bias, d):
    base = jnp.moveaxis(rel_bias[_rel_bucket(jnp.maximum(d, 0))].astype(F32), -1, 0)
    outs = []
    for w, r in DILATED_PATTERNS:
        ok = (d >= 0) & (d % r == 0) & (d <= w)
        outs.append(jnp.where(ok[None], base, NEG))
    return jnp.stack(outs)


def _merged_bias(rel_bias, d):
    count = sum(((d >= 0) & (d % r == 0) & (d <= w)).astype(F32) for w, r in DILATED_PATTERNS)
    base = jnp.moveaxis(rel_bias[_rel_bucket(jnp.maximum(d, 0))].astype(F32), -1, 0)
    return jnp.where((count > 0)[None], base + jnp.log(jnp.maximum(count, 1.0))[None], NEG)


def _toeplitz_tiles(vec, nd, tq):
    h = vec.shape[0]
    span = 2 * tq - 1
    win = jnp.stack([vec[:, k * tq:k * tq + span] for k in range(nd)], 1)
    rev = jnp.pad(win[..., ::-1], ((0, 0), (0, 0), (0, 1)))
    skew = jnp.broadcast_to(rev[:, :, None, :], (h, nd, tq, 2 * tq)).reshape(h, nd, tq * 2 * tq)
    skew = skew[..., :tq * span].reshape(h, nd, tq, span)
    return skew[..., tq - 1:span]


def _attn_prompt_kernel(q_ref, k_ref, v_ref, bias_ref, o_ref, kb_ref, vb_ref, *, tq):
    i = pl.program_id(1)
    nb = q_ref.shape[0]

    @pl.when(i == 0)
    def _():
        kb_ref[...] = k_ref[...].astype(BF16)
        vb_ref[...] = v_ref[...].astype(BF16)

    qs = [(q_ref[b] * (HEAD_DIM ** -0.5)).astype(BF16) for b in range(nb)]

    def body(j, carry):
        start = pl.multiple_of(j * tq, tq)
        bias = bias_ref[i - j]
        out = []
        for b in range(nb):
            m, l, acc = carry[b]
            s = _dot_nt(qs[b], kb_ref[b, pl.ds(start, tq), :]) + bias
            m_new = jnp.maximum(m, jnp.max(s, -1, keepdims=True))
            p = jnp.exp(s - m_new)
            a = jnp.exp(m - m_new)
            l = a * l + jnp.sum(p, -1, keepdims=True)
            acc = a * acc + _dot(p.astype(BF16), vb_ref[b, pl.ds(start, tq), :])
            out.append((m_new, l, acc))
        return tuple(out)

    one = (jnp.full((tq, 1), NEG, F32), jnp.zeros((tq, 1), F32), jnp.zeros((tq, HEAD_DIM), F32))
    res = lax.fori_loop(0, i + 1, body, (one,) * nb)
    for b in range(nb):
        _, l, acc = res[b]
        o_ref[b] = (acc / l).astype(o_ref.dtype)


def _attn_prompt(proj, rel_bias, n_heads, *, tq):
    b, s, _ = proj.shape
    nd = s // tq
    table = _toeplitz_tiles(_merged_bias(rel_bias, jnp.arange(-(tq - 1), s)), nd, tq)
    return pl.pallas_call(
        functools.partial(_attn_prompt_kernel, tq=tq),
        grid=(n_heads, nd),
        in_specs=[
            pl.BlockSpec((b, tq, HEAD_DIM), lambda h, i: (0, i, h)),
            pl.BlockSpec((b, s, HEAD_DIM), lambda h, i: (0, 0, n_heads + h)),
            pl.BlockSpec((b, s, HEAD_DIM), lambda h, i: (0, 0, 2 * n_heads + h)),
            pl.BlockSpec((None, nd, tq, tq), lambda h, i: (h, 0, 0, 0)),
        ],
        out_specs=pl.BlockSpec((b, tq, HEAD_DIM), lambda h, i: (0, i, h)),
        out_shape=jax.ShapeDtypeStruct((b, s, n_heads * HEAD_DIM), BF16),
        scratch_shapes=[pltpu.VMEM((b, s, HEAD_DIM), BF16), pltpu.VMEM((b, s, HEAD_DIM), BF16)],
        compiler_params=_params("arbitrary", "arbitrary"),
        name="attn_prompt",
    )(proj, proj, proj, table)


QUERY_PAD = 16


def _attn_sample_kernel(q_ref, kn_ref, vn_ref, kc_ref, vc_ref, bias_ref, o_ref, kx_ref, vx_ref, *, hg, t):
    p = kc_ref.shape[0]
    n_keys, w = kx_ref.shape
    zrow = jnp.zeros((n_keys - p - t, w), F32)
    kx_ref[0:p, :] = kc_ref[...].astype(BF16)
    vx_ref[0:p, :] = vc_ref[...].astype(BF16)
    kx_ref[p:n_keys, :] = jnp.concatenate([kn_ref[...], zrow], 0).astype(BF16)
    vx_ref[p:n_keys, :] = jnp.concatenate([vn_ref[...], zrow], 0).astype(BF16)
    n_pat = bias_ref.shape[0]
    add = lambda a, b: a + b
    for h in range(hg):
        cs = slice(h * HEAD_DIM, (h + 1) * HEAD_DIM)
        q16 = jnp.concatenate([q_ref[:, cs], jnp.zeros((QUERY_PAD - t, HEAD_DIM), F32)], 0).astype(BF16)
        s = _dot_nt(q16, kx_ref[:, cs])[0:t, :] * (HEAD_DIM ** -0.5)
        outs, lses = [], []
        for g in range(n_pat):
            lg = s + bias_ref[g, h]
            m = jnp.max(lg, -1, keepdims=True)
            e = jnp.exp(lg - m)
            den = jnp.sum(e, -1, keepdims=True)
            p16 = jnp.concatenate([e / den, jnp.zeros((QUERY_PAD - t, n_keys), F32)], 0).astype(BF16)
            outs.append(_dot(p16, vx_ref[:, cs])[0:t, :])
            lses.append(m + jnp.log(den))
        top = functools.reduce(jnp.maximum, lses)
        ws = [jnp.exp(l - top) for l in lses]
        tot = functools.reduce(add, ws)
        o_ref[:, cs] = functools.reduce(add, [(wg / tot) * og for wg, og in zip(ws, outs)])


def _attn_sample(proj, cache_k, cache_v, rel_bias, n_heads, t, *, hg):
    bt = proj.shape[0]
    b = bt // t
    p = cache_k.shape[1]
    w = hg * HEAD_DIM
    nhg = n_heads // hg
    assert t <= QUERY_PAD and p % QUERY_PAD == 0
    n_keys = p + QUERY_PAD
    key = jnp.arange(n_keys)
    dist = jnp.where(key[None, :] < p + t, p + jnp.arange(t)[:, None] - key[None, :], -1)
    bias = _pattern_bias(rel_bias, dist)
    n_pat = bias.shape[0]
    return pl.pallas_call(
        functools.partial(_attn_sample_kernel, hg=hg, t=t),
        grid=(b, nhg),
        in_specs=[
            pl.BlockSpec((t, w), lambda bb, g: (bb, g)),
            pl.BlockSpec((t, w), lambda bb, g: (bb, nhg + g)),
            pl.BlockSpec((t, w), lambda bb, g: (bb, 2 * nhg + g)),
            pl.BlockSpec((None, p, w), lambda bb, g: (bb, 0, g)),
            pl.BlockSpec((None, p, w), lambda bb, g: (bb, 0, g)),
            pl.BlockSpec((n_pat, hg, t, n_keys), lambda bb, g: (0, g, 0, 0)),
        ],
        out_specs=pl.BlockSpec((t, w), lambda bb, g: (bb, g)),
        out_shape=jax.ShapeDtypeStruct((bt, n_heads * HEAD_DIM), F32),
        scratch_shapes=[pltpu.VMEM((n_keys, w), BF16), pltpu.VMEM((n_keys, w), BF16)],
        compiler_params=_params("arbitrary", "arbitrary"),
        name="attn_sample",
    )(proj, proj, proj, cache_k, cache_v, bias)


def _round_bf16(x):
    return x.astype(BF16).astype(F32)


CONV_ALIGN = 8


def _conv_kernel(a_ref, g_ref, st_ref, w_ref, cb_ref, ng_ref, nb_ref, o_ref, tail_ref, ext_ref, y_ref,
                 extr_ref, sh_ref, *, t, rc, cc):
    ti = pl.program_id(1)
    ch = a_ref.shape[-1]
    n_ext = CONV_HALO + t

    @pl.when(ti == 0)
    def _():
        ext_ref[0:CONV_HALO, :] = st_ref[...]
        extr_ref[n_ext:n_ext + CONV_ALIGN, :] = jnp.zeros((CONV_ALIGN, ch), F32)

    ext_ref[CONV_HALO:n_ext, :] = a_ref[...] * _sigmoid(g_ref[...])
    extr_ref[0:n_ext, :] = _round_bf16(ext_ref[...])
    for s in range(CONV_ALIGN):
        sh_ref[s] = extr_ref[s:s + n_ext, :]
    first = CONV_HALO - (CONV_WIDTH - 1)
    for r0 in range(0, t, rc):
        for c0 in range(0, ch, cc):
            acc = jnp.zeros((rc, cc), F32)
            for j in range(CONV_WIDTH):
                s, base = (first + j) % CONV_ALIGN, (first + j) // CONV_ALIGN * CONV_ALIGN
                acc = acc + w_ref[j:j + 1, c0:c0 + cc] * sh_ref[s, base + r0:base + r0 + rc, c0:c0 + cc]
            y_ref[r0:r0 + rc, c0:c0 + cc] = acc + cb_ref[:, c0:c0 + cc]
    for r0 in range(0, t, rc):
        y = _layer_norm_rows(y_ref[r0:r0 + rc, :], ng_ref[...], nb_ref[...])
        o_ref[r0:r0 + rc, :] = _silu(y).astype(o_ref.dtype)

    @pl.when(ti == pl.num_programs(1) - 1)
    def _():
        tail_ref[...] = ext_ref[t + first:n_ext, :]

    ext_ref[0:CONV_HALO, :] = ext_ref[t:n_ext, :]


def _conv_module(proj, a_blk, state, conv_w, conv_b, ng, nb, *, ch, t, out_dtype, name):
    b, length, _ = proj.shape
    keep = CONV_WIDTH - 1
    st = jnp.pad(state, ((0, 0), (CONV_HALO - keep, 0), (0, 0)))
    rc = min(t, 32)
    vec_spec = pl.BlockSpec((1, ch), lambda bb, i: (0, 0))
    return pl.pallas_call(
        functools.partial(_conv_kernel, t=t, rc=rc, cc=512),
        grid=(b, length // t),
        in_specs=[
            pl.BlockSpec((None, t, ch), lambda bb, i: (bb, i, a_blk)),
            pl.BlockSpec((None, t, ch), lambda bb, i: (bb, i, a_blk + 1)),
            pl.BlockSpec((None, CONV_HALO, ch), lambda bb, i: (bb, 0, 0)),
            pl.BlockSpec((CONV_WIDTH, ch), lambda bb, i: (0, 0)),
            vec_spec, vec_spec, vec_spec,
        ],
        out_specs=[
            pl.BlockSpec((None, t, ch), lambda bb, i: (bb, i, 0)),
            pl.BlockSpec((None, keep, ch), lambda bb, i: (bb, 0, 0)),
        ],
        out_shape=[jax.ShapeDtypeStruct((b, length, ch), out_dtype),
                   jax.ShapeDtypeStruct((b, keep, ch), F32)],
        scratch_shapes=[pltpu.VMEM((CONV_HALO + t, ch), F32), pltpu.VMEM((t, ch), F32),
                        pltpu.VMEM((CONV_HALO + t + CONV_ALIGN, ch), F32),
                        pltpu.VMEM((CONV_ALIGN, CONV_HALO + t, ch), F32)],
        compiler_params=_params("arbitrary", "arbitrary"),
        name=name,
    )(proj, proj, st, conv_w, conv_b.reshape(1, ch), ng.reshape(1, ch), nb.reshape(1, ch))


def _split3(x):
    a1 = x.astype(BF16)
    r1 = x - a1.astype(F32)
    a2 = r1.astype(BF16)
    a3 = (r1 - a2.astype(F32)).astype(BF16)
    return a1, a2, a3


def _hgrn_gates(q, f, lb):
    fg = lb + (1.0 - lb) * _sigmoid(f)
    return _silu(q), 1.0 - fg, jnp.log(fg)


def _hgrn_finish(o, g, ng, dtype):
    normed = o * lax.rsqrt(jnp.mean(o * o, -1, keepdims=True) + EPS) * ng
    return (normed * _silu(g)).astype(dtype)


def _hgrn_chunk_kernel(q_ref, f_ref, i_ref, g_ref, lb_ref, ng_ref, s0_ref, o_ref, sfin_ref, st_ref, *, hg, t, c):
    ti = pl.program_id(2)
    dk = HEAD_DIM

    @pl.when(ti == 0)
    def _():
        for h in range(hg):
            st_ref[h] = s0_ref[h].T

    lb = lb_ref[...]
    ng = ng_ref[...]
    causal = lax.broadcasted_iota(jnp.int32, (c, c), 0) >= lax.broadcasted_iota(jnp.int32, (c, c), 1)
    tri = causal.astype(BF16)
    heads = [slice(h * dk, (h + 1) * dk) for h in range(hg)]
    chunks = [slice(c0, c0 + c) for c0 in range(0, t, c)]
    q_in, decs, intra, incr = [], [], [], []
    for rows in chunks:
        qq, kk, logf = _hgrn_gates(q_ref[rows, :], f_ref[rows, :], lb)
        l1, l2, l3 = _split3(logf)
        cum = _dot(tri, l1) + _dot(tri, l2) + _dot(tri, l3)
        mid = cum[c // 2 - 1:c // 2, :]
        last = cum[c - 1:c, :]
        q_md = (qq * jnp.exp(cum - mid)).astype(BF16)
        k_md = (kk * jnp.exp(mid - cum)).astype(BF16)
        k_end = (kk * jnp.exp(last - cum)).astype(BF16)
        q_in.append((qq * jnp.exp(cum)).astype(BF16))
        decs.append(jnp.exp(last))
        vs = [i_ref[rows, cs].astype(BF16) for cs in heads]
        scores = [jnp.where(causal, _dot_nt(q_md[:, cs], k_md[:, cs]), 0.0).astype(BF16) for cs in heads]
        intra.append([_dot(sc, v) for sc, v in zip(scores, vs)])
        incr.append([_dot_tn(v, k_end[:, cs]) for v, cs in zip(vs, heads)])
    states = [st_ref[h] for h in range(hg)]
    for n, rows in enumerate(chunks):
        for h, cs in enumerate(heads):
            o = intra[n][h] + _dot_nt(q_in[n][:, cs], states[h].astype(BF16))
            states[h] = states[h] * decs[n][:, cs] + incr[n][h]
            o_ref[rows, cs] = _hgrn_finish(o, g_ref[rows, cs], ng, o_ref.dtype)
    for h in range(hg):
        st_ref[h] = states[h]

    @pl.when(ti == pl.num_programs(2) - 1)
    def _():
        for h in range(hg):
            sfin_ref[h] = st_ref[h].T


def _hgrn_step_kernel(q_ref, f_ref, i_ref, g_ref, lb_ref, ng_ref, s0_ref, o_ref, sfin_ref, *, hg, c):
    dk = HEAD_DIM
    hi = lax.Precision.HIGHEST
    ng = ng_ref[...]
    step = lax.broadcasted_iota(jnp.int32, (c, 1), 0)
    qq, kk, logf = _hgrn_gates(q_ref[...], f_ref[...], lb_ref[...])
    acc = [logf[0:1, :]]
    for r in range(1, c):
        acc.append(acc[-1] + logf[r:r + 1, :])
    cum = jnp.concatenate(acc, 0)
    last = cum[c - 1:c, :]
    q_in = qq * jnp.exp(cum)
    k_end = kk * jnp.exp(last - cum)
    dec = jnp.exp(last)
    for h in range(hg):
        cs = slice(h * dk, (h + 1) * dk)
        st = s0_ref[h].T
        v = _round_bf16(i_ref[:, cs])
        o_rows = []
        for r in range(c):
            pair = qq[r:r + 1, cs] * jnp.exp(cum[r:r + 1, cs] - cum[:, cs]) * kk[:, cs]
            sc = jnp.where(step <= r, jnp.sum(pair, -1, keepdims=True), 0.0)
            o_rows.append(jnp.sum(_round_bf16(sc) * v, 0, keepdims=True))
        o = _dot_nt(_round_bf16(q_in[:, cs]), _round_bf16(st), hi) + jnp.concatenate(o_rows, 0)
        sfin_ref[h] = (st * dec[:, cs] + _dot_tn(v, _round_bf16(k_end[:, cs]), hi)).T
        o_ref[:, cs] = _hgrn_finish(o, g_ref[:, cs], ng, o_ref.dtype)


def _hgrn(proj, s0, lb, ng, n_heads, *, hg, t, c, out_dtype, name):
    b, length, _ = proj.shape
    w = hg * HEAD_DIM
    nhg = n_heads // hg
    col = lambda k: pl.BlockSpec((None, t, w), lambda bb, g, i: (bb, i, k * nhg + g))
    st_spec = pl.BlockSpec((None, hg, HEAD_DIM, HEAD_DIM), lambda bb, g, i: (bb, g, 0, 0))
    if t == length == c:
        body, scratch = functools.partial(_hgrn_step_kernel, hg=hg, c=c), []
    else:
        body = functools.partial(_hgrn_chunk_kernel, hg=hg, t=t, c=c)
        scratch = [pltpu.VMEM((hg, HEAD_DIM, HEAD_DIM), F32)]
    return pl.pallas_call(
        body,
        grid=(b, nhg, length // t),
        in_specs=[col(0), col(1), col(2), col(3),
                  pl.BlockSpec((1, w), lambda bb, g, i: (0, g)),
                  pl.BlockSpec((1, HEAD_DIM), lambda bb, g, i: (0, 0)),
                  st_spec],
        out_specs=[pl.BlockSpec((None, t, w), lambda bb, g, i: (bb, i, g)), st_spec],
        out_shape=[jax.ShapeDtypeStruct((b, length, n_heads * HEAD_DIM), out_dtype),
                   jax.ShapeDtypeStruct(s0.shape, F32)],
        scratch_shapes=scratch,
        compiler_params=_params("arbitrary", "arbitrary", "arbitrary"),
        name=name,
    )(proj, proj, proj, proj, lb.reshape(1, -1), ng.reshape(1, HEAD_DIM), s0)


def _top2(vals):
    n = len(vals)
    m1 = functools.reduce(jnp.maximum, vals)
    i1 = jnp.full(m1.shape, n - 1, jnp.int32)
    for k in range(n - 2, -1, -1):
        i1 = jnp.where(vals[k] == m1, k, i1)
    rest = [jnp.where(i1 == k, -1.0, vals[k]) for k in range(n)]
    m2 = functools.reduce(jnp.maximum, rest)
    i2 = jnp.full(m1.shape, n - 1, jnp.int32)
    for k in range(n - 2, -1, -1):
        i2 = jnp.where(rest[k] == m2, k, i2)
    return m1, i1, m2, i2


def _router_kernel(x_ref, w_ref, b_ref, ex_ref, gt_ref):
    logits = _dot_nt(w_ref[...].astype(BF16), x_ref[...]) + b_ref[...]
    e = jnp.exp(logits - jnp.max(logits, 0, keepdims=True))
    p = e / jnp.sum(e, 0, keepdims=True)
    members = [p[k * N_GROUPS:(k + 1) * N_GROUPS, :] for k in range(GROUP_SIZE)]
    m1, i1, m2, i2 = _top2(members)
    score = m1 + m2
    gid = lax.broadcasted_iota(jnp.int32, score.shape, 0)
    best = jnp.max(score, 0, keepdims=True)
    gsel = jnp.min(jnp.where(score == best, gid, N_GROUPS), 0, keepdims=True)
    sel = gid == gsel
    pick_f = lambda a: jnp.sum(jnp.where(sel, a, 0.0), 0, keepdims=True)
    pick_i = lambda a: jnp.sum(jnp.where(sel, a, 0), 0, keepdims=True)
    p1, p2 = pick_f(m1), pick_f(m2)
    ex_ref[0:1, :] = gsel * GROUP_SIZE + pick_i(i1)
    ex_ref[1:2, :] = gsel * GROUP_SIZE + pick_i(i2)
    gt_ref[0:1, :] = p1 / (p1 + p2)
    gt_ref[1:2, :] = p2 / (p1 + p2)


def _router(x, w_t, b_col, *, tm, name):
    n, d = x.shape
    out_spec = pl.BlockSpec((TOP_K, tm), lambda i: (0, i))
    return pl.pallas_call(
        _router_kernel,
        grid=(pl.cdiv(n, tm),),
        in_specs=[pl.BlockSpec((tm, d), lambda i: (i, 0)),
                  pl.BlockSpec((N_EXPERTS, d), lambda i: (0, 0)),
                  pl.BlockSpec((N_EXPERTS, 1), lambda i: (0, 0))],
        out_specs=[out_spec, out_spec],
        out_shape=[jax.ShapeDtypeStruct((TOP_K, n), jnp.int32), jax.ShapeDtypeStruct((TOP_K, n), F32)],
        compiler_params=_params("arbitrary"),
        name=name,
    )(x, w_t, b_col)


MOE_SLOT_ROWS = 576


def _slot_index(s, j, nu_ref, nj):
    used = s < nu_ref[0]
    return jnp.minimum(s, nu_ref[0] - 1), jnp.where(used, j, nj - 1)


def _gate_up_kernel(se_ref, ss_ref, nu_ref, tok_ref, x_hbm, wg_ref, wu_ref, h_ref, xg_ref, xb_ref, sem,
                    *, n_slots, nj):
    s = pl.program_id(0)
    j = pl.program_id(1)
    n_used = nu_ref[0]
    rows = xb_ref.shape[0]
    part = rows // nj

    def start_rows(slot, first, count):
        buf = slot % 2
        base = ss_ref[slot] + first
        for r in range(count):
            tok = tok_ref[base + r]
            pltpu.make_async_copy(x_hbm.at[pl.ds(tok, 1), :], xg_ref.at[buf, pl.ds(first + r, 1), :],
                                  sem.at[buf]).start()

    def wait_buffer(slot):
        buf = slot % 2
        pltpu.make_async_copy(x_hbm.at[pl.ds(0, rows), :], xg_ref.at[buf], sem.at[buf]).wait()

    @pl.when(jnp.logical_and(s == 0, j == 0))
    def _():
        start_rows(0, 0, rows)

    @pl.when(jnp.logical_and(s <= n_used, j == 0))
    def _():
        wait_buffer(s)

    @pl.when(jnp.logical_and(s < n_used, j == 0))
    def _():
        xb_ref[...] = xg_ref[s % 2].astype(BF16)

    @pl.when(s < n_used)
    def _():
        start_rows(s + 1, j * part, part)
        x = xb_ref[...]
        gate = _dot(x, wg_ref[...].astype(BF16))
        up = _dot(x, wu_ref[...].astype(BF16))
        h_ref[...] = (_silu(gate) * up).astype(h_ref.dtype)

    @pl.when(jnp.logical_and(jnp.logical_and(s == n_slots - 1, j == nj - 1), n_used == n_slots))
    def _():
        wait_buffer(n_slots)


def _down_kernel(se_ref, nu_ref, h_ref, wd_ref, y_ref):
    @pl.when(pl.program_id(0) < nu_ref[0])
    def _():
        y_ref[...] = _dot(h_ref[...], wd_ref[...].astype(BF16))


def _expert_ffn(x_rows, slot_expert, slot_start, n_used, tok_sorted, w_gate_up, w_down, layer, *, th, tn):
    n, d = x_rows.shape
    d_exp = w_down.shape[2]
    n_slots = slot_expert.shape[0] - 1
    rows = n_slots * MOE_SLOT_ROWS
    nj = d_exp // th

    def w_map(col0):
        def index(s, j, se, ss, nu, tok):
            se_, je_ = _slot_index(s, j, nu, nj)
            return (layer, se[se_], 0, col0 + je_)
        return index

    def h_map(s, j, se, ss, nu, tok):
        return _slot_index(s, j, nu, nj)

    h = pl.pallas_call(
        functools.partial(_gate_up_kernel, n_slots=n_slots, nj=nj),
        grid_spec=pltpu.PrefetchScalarGridSpec(
            num_scalar_prefetch=4,
            grid=(n_slots, nj),
            in_specs=[
                pl.BlockSpec(memory_space=pl.ANY),
                pl.BlockSpec((None, None, d, th), w_map(0)),
                pl.BlockSpec((None, None, d, th), w_map(nj)),
            ],
            out_specs=pl.BlockSpec((MOE_SLOT_ROWS, th), h_map),
            scratch_shapes=[pltpu.VMEM((2, MOE_SLOT_ROWS, d), F32), pltpu.VMEM((MOE_SLOT_ROWS, d), BF16),
                            pltpu.SemaphoreType.DMA((2,))],
        ),
        out_shape=jax.ShapeDtypeStruct((rows, d_exp), BF16),
        compiler_params=_params("arbitrary", "arbitrary"),
        name=f"moe_gate_up_{layer}",
    )(slot_expert, slot_start, n_used, tok_sorted, x_rows, w_gate_up, w_gate_up)

    nj2 = d // tn

    def wd_map(s, j, se, nu):
        se_, je_ = _slot_index(s, j, nu, nj2)
        return (layer, se[se_], 0, je_)

    return pl.pallas_call(
        _down_kernel,
        grid_spec=pltpu.PrefetchScalarGridSpec(
            num_scalar_prefetch=2,
            grid=(n_slots, nj2),
            in_specs=[
                pl.BlockSpec((MOE_SLOT_ROWS, d_exp), lambda s, j, se, nu: (_slot_index(s, j, nu, nj2)[0], 0)),
                pl.BlockSpec((None, None, d_exp, tn), wd_map),
            ],
            out_specs=pl.BlockSpec((MOE_SLOT_ROWS, tn), lambda s, j, se, nu: _slot_index(s, j, nu, nj2)),
        ),
        out_shape=jax.ShapeDtypeStruct((rows, d), F32),
        compiler_params=_params("arbitrary", "arbitrary"),
        name=f"moe_down_{layer}",
    )(slot_expert, n_used, h, w_down)


def _moe_layer(x_rows, experts, gates, w_gate_up, w_down, layer, ln_g, ln_b, *, tm_ln, split_at=None):
    n, d = x_rows.shape
    n_assign = n * TOP_K
    r = MOE_SLOT_ROWS
    flat_e = experts.reshape(-1)
    order = jnp.argsort(flat_e).astype(jnp.int32)
    e_sorted = flat_e[order]
    tok_sorted = order // TOP_K
    counts = jnp.bincount(flat_e, length=N_EXPERTS).astype(jnp.int32)
    starts = jnp.cumsum(counts) - counts
    slots_per = (counts + r - 1) // r
    slot_end = jnp.cumsum(slots_per)
    slot_first = slot_end - slots_per
    n_slots = -(-n_assign // r) + N_EXPERTS
    sid = jnp.arange(n_slots + 1, dtype=jnp.int32)
    used = sid < slot_end[-1]
    slot_expert = jnp.minimum(jnp.searchsorted(slot_end, sid, side='right'), N_EXPERTS - 1).astype(jnp.int32)
    local = sid - slot_first[slot_expert]
    slot_start = jnp.where(used, starts[slot_expert] + local * r, 0).astype(jnp.int32)
    n_used = slot_end[-1].astype(jnp.int32).reshape(1)
    tok_sorted = jnp.pad(tok_sorted, (0, r))
    rank = jnp.arange(n_assign, dtype=jnp.int32) - starts[e_sorted]
    dest_sorted = (slot_first[e_sorted] + rank // r) * r + rank % r
    dest = dest_sorted[jnp.argsort(order)].reshape(n, TOP_K)
    ys = _expert_ffn(x_rows, slot_expert, slot_start, n_used, tok_sorted, w_gate_up, w_down, layer, th=256, tn=1024)
    return _ln_moe(x_rows, ys, dest, gates, ln_g, ln_b, tm=tm_ln, name=f"ln_moe_{layer}", split_at=split_at)


def kernel(x_prompt, x_sample, cache_win_k, cache_win_v, state_conv, state_hgrn, w_in_ab, rel_bias, conv_w,
           conv_b, conv_norm_g, conv_norm_b, w_out_ab, w_in_c, hgrn_lb, hgrn_norm_g, w_out_c, ln_g, ln_b,
           router_w, router_b, w_gate_up, w_down):
    bp, sp, d = x_prompt.shape
    bs, ts, _ = x_sample.shape
    n_p, n_s = bp * sp, bs * ts
    attn_w = d // 2
    n_ah = attn_w // HEAD_DIM
    conv_ch = d - attn_w
    n_hh = d // HEAD_DIM
    keep = CONV_WIDTH - 1

    lb_soft = jax.nn.softmax(hgrn_lb.astype(F32), axis=0)
    lower_bounds = jnp.cumsum(lb_soft, axis=0) - lb_soft[0]
    perm = (jnp.arange(N_EXPERTS) % N_GROUPS) * GROUP_SIZE + jnp.arange(N_EXPERTS) // N_GROUPS
    router_wt = router_w.T[perm]
    router_bc = router_b.astype(F32)[perm].reshape(N_EXPERTS, 1)

    x = jnp.concatenate([x_prompt.reshape(n_p, d), x_sample.reshape(n_s, d)], axis=0)
    xb = x.astype(BF16)
    outs = {}
    for l in range(DEPTH):
        if l % 2 == 0:
            a = l // 2
            n_in = w_in_ab.shape[2]
            proj_p, proj_s = _matmul([xb], [xb], w_in_ab, a, n_in, tm=512, tn=512, rows=n_p, extra_rows=n_s,
                                     extra_row0=n_p, name=f"ab_in{l}")
            pp3 = proj_p.reshape(bp, sp, n_in)
            attn_p = _attn_prompt(pp3, rel_bias, n_ah, tq=256)
            conv_p, tail_p = _conv_module(pp3, 3 * attn_w // conv_ch, jnp.zeros((bp, keep, conv_ch), F32),
                                          conv_w[a], conv_b[a], conv_norm_g[a], conv_norm_b[a],
                                          ch=conv_ch, t=128, out_dtype=BF16, name=f"conv_p{l}")
            attn_s = _attn_sample(proj_s, cache_win_k[a].reshape(bs, -1, attn_w),
                                  cache_win_v[a].reshape(bs, -1, attn_w), rel_bias, n_ah, ts, hg=4)
            conv_s, tail_s = _conv_module(proj_s.reshape(bs, ts, n_in), 3 * attn_w // conv_ch, state_conv[a],
                                          conv_w[a], conv_b[a], conv_norm_g[a], conv_norm_b[a],
                                          ch=conv_ch, t=ts, out_dtype=F32, name=f"conv_s{l}")
            acts_p = [attn_p.reshape(n_p, attn_w), conv_p.reshape(n_p, conv_ch)]
            acts_s = [attn_s.astype(BF16), conv_s.reshape(n_s, conv_ch).astype(BF16)]
            w_out, lw = w_out_ab, a
            outs.setdefault("wk_p", []).append(proj_p[:, attn_w:2 * attn_w].reshape(bp, sp, n_ah, HEAD_DIM))
            outs.setdefault("wv_p", []).append(proj_p[:, 2 * attn_w:3 * attn_w].reshape(bp, sp, n_ah, HEAD_DIM))
            outs.setdefault("cv_p", []).append(tail_p)
            outs.setdefault("wk_s", []).append(proj_s[:, attn_w:2 * attn_w].reshape(bs, ts, n_ah, HEAD_DIM))
            outs.setdefault("wv_s", []).append(proj_s[:, 2 * attn_w:3 * attn_w].reshape(bs, ts, n_ah, HEAD_DIM))
            outs.setdefault("cv_s", []).append(tail_s)
        else:
            c = l // 2
            n_in = w_in_c.shape[2]
            proj_p, proj_s = _matmul([xb], [xb], w_in_c, c, n_in, tm=512, tn=512, rows=n_p, extra_rows=n_s,
                                     extra_row0=n_p, name=f"c_in{l}")
            o_p, h_p = _hgrn(proj_p.reshape(bp, sp, n_in), jnp.zeros((bp, n_hh, HEAD_DIM, HEAD_DIM), F32),
                             lower_bounds[l], hgrn_norm_g[c], n_hh, hg=4, t=256, c=64,
                             out_dtype=BF16, name=f"hgrn_p{l}")
            o_s, h_s = _hgrn(proj_s.reshape(bs, ts, n_in), state_hgrn[c], lower_bounds[l], hgrn_norm_g[c],
                             n_hh, hg=4, t=ts, c=ts, out_dtype=F32, name=f"hgrn_s{l}")
            acts_p = [o_p.reshape(n_p, d)]
            acts_s = [o_s.reshape(n_s, d).astype(BF16)]
            w_out, lw = w_out_c, c
            outs.setdefault("hs_p", []).append(h_p)
            outs.setdefault("hs_s", []).append(h_s)
        m_p, m_s = _matmul(acts_p, acts_s, w_out, lw, d, tm=512, tn=512, rows=n_p, extra_rows=n_s, name=f"mix_out{l}")
        x, xb = _ln_res(x, m_p, m_s, ln_g[l, 0], ln_b[l, 0], tm=256, name=f"ln_mix{l}")
        experts, gates = _router(xb, router_wt, router_bc, tm=512, name=f"router{l}")
        x, xb = _moe_layer(x, experts.T, gates.T, w_gate_up, w_down, l, ln_g[l, 1], ln_b[l, 1], tm_ln=128,
                           split_at=n_p if l == DEPTH - 1 else None)
    y_p, y_s = x, xb
    stack = lambda k: jnp.stack(outs[k])
    return (y_p.reshape(bp, sp, d), y_s.reshape(bs, ts, d), stack("wk_p"), stack("wv_p"), stack("cv_p"),
            stack("hs_p"), stack("wk_s"), stack("wv_s"), stack("cv_s"), stack("hs_s"))
```

```python
import functools
import math

import jax
import jax.numpy as jnp
from jax import lax
from jax.experimental import pallas as pl
from jax.experimental.pallas import tpu as pltpu

F32 = jnp.float32
BF16 = jnp.bfloat16

DEPTH = 2
HEAD_DIM = 128
DILATED_PATTERNS = ((128, 1), (512, 4), (2048, 16))
NUM_BUCKETS = 32
MAX_EXACT = 16
MAX_DISTANCE = 2048
CONV_WIDTH = 31
N_EXPERTS = 32
N_GROUPS = 8
GROUP_SIZE = N_EXPERTS // N_GROUPS
TOP_K = 2
ALPHA = (2.0 * DEPTH) ** 0.25
EPS = 1e-5
NEG = -1e30
VMEM_LIMIT = 56 * 1024 * 1024
CONV_HALO = 32


def _params(*sem):
    return pltpu.CompilerParams(dimension_semantics=sem, vmem_limit_bytes=VMEM_LIMIT)


def _sigmoid(x):
    return 1.0 / (1.0 + jnp.exp(-x))


def _silu(x):
    return x * _sigmoid(x)


def _dot_nt(a, b, precision=None):
    return lax.dot_general(a, b, (((1,), (1,)), ((), ())), preferred_element_type=F32, precision=precision)


def _dot_tn(a, b, precision=None):
    return lax.dot_general(a, b, (((0,), (0,)), ((), ())), preferred_element_type=F32, precision=precision)


def _dot(a, b, precision=None):
    return jnp.dot(a, b, preferred_element_type=F32, precision=precision)


def _mm_kernel(*refs, n_x, n_main):
    xm_refs, xe_refs = refs[:n_x], refs[n_x:2 * n_x]
    w_ref, om_ref, oe_ref, wb_ref = refs[2 * n_x:]
    m = pl.program_id(1)

    @pl.when(m == 0)
    def _():
        wb_ref[...] = w_ref[...].astype(BF16)

    def product(x_refs, o_ref):
        acc = None
        off = 0
        for xr in x_refs:
            k = xr.shape[-1]
            part = _dot(xr[...], wb_ref[off:off + k, :])
            acc = part if acc is None else acc + part
            off += k
        o_ref[...] = acc.astype(o_ref.dtype)

    @pl.when(m < n_main)
    def _():
        product(xm_refs, om_ref)

    @pl.when(m == n_main)
    def _():
        product(xe_refs, oe_ref)


def _matmul(xs, xs_extra, w, layer, n_cols, *, tm, tn, name, rows, extra_rows, extra_row0=0):
    k_tot = sum(x.shape[1] for x in xs)
    assert w.shape[1] == k_tot and rows % tm == 0 and n_cols % tn == 0 and extra_row0 % extra_rows == 0
    assert [x.shape[1] for x in xs] == [x.shape[1] for x in xs_extra]
    n_main = rows // tm
    eoff = extra_row0 // extra_rows
    main_row = lambda m: jnp.minimum(m, n_main - 1)
    in_specs = [pl.BlockSpec((tm, x.shape[1]), lambda n, m: (main_row(m), 0)) for x in xs]
    in_specs += [pl.BlockSpec((extra_rows, x.shape[1]), lambda n, m: (eoff, 0)) for x in xs_extra]
    in_specs.append(pl.BlockSpec((None, k_tot, tn), lambda n, m: (layer, 0, n)))
    return pl.pallas_call(
        functools.partial(_mm_kernel, n_x=len(xs), n_main=n_main),
        grid=(n_cols // tn, n_main + 1),
        in_specs=in_specs,
        out_specs=[pl.BlockSpec((tm, tn), lambda n, m: (main_row(m), n)),
                   pl.BlockSpec((extra_rows, tn), lambda n, m: (0, n))],
        out_shape=[jax.ShapeDtypeStruct((rows, n_cols), F32), jax.ShapeDtypeStruct((extra_rows, n_cols), F32)],
        scratch_shapes=[pltpu.VMEM((k_tot, tn), BF16)],
        compiler_params=_params("arbitrary", "arbitrary"),
        name=name,
    )(*xs, *xs_extra, w)


def _layer_norm_rows(z, g, b):
    mu = jnp.mean(z, -1, keepdims=True)
    zc = z - mu
    var = jnp.mean(zc * zc, -1, keepdims=True)
    return zc * lax.rsqrt(var + EPS) * g + b


LN_ROW_CHUNK = 32


def _ln_res_kernel(x_ref, xe_ref, m_ref, me_ref, g_ref, b_ref, of_ref, ob_ref):
    last = pl.program_id(0) == pl.num_programs(0) - 1
    n_extra = me_ref.shape[0]
    for r0 in range(0, x_ref.shape[0], LN_ROW_CHUNK):
        rows = slice(r0, r0 + LN_ROW_CHUNK)
        res, mix = x_ref[rows, :], m_ref[rows, :]
        if r0 < n_extra:
            res = jnp.where(last, xe_ref[rows, :], res)
            mix = jnp.where(last, me_ref[rows, :], mix)
        y = _layer_norm_rows(ALPHA * res + mix, g_ref[...], b_ref[...])
        of_ref[rows, :] = y
        ob_ref[rows, :] = y.astype(BF16)


def _ln_res(x_main, x_extra, m_main, m_extra, g, b, *, tm, name):
    d = x_main.shape[1]
    n_main, n_extra = m_main.shape[0], m_extra.shape[0]
    assert n_main % tm == 0 and x_extra.shape[0] % n_extra == 0 and n_extra % LN_ROW_CHUNK == 0 and n_extra <= tm
    rows = n_main + n_extra
    xe_blk = x_extra.shape[0] // n_extra - 1
    main_spec = pl.BlockSpec((tm, d), lambda i: (jnp.minimum(i, n_main // tm - 1), 0))
    row_spec = pl.BlockSpec((tm, d), lambda i: (i, 0))
    vec_spec = pl.BlockSpec((1, d), lambda i: (0, 0))
    return pl.pallas_call(
        _ln_res_kernel,
        grid=(n_main // tm + 1,),
        in_specs=[main_spec, pl.BlockSpec((n_extra, d), lambda i: (xe_blk, 0)),
                  main_spec, pl.BlockSpec((n_extra, d), lambda i: (0, 0)), vec_spec, vec_spec],
        out_specs=[row_spec, row_spec],
        out_shape=[jax.ShapeDtypeStruct((rows, d), F32), jax.ShapeDtypeStruct((rows, d), BF16)],
        compiler_params=_params("arbitrary"),
        name=name,
    )(x_main, x_extra, m_main, m_extra, g.reshape(1, d), b.reshape(1, d))


def _ln_moe_kernel(dest_ref, x_ref, gt_ref, g_ref, b_ref, ys_hbm, oa_ref, ob_ref, yg_ref, sem,
                   *, tm, n_steps, split):
    i = pl.program_id(0)

    def start_rows(step):
        buf = step % 2
        base = step * (tm * TOP_K)
        for r in range(tm):
            for k in range(TOP_K):
                row = dest_ref[base + r * TOP_K + k]
                pltpu.make_async_copy(ys_hbm.at[pl.ds(row, 1), :], yg_ref.at[buf, pl.ds(k * tm + r, 1), :],
                                      sem.at[buf]).start()

    def wait_buffer(step):
        buf = step % 2
        pltpu.make_async_copy(ys_hbm.at[pl.ds(0, TOP_K * tm), :], yg_ref.at[buf], sem.at[buf]).wait()

    @pl.when(i == 0)
    def _():
        start_rows(0)

    wait_buffer(i)
    start_rows(i + 1)
    buf = i % 2
    for r0 in range(0, tm, LN_ROW_CHUNK):
        rows = slice(r0, r0 + LN_ROW_CHUNK)
        gt = gt_ref[rows, :]
        ff = (gt[:, 0:1] * yg_ref[buf, r0:r0 + LN_ROW_CHUNK, :]
              + gt[:, 1:2] * yg_ref[buf, tm + r0:tm + r0 + LN_ROW_CHUNK, :])
        y = _layer_norm_rows(ALPHA * x_ref[rows, :] + ff, g_ref[...], b_ref[...])
        if not split:
            oa_ref[rows, :] = y
            ob_ref[rows, :] = y.astype(BF16)
        else:
            @pl.when(i < n_steps - 1)
            def _():
                oa_ref[rows, :] = y

            if r0 < ob_ref.shape[0]:
                @pl.when(i == n_steps - 1)
                def _():
                    ob_ref[rows, :] = y

    @pl.when(i == n_steps - 1)
    def _():
        wait_buffer(i + 1)


def _ln_moe(x, ys, dest, gates, g, b, *, tm, name, split_at=None):
    rows, d = x.shape
    n_steps = pl.cdiv(rows, tm)
    dest_flat = jnp.pad(dest.reshape(-1), (0, (n_steps + 1) * tm * TOP_K - rows * TOP_K))
    row_spec = pl.BlockSpec((tm, d), lambda i, dr: (i, 0))
    vec_spec = pl.BlockSpec((1, d), lambda i, dr: (0, 0))
    if split_at is None:
        out_specs = [row_spec, row_spec]
        out_shape = [jax.ShapeDtypeStruct((rows, d), F32), jax.ShapeDtypeStruct((rows, d), BF16)]
    else:
        n_tail = rows - split_at
        assert split_at == (n_steps - 1) * tm and n_tail % LN_ROW_CHUNK == 0
        out_specs = [pl.BlockSpec((tm, d), lambda i, dr: (jnp.minimum(i, n_steps - 2), 0)),
                     pl.BlockSpec((n_tail, d), lambda i, dr: (0, 0))]
        out_shape = [jax.ShapeDtypeStruct((split_at, d), F32), jax.ShapeDtypeStruct((n_tail, d), F32)]
    return pl.pallas_call(
        functools.partial(_ln_moe_kernel, tm=tm, n_steps=n_steps, split=split_at is not None),
        grid_spec=pltpu.PrefetchScalarGridSpec(
            num_scalar_prefetch=1,
            grid=(n_steps,),
            in_specs=[row_spec, pl.BlockSpec((tm, TOP_K), lambda i, dr: (i, 0)), vec_spec, vec_spec,
                      pl.BlockSpec(memory_space=pl.ANY)],
            out_specs=out_specs,
            scratch_shapes=[pltpu.VMEM((2, TOP_K * tm, d), F32), pltpu.SemaphoreType.DMA((2,))],
        ),
        out_shape=out_shape,
        compiler_params=_params("arbitrary"),
        name=name,
    )(dest_flat, x, gates, g.reshape(1, d), b.reshape(1, d), ys)


def _rel_bucket(dist):
    dist = dist.astype(jnp.int32)
    d = jnp.maximum(dist, 1).astype(F32)
    large = MAX_EXACT + (jnp.log(d / MAX_EXACT) / math.log(MAX_DISTANCE / MAX_EXACT)
                         * (NUM_BUCKETS - MAX_EXACT)).astype(jnp.int32)
    large = jnp.minimum(large, NUM_BUCKETS - 1)
    return jnp.where(dist < MAX_EXACT, dist, large)


def _pattern_bias(rel_bias, d):
    base = jnp.moveaxis(rel_bias[_rel_bucket(jnp.maximum(d, 0))].astype(F32), -1, 0)
    outs = []
    for w, r in DILATED_PATTERNS:
        ok = (d >= 0) & (d % r == 0) & (d <= w)
        outs.append(jnp.where(ok[None], base, NEG))
    return jnp.stack(outs)


def _merged_bias(rel_bias, d):
    count = sum(((d >= 0) & (d % r == 0) & (d <= w)).astype(F32) for w, r in DILATED_PATTERNS)
    base = jnp.moveaxis(rel_bias[_rel_bucket(jnp.maximum(d, 0))].astype(F32), -1, 0)
    return jnp.where((count > 0)[None], base + jnp.log(jnp.maximum(count, 1.0))[None], NEG)


def _toeplitz_tiles(vec, nd, tq):
    h = vec.shape[0]
    span = 2 * tq - 1
    win = jnp.stack([vec[:, k * tq:k * tq + span] for k in range(nd)], 1)
    rev = jnp.pad(win[..., ::-1], ((0, 0), (0, 0), (0, 1)))
    skew = jnp.broadcast_to(rev[:, :, None, :], (h, nd, tq, 2 * tq)).reshape(h, nd, tq * 2 * tq)
    skew = skew[..., :tq * span].reshape(h, nd, tq, span)
    return skew[..., tq - 1:span]


def _attn_prompt_kernel(q_ref, k_ref, v_ref, bias_ref, o_ref, kb_ref, vb_ref, *, tq):
    i = pl.program_id(1)
    nb = q_ref.shape[0]

    @pl.when(i == 0)
    def _():
        kb_ref[...] = k_ref[...].astype(BF16)
        vb_ref[...] = v_ref[...].astype(BF16)

    qs = [(q_ref[b] * (HEAD_DIM ** -0.5)).astype(BF16) for b in range(nb)]

    def body(j, carry):
        start = pl.multiple_of(j * tq, tq)
        bias = bias_ref[i - j]
        out = []
        for b in range(nb):
            m, l, acc = carry[b]
            s = _dot_nt(qs[b], kb_ref[b, pl.ds(start, tq), :]) + bias
            m_new = jnp.maximum(m, jnp.max(s, -1, keepdims=True))
            p = jnp.exp(s - m_new)
            a = jnp.exp(m - m_new)
            l = a * l + jnp.sum(p, -1, keepdims=True)
            acc = a * acc + _dot(p.astype(BF16), vb_ref[b, pl.ds(start, tq), :])
            out.append((m_new, l, acc))
        return tuple(out)

    one = (jnp.full((tq, 1), NEG, F32), jnp.zeros((tq, 1), F32), jnp.zeros((tq, HEAD_DIM), F32))
    res = lax.fori_loop(0, i + 1, body, (one,) * nb)
    for b in range(nb):
        _, l, acc = res[b]
        o_ref[b] = (acc / l).astype(o_ref.dtype)


def _attn_prompt(proj, rel_bias, n_heads, *, tq):
    b, s, _ = proj.shape
    nd = s // tq
    table = _toeplitz_tiles(_merged_bias(rel_bias, jnp.arange(-(tq - 1), s)), nd, tq)
    return pl.pallas_call(
        functools.partial(_attn_prompt_kernel, tq=tq),
        grid=(n_heads, nd),
        in_specs=[
            pl.BlockSpec((b, tq, HEAD_DIM), lambda h, i: (0, i, h)),
            pl.BlockSpec((b, s, HEAD_DIM), lambda h, i: (0, 0, n_heads + h)),
            pl.BlockSpec((b, s, HEAD_DIM), lambda h, i: (0, 0, 2 * n_heads + h)),
            pl.BlockSpec((None, nd, tq, tq), lambda h, i: (h, 0, 0, 0)),
        ],
        out_specs=pl.BlockSpec((b, tq, HEAD_DIM), lambda h, i: (0, i, h)),
        out_shape=jax.ShapeDtypeStruct((b, s, n_heads * HEAD_DIM), BF16),
        scratch_shapes=[pltpu.VMEM((b, s, HEAD_DIM), BF16), pltpu.VMEM((b, s, HEAD_DIM), BF16)],
        compiler_params=_params("arbitrary", "arbitrary"),
        name="attn_prompt",
    )(proj, proj, proj, table)


QUERY_PAD = 16


def _attn_sample_kernel(q_ref, kn_ref, vn_ref, bias_ref, kc_hbm, vc_hbm, o_ref, kc_ref, vc_ref, kx_ref, vx_ref,
                        sem, *, hg, t, nhg):
    step = pl.program_id(0) * nhg + pl.program_id(1)
    n_steps = pl.num_programs(0) * nhg
    p = kc_ref.shape[1]
    n_keys, w = kx_ref.shape

    def copies(s, slot):
        b_, g_ = s // nhg, s % nhg
        out = []
        for h in range(hg):
            lanes = pl.ds(h * HEAD_DIM, HEAD_DIM)
            out.append(pltpu.make_async_copy(kc_hbm.at[b_, :, g_ * hg + h, :], kc_ref.at[slot, :, lanes], sem.at[0, slot]))
            out.append(pltpu.make_async_copy(vc_hbm.at[b_, :, g_ * hg + h, :], vc_ref.at[slot, :, lanes], sem.at[1, slot]))
        return out

    @pl.when(step == 0)
    def _():
        for c in copies(0, 0):
            c.start()

    slot = step % 2

    @pl.when(step + 1 < n_steps)
    def _():
        for c in copies(step + 1, 1 - slot):
            c.start()

    for c in copies(step, slot):
        c.wait()
    zrow = jnp.zeros((n_keys - p - t, w), F32)
    kx_ref[0:p, :] = kc_ref[slot].astype(BF16)
    vx_ref[0:p, :] = vc_ref[slot].astype(BF16)
    kx_ref[p:n_keys, :] = jnp.concatenate([kn_ref[...], zrow], 0).astype(BF16)
    vx_ref[p:n_keys, :] = jnp.concatenate([vn_ref[...], zrow], 0).astype(BF16)
    n_pat = bias_ref.shape[0]
    add = lambda a, b: a + b
    for h in range(hg):
        cs = slice(h * HEAD_DIM, (h + 1) * HEAD_DIM)
        q16 = jnp.concatenate([q_ref[:, cs], jnp.zeros((QUERY_PAD - t, HEAD_DIM), F32)], 0).astype(BF16)
        s = _dot_nt(q16, kx_ref[:, cs])[0:t, :] * (HEAD_DIM ** -0.5)
        outs, lses = [], []
        for g in range(n_pat):
            lg = s + bias_ref[g, h]
            m = jnp.max(lg, -1, keepdims=True)
            e = jnp.exp(lg - m)
            den = jnp.sum(e, -1, keepdims=True)
            p16 = jnp.concatenate([e / den, jnp.zeros((QUERY_PAD - t, n_keys), F32)], 0).astype(BF16)
            outs.append(_dot(p16, vx_ref[:, cs])[0:t, :])
            lses.append(m + jnp.log(den))
        top = functools.reduce(jnp.maximum, lses)
        ws = [jnp.exp(l - top) for l in lses]
        tot = functools.reduce(add, ws)
        o_ref[:, cs] = functools.reduce(add, [(wg / tot) * og for wg, og in zip(ws, outs)])


def _attn_sample(proj, cache_k, cache_v, rel_bias, t, *, hg):
    bt = proj.shape[0]
    b, p, n_heads, _ = cache_k.shape
    w = hg * HEAD_DIM
    nhg = n_heads // hg
    assert t <= QUERY_PAD and p % QUERY_PAD == 0 and bt == b * t
    n_keys = p + QUERY_PAD
    key = jnp.arange(n_keys)
    dist = jnp.where(key[None, :] < p + t, p + jnp.arange(t)[:, None] - key[None, :], -1)
    bias = _pattern_bias(rel_bias, dist)
    n_pat = bias.shape[0]
    return pl.pallas_call(
        functools.partial(_attn_sample_kernel, hg=hg, t=t, nhg=nhg),
        grid=(b, nhg),
        in_specs=[
            pl.BlockSpec((t, w), lambda bb, g: (bb, g)),
            pl.BlockSpec((t, w), lambda bb, g: (bb, nhg + g)),
            pl.BlockSpec((t, w), lambda bb, g: (bb, 2 * nhg + g)),
            pl.BlockSpec((n_pat, hg, t, n_keys), lambda bb, g: (0, g, 0, 0)),
            pl.BlockSpec(memory_space=pl.ANY),
            pl.BlockSpec(memory_space=pl.ANY),
        ],
        out_specs=pl.BlockSpec((t, w), lambda bb, g: (bb, g)),
        out_shape=jax.ShapeDtypeStruct((bt, n_heads * HEAD_DIM), F32),
        scratch_shapes=[pltpu.VMEM((2, p, w), F32), pltpu.VMEM((2, p, w), F32),
                        pltpu.VMEM((n_keys, w), BF16), pltpu.VMEM((n_keys, w), BF16),
                        pltpu.SemaphoreType.DMA((2, 2))],
        compiler_params=_params("arbitrary", "arbitrary"),
        name="attn_sample",
    )(proj, proj, proj, bias, cache_k, cache_v)


def _round_bf16(x):
    return x.astype(BF16).astype(F32)


CONV_ALIGN = 8


def _conv_kernel(a_ref, g_ref, st_ref, w_ref, cb_ref, ng_ref, nb_ref, o_ref, tail_ref, ext_ref, y_ref,
                 extr_ref, sh_ref, *, t, rc, cc):
    ti = pl.program_id(1)
    ch = a_ref.shape[-1]
    n_ext = CONV_HALO + t

    @pl.when(ti == 0)
    def _():
        ext_ref[0:CONV_HALO, :] = st_ref[...]
        extr_ref[n_ext:n_ext + CONV_ALIGN, :] = jnp.zeros((CONV_ALIGN, ch), F32)

    ext_ref[CONV_HALO:n_ext, :] = a_ref[...] * _sigmoid(g_ref[...])
    extr_ref[0:n_ext, :] = _round_bf16(ext_ref[...])
    for s in range(CONV_ALIGN):
        sh_ref[s] = extr_ref[s:s + n_ext, :]
    first = CONV_HALO - (CONV_WIDTH - 1)
    for r0 in range(0, t, rc):
        for c0 in range(0, ch, cc):
            acc = jnp.zeros((rc, cc), F32)
            for j in range(CONV_WIDTH):
                s, base = (first + j) % CONV_ALIGN, (first + j) // CONV_ALIGN * CONV_ALIGN
                acc = acc + w_ref[j:j + 1, c0:c0 + cc] * sh_ref[s, base + r0:base + r0 + rc, c0:c0 + cc]
            y_ref[r0:r0 + rc, c0:c0 + cc] = acc + cb_ref[:, c0:c0 + cc]
    for r0 in range(0, t, rc):
        y = _layer_norm_rows(y_ref[r0:r0 + rc, :], ng_ref[...], nb_ref[...])
        o_ref[r0:r0 + rc, :] = _silu(y).astype(o_ref.dtype)

    @pl.when(ti == pl.num_programs(1) - 1)
    def _():
        tail_ref[...] = ext_ref[t + first:n_ext, :]

    ext_ref[0:CONV_HALO, :] = ext_ref[t:n_ext, :]


def _conv_module(proj, a_blk, state, conv_w, conv_b, ng, nb, *, ch, t, out_dtype, name):
    b, length, _ = proj.shape
    keep = CONV_WIDTH - 1
    st = jnp.pad(state, ((0, 0), (CONV_HALO - keep, 0), (0, 0)))
    rc = min(t, 32)
    vec_spec = pl.BlockSpec((1, ch), lambda bb, i: (0, 0))
    return pl.pallas_call(
        functools.partial(_conv_kernel, t=t, rc=rc, cc=512),
        grid=(b, length // t),
        in_specs=[
            pl.BlockSpec((None, t, ch), lambda bb, i: (bb, i, a_blk)),
            pl.BlockSpec((None, t, ch), lambda bb, i: (bb, i, a_blk + 1)),
            pl.BlockSpec((None, CONV_HALO, ch), lambda bb, i: (bb, 0, 0)),
            pl.BlockSpec((CONV_WIDTH, ch), lambda bb, i: (0, 0)),
            vec_spec, vec_spec, vec_spec,
        ],
        out_specs=[
            pl.BlockSpec((None, t, ch), lambda bb, i: (bb, i, 0)),
            pl.BlockSpec((None, keep, ch), lambda bb, i: (bb, 0, 0)),
        ],
        out_shape=[jax.ShapeDtypeStruct((b, length, ch), out_dtype),
                   jax.ShapeDtypeStruct((b, keep, ch), F32)],
        scratch_shapes=[pltpu.VMEM((CONV_HALO + t, ch), F32), pltpu.VMEM((t, ch), F32),
                        pltpu.VMEM((CONV_HALO + t + CONV_ALIGN, ch), F32),
                        pltpu.VMEM((CONV_ALIGN, CONV_HALO + t, ch), F32)],
        compiler_params=_params("arbitrary", "arbitrary"),
        name=name,
    )(proj, proj, st, conv_w, conv_b.reshape(1, ch), ng.reshape(1, ch), nb.reshape(1, ch))


def _split3(x):
    a1 = x.astype(BF16)
    r1 = x - a1.astype(F32)
    a2 = r1.astype(BF16)
    a3 = (r1 - a2.astype(F32)).astype(BF16)
    return a1, a2, a3


def _hgrn_gates(q, f, lb):
    fg = lb + (1.0 - lb) * _sigmoid(f)
    return _silu(q), 1.0 - fg, jnp.log(fg)


def _hgrn_finish(o, g, ng, dtype):
    normed = o * lax.rsqrt(jnp.mean(o * o, -1, keepdims=True) + EPS) * ng
    return (normed * _silu(g)).astype(dtype)


def _hgrn_chunk_kernel(q_ref, f_ref, i_ref, g_ref, lb_ref, ng_ref, s0_ref, o_ref, sfin_ref, st_ref, *, hg, t, c):
    ti = pl.program_id(2)
    dk = HEAD_DIM

    @pl.when(ti == 0)
    def _():
        for h in range(hg):
            st_ref[h] = s0_ref[h].T

    lb = lb_ref[...]
    ng = ng_ref[...]
    causal = lax.broadcasted_iota(jnp.int32, (c, c), 0) >= lax.broadcasted_iota(jnp.int32, (c, c), 1)
    tri = causal.astype(BF16)
    heads = [slice(h * dk, (h + 1) * dk) for h in range(hg)]
    chunks = [slice(c0, c0 + c) for c0 in range(0, t, c)]
    q_in, decs, intra, incr = [], [], [], []
    for rows in chunks:
        qq, kk, logf = _hgrn_gates(q_ref[rows, :], f_ref[rows, :], lb)
        l1, l2, l3 = _split3(logf)
        cum = _dot(tri, l1) + _dot(tri, l2) + _dot(tri, l3)
        mid = cum[c // 2 - 1:c // 2, :]
        last = cum[c - 1:c, :]
        q_md = (qq * jnp.exp(cum - mid)).astype(BF16)
        k_md = (kk * jnp.exp(mid - cum)).astype(BF16)
        k_end = (kk * jnp.exp(last - cum)).astype(BF16)
        q_in.append((qq * jnp.exp(cum)).astype(BF16))
        decs.append(jnp.exp(last))
        vs = [i_ref[rows, cs].astype(BF16) for cs in heads]
        scores = [jnp.where(causal, _dot_nt(q_md[:, cs], k_md[:, cs]), 0.0).astype(BF16) for cs in heads]
        intra.append([_dot(sc, v) for sc, v in zip(scores, vs)])
        incr.append([_dot_tn(v, k_end[:, cs]) for v, cs in zip(vs, heads)])
    states = [st_ref[h] for h in range(hg)]
    for n, rows in enumerate(chunks):
        for h, cs in enumerate(heads):
            o = intra[n][h] + _dot_nt(q_in[n][:, cs], states[h].astype(BF16))
            states[h] = states[h] * decs[n][:, cs] + incr[n][h]
            o_ref[rows, cs] = _hgrn_finish(o, g_ref[rows, cs], ng, o_ref.dtype)
    for h in range(hg):
        st_ref[h] = states[h]

    @pl.when(ti == pl.num_programs(2) - 1)
    def _():
        for h in range(hg):
            sfin_ref[h] = st_ref[h].T


def _hgrn_step_kernel(q_ref, f_ref, i_ref, g_ref, lb_ref, ng_ref, s0_ref, o_ref, sfin_ref, *, hg, c):
    dk = HEAD_DIM
    hi = lax.Precision.HIGHEST
    ng = ng_ref[...]
    step = lax.broadcasted_iota(jnp.int32, (c, 1), 0)
    qq, kk, logf = _hgrn_gates(q_ref[...], f_ref[...], lb_ref[...])
    acc = [logf[0:1, :]]
    for r in range(1, c):
        acc.append(acc[-1] + logf[r:r + 1, :])
    cum = jnp.concatenate(acc, 0)
    last = cum[c - 1:c, :]
    q_in = qq * jnp.exp(cum)
    k_end = kk * jnp.exp(last - cum)
    dec = jnp.exp(last)
    for h in range(hg):
        cs = slice(h * dk, (h + 1) * dk)
        st = s0_ref[h].T
        v = _round_bf16(i_ref[:, cs])
        o_rows = []
        for r in range(c):
            pair = qq[r:r + 1, cs] * jnp.exp(cum[r:r + 1, cs] - cum[:, cs]) * kk[:, cs]
            sc = jnp.where(step <= r, jnp.sum(pair, -1, keepdims=True), 0.0)
            o_rows.append(jnp.sum(_round_bf16(sc) * v, 0, keepdims=True))
        o = _dot_nt(_round_bf16(q_in[:, cs]), _round_bf16(st), hi) + jnp.concatenate(o_rows, 0)
        sfin_ref[h] = (st * dec[:, cs] + _dot_tn(v, _round_bf16(k_end[:, cs]), hi)).T
        o_ref[:, cs] = _hgrn_finish(o, g_ref[:, cs], ng, o_ref.dtype)


def _hgrn(proj, s0, lb, ng, n_heads, *, hg, t, c, out_dtype, name):
    b, length, _ = proj.shape
    w = hg * HEAD_DIM
    nhg = n_heads // hg
    col = lambda k: pl.BlockSpec((None, t, w), lambda bb, g, i: (bb, i, k * nhg + g))
    st_spec = pl.BlockSpec((None, hg, HEAD_DIM, HEAD_DIM), lambda bb, g, i: (bb, g, 0, 0))
    if t == length == c:
        body, scratch = functools.partial(_hgrn_step_kernel, hg=hg, c=c), []
    else:
        body = functools.partial(_hgrn_chunk_kernel, hg=hg, t=t, c=c)
        scratch = [pltpu.VMEM((hg, HEAD_DIM, HEAD_DIM), F32)]
    return pl.pallas_call(
        body,
        grid=(b, nhg, length // t),
        in_specs=[col(0), col(1), col(2), col(3),
                  pl.BlockSpec((1, w), lambda bb, g, i: (0, g)),
                  pl.BlockSpec((1, HEAD_DIM), lambda bb, g, i: (0, 0)),
                  st_spec],
        out_specs=[pl.BlockSpec((None, t, w), lambda bb, g, i: (bb, i, g)), st_spec],
        out_shape=[jax.ShapeDtypeStruct((b, length, n_heads * HEAD_DIM), out_dtype),
                   jax.ShapeDtypeStruct(s0.shape, F32)],
        scratch_shapes=scratch,
        compiler_params=_params("arbitrary", "arbitrary", "arbitrary"),
        name=name,
    )(proj, proj, proj, proj, lb.reshape(1, -1), ng.reshape(1, HEAD_DIM), s0)


def _top2(vals):
    n = len(vals)
    m1 = functools.reduce(jnp.maximum, vals)
    i1 = jnp.full(m1.shape, n - 1, jnp.int32)
    for k in range(n - 2, -1, -1):
        i1 = jnp.where(vals[k] == m1, k, i1)
    rest = [jnp.where(i1 == k, -1.0, vals[k]) for k in range(n)]
    m2 = functools.reduce(jnp.maximum, rest)
    i2 = jnp.full(m1.shape, n - 1, jnp.int32)
    for k in range(n - 2, -1, -1):
        i2 = jnp.where(rest[k] == m2, k, i2)
    return m1, i1, m2, i2


def _router_kernel(x_ref, w_ref, b_ref, ex_ref, gt_ref):
    logits = _dot_nt(w_ref[...].astype(BF16), x_ref[...]) + b_ref[...]
    e = jnp.exp(logits - jnp.max(logits, 0, keepdims=True))
    p = e / jnp.sum(e, 0, keepdims=True)
    members = [p[k * N_GROUPS:(k + 1) * N_GROUPS, :] for k in range(GROUP_SIZE)]
    m1, i1, m2, i2 = _top2(members)
    score = m1 + m2
    gid = lax.broadcasted_iota(jnp.int32, score.shape, 0)
    best = jnp.max(score, 0, keepdims=True)
    gsel = jnp.min(jnp.where(score == best, gid, N_GROUPS), 0, keepdims=True)
    sel = gid == gsel
    pick_f = lambda a: jnp.sum(jnp.where(sel, a, 0.0), 0, keepdims=True)
    pick_i = lambda a: jnp.sum(jnp.where(sel, a, 0), 0, keepdims=True)
    p1, p2 = pick_f(m1), pick_f(m2)
    ex_ref[0:1, :] = gsel * GROUP_SIZE + pick_i(i1)
    ex_ref[1:2, :] = gsel * GROUP_SIZE + pick_i(i2)
    gt_ref[0:1, :] = p1 / (p1 + p2)
    gt_ref[1:2, :] = p2 / (p1 + p2)


def _router(x, w_t, b_col, *, tm, name):
    n, d = x.shape
    out_spec = pl.BlockSpec((TOP_K, tm), lambda i: (0, i))
    return pl.pallas_call(
        _router_kernel,
        grid=(pl.cdiv(n, tm),),
        in_specs=[pl.BlockSpec((tm, d), lambda i: (i, 0)),
                  pl.BlockSpec((N_EXPERTS, d), lambda i: (0, 0)),
                  pl.BlockSpec((N_EXPERTS, 1), lambda i: (0, 0))],
        out_specs=[out_spec, out_spec],
        out_shape=[jax.ShapeDtypeStruct((TOP_K, n), jnp.int32), jax.ShapeDtypeStruct((TOP_K, n), F32)],
        compiler_params=_params("arbitrary"),
        name=name,
    )(x, w_t, b_col)


MOE_SLOT_ROWS = 576


def _slot_index(s, j, nu_ref, nj):
    used = s < nu_ref[0]
    return jnp.minimum(s, nu_ref[0] - 1), jnp.where(used, j, nj - 1)


def _gate_up_kernel(se_ref, ss_ref, nu_ref, tok_ref, x_hbm, wg_ref, wu_ref, h_ref, xg_ref, xb_ref, sem,
                    *, n_slots, nj):
    s = pl.program_id(0)
    j = pl.program_id(1)
    n_used = nu_ref[0]
    rows = xb_ref.shape[0]
    part = rows // nj

    def start_rows(slot, first, count):
        buf = slot % 2
        base = ss_ref[slot] + first
        for r in range(count):
            tok = tok_ref[base + r]
            pltpu.make_async_copy(x_hbm.at[pl.ds(tok, 1), :], xg_ref.at[buf, pl.ds(first + r, 1), :],
                                  sem.at[buf]).start()

    def wait_buffer(slot):
        buf = slot % 2
        pltpu.make_async_copy(x_hbm.at[pl.ds(0, rows), :], xg_ref.at[buf], sem.at[buf]).wait()

    @pl.when(jnp.logical_and(s == 0, j == 0))
    def _():
        start_rows(0, 0, rows)

    @pl.when(jnp.logical_and(s <= n_used, j == 0))
    def _():
        wait_buffer(s)

    @pl.when(jnp.logical_and(s < n_used, j == 0))
    def _():
        xb_ref[...] = xg_ref[s % 2].astype(BF16)

    @pl.when(s < n_used)
    def _():
        start_rows(s + 1, j * part, part)
        x = xb_ref[...]
        gate = _dot(x, wg_ref[...].astype(BF16))
        up = _dot(x, wu_ref[...].astype(BF16))
        h_ref[...] = (_silu(gate) * up).astype(h_ref.dtype)

    @pl.when(jnp.logical_and(jnp.logical_and(s == n_slots - 1, j == nj - 1), n_used == n_slots))
    def _():
        wait_buffer(n_slots)


def _down_kernel(se_ref, nu_ref, h_ref, wd_ref, y_ref):
    @pl.when(pl.program_id(0) < nu_ref[0])
    def _():
        y_ref[...] = _dot(h_ref[...], wd_ref[...].astype(BF16))


def _expert_ffn(x_rows, slot_expert, slot_start, n_used, tok_sorted, w_gate_up, w_down, layer, *, th, tn):
    n, d = x_rows.shape
    d_exp = w_down.shape[2]
    n_slots = slot_expert.shape[0] - 1
    rows = n_slots * MOE_SLOT_ROWS
    nj = d_exp // th

    def w_map(col0):
        def index(s, j, se, ss, nu, tok):
            se_, je_ = _slot_index(s, j, nu, nj)
            return (layer, se[se_], 0, col0 + je_)
        return index

    def h_map(s, j, se, ss, nu, tok):
        return _slot_index(s, j, nu, nj)

    h = pl.pallas_call(
        functools.partial(_gate_up_kernel, n_slots=n_slots, nj=nj),
        grid_spec=pltpu.PrefetchScalarGridSpec(
            num_scalar_prefetch=4,
            grid=(n_slots, nj),
            in_specs=[
                pl.BlockSpec(memory_space=pl.ANY),
                pl.BlockSpec((None, None, d, th), w_map(0)),
                pl.BlockSpec((None, None, d, th), w_map(nj)),
            ],
            out_specs=pl.BlockSpec((MOE_SLOT_ROWS, th), h_map),
            scratch_shapes=[pltpu.VMEM((2, MOE_SLOT_ROWS, d), F32), pltpu.VMEM((MOE_SLOT_ROWS, d), BF16),
                            pltpu.SemaphoreType.DMA((2,))],
        ),
        out_shape=jax.ShapeDtypeStruct((rows, d_exp), BF16),
        compiler_params=_params("arbitrary", "arbitrary"),
        name=f"moe_gate_up_{layer}",
    )(slot_expert, slot_start, n_used, tok_sorted, x_rows, w_gate_up, w_gate_up)

    nj2 = d // tn

    def wd_map(s, j, se, nu):
        se_, je_ = _slot_index(s, j, nu, nj2)
        return (layer, se[se_], 0, je_)

    return pl.pallas_call(
        _down_kernel,
        grid_spec=pltpu.PrefetchScalarGridSpec(
            num_scalar_prefetch=2,
            grid=(n_slots, nj2),
            in_specs=[
                pl.BlockSpec((MOE_SLOT_ROWS, d_exp), lambda s, j, se, nu: (_slot_index(s, j, nu, nj2)[0], 0)),
                pl.BlockSpec((None, None, d_exp, tn), wd_map),
            ],
            out_specs=pl.BlockSpec((MOE_SLOT_ROWS, tn), lambda s, j, se, nu: _slot_index(s, j, nu, nj2)),
        ),
        out_shape=jax.ShapeDtypeStruct((rows, d), F32),
        compiler_params=_params("arbitrary", "arbitrary"),
        name=f"moe_down_{layer}",
    )(slot_expert, n_used, h, w_down)


def _moe_layer(x_rows, experts, gates, w_gate_up, w_down, layer, ln_g, ln_b, *, tm_ln, split_at=None):
    n, d = x_rows.shape
    n_assign = n * TOP_K
    r = MOE_SLOT_ROWS
    flat_e = experts.reshape(-1)
    order = jnp.argsort(flat_e).astype(jnp.int32)
    e_sorted = flat_e[order]
    tok_sorted = order // TOP_K
    counts = jnp.bincount(flat_e, length=N_EXPERTS).astype(jnp.int32)
    starts = jnp.cumsum(counts) - counts
    slots_per = (counts + r - 1) // r
    slot_end = jnp.cumsum(slots_per)
    slot_first = slot_end - slots_per
    n_slots = -(-n_assign // r) + N_EXPERTS
    sid = jnp.arange(n_slots + 1, dtype=jnp.int32)
    used = sid < slot_end[-1]
    slot_expert = jnp.minimum(jnp.searchsorted(slot_end, sid, side='right'), N_EXPERTS - 1).astype(jnp.int32)
    local = sid - slot_first[slot_expert]
    slot_start = jnp.where(used, starts[slot_expert] + local * r, 0).astype(jnp.int32)
    n_used = slot_end[-1].astype(jnp.int32).reshape(1)
    tok_sorted = jnp.pad(tok_sorted, (0, r))
    rank = jnp.arange(n_assign, dtype=jnp.int32) - starts[e_sorted]
    dest_sorted = (slot_first[e_sorted] + rank // r) * r + rank % r
    dest = dest_sorted[jnp.argsort(order)].reshape(n, TOP_K)
    ys = _expert_ffn(x_rows, slot_expert, slot_start, n_used, tok_sorted, w_gate_up, w_down, layer, th=256, tn=1024)
    return _ln_moe(x_rows, ys, dest, gates, ln_g, ln_b, tm=tm_ln, name=f"ln_moe_{layer}", split_at=split_at)


def kernel(x_prompt, x_sample, cache_win_k, cache_win_v, state_conv, state_hgrn, w_in_ab, rel_bias, conv_w,
           conv_b, conv_norm_g, conv_norm_b, w_out_ab, w_in_c, hgrn_lb, hgrn_norm_g, w_out_c, ln_g, ln_b,
           router_w, router_b, w_gate_up, w_down):
    bp, sp, d = x_prompt.shape
    bs, ts, _ = x_sample.shape
    n_p, n_s = bp * sp, bs * ts
    attn_w = d // 2
    n_ah = attn_w // HEAD_DIM
    conv_ch = d - attn_w
    n_hh = d // HEAD_DIM
    keep = CONV_WIDTH - 1

    lb_soft = jax.nn.softmax(hgrn_lb.astype(F32), axis=0)
    lower_bounds = jnp.cumsum(lb_soft, axis=0) - lb_soft[0]
    perm = (jnp.arange(N_EXPERTS) % N_GROUPS) * GROUP_SIZE + jnp.arange(N_EXPERTS) // N_GROUPS
    router_wt = router_w.T[perm]
    router_bc = router_b.astype(F32)[perm].reshape(N_EXPERTS, 1)

    x_main, x_extra = x_prompt.reshape(n_p, d), x_sample.reshape(n_s, d)
    xb_main, xb_extra, xb_row0 = x_main.astype(BF16), x_extra.astype(BF16), 0
    outs = {}
    for l in range(DEPTH):
        if l % 2 == 0:
            a = l // 2
            n_in = w_in_ab.shape[2]
            proj_p, proj_s = _matmul([xb_main], [xb_extra], w_in_ab, a, n_in, tm=512, tn=512, rows=n_p,
                                     extra_rows=n_s, extra_row0=xb_row0, name=f"ab_in{l}")
            pp3 = proj_p.reshape(bp, sp, n_in)
            attn_p = _attn_prompt(pp3, rel_bias, n_ah, tq=256)
            conv_p, tail_p = _conv_module(pp3, 3 * attn_w // conv_ch, jnp.zeros((bp, keep, conv_ch), F32),
                                          conv_w[a], conv_b[a], conv_norm_g[a], conv_norm_b[a],
                                          ch=conv_ch, t=128, out_dtype=BF16, name=f"conv_p{l}")
            attn_s = _attn_sample(proj_s, cache_win_k[a], cache_win_v[a], rel_bias, ts, hg=4)
            conv_s, tail_s = _conv_module(proj_s.reshape(bs, ts, n_in), 3 * attn_w // conv_ch, state_conv[a],
                                          conv_w[a], conv_b[a], conv_norm_g[a], conv_norm_b[a],
                                          ch=conv_ch, t=ts, out_dtype=F32, name=f"conv_s{l}")
            acts_p = [attn_p.reshape(n_p, attn_w), conv_p.reshape(n_p, conv_ch)]
            acts_s = [attn_s.astype(BF16), conv_s.reshape(n_s, conv_ch).astype(BF16)]
            w_out, lw = w_out_ab, a
            outs.setdefault("wk_p", []).append(proj_p[:, attn_w:2 * attn_w].reshape(bp, sp, n_ah, HEAD_DIM))
            outs.setdefault("wv_p", []).append(proj_p[:, 2 * attn_w:3 * attn_w].reshape(bp, sp, n_ah, HEAD_DIM))
            outs.setdefault("cv_p", []).append(tail_p)
            outs.setdefault("wk_s", []).append(proj_s[:, attn_w:2 * attn_w].reshape(bs, ts, n_ah, HEAD_DIM))
            outs.setdefault("wv_s", []).append(proj_s[:, 2 * attn_w:3 * attn_w].reshape(bs, ts, n_ah, HEAD_DIM))
            outs.setdefault("cv_s", []).append(tail_s)
        else:
            c = l // 2
            n_in = w_in_c.shape[2]
            proj_p, proj_s = _matmul([xb_main], [xb_extra], w_in_c, c, n_in, tm=512, tn=512, rows=n_p,
                                     extra_rows=n_s, extra_row0=xb_row0, name=f"c_in{l}")
            o_p, h_p = _hgrn(proj_p.reshape(bp, sp, n_in), jnp.zeros((bp, n_hh, HEAD_DIM, HEAD_DIM), F32),
                             lower_bounds[l], hgrn_norm_g[c], n_hh, hg=4, t=256, c=64,
                             out_dtype=BF16, name=f"hgrn_p{l}")
            o_s, h_s = _hgrn(proj_s.reshape(bs, ts, n_in), state_hgrn[c], lower_bounds[l], hgrn_norm_g[c],
                             n_hh, hg=4, t=ts, c=ts, out_dtype=F32, name=f"hgrn_s{l}")
            acts_p = [o_p.reshape(n_p, d)]
            acts_s = [o_s.reshape(n_s, d).astype(BF16)]
            w_out, lw = w_out_c, c
            outs.setdefault("hs_p", []).append(h_p)
            outs.setdefault("hs_s", []).append(h_s)
        m_p, m_s = _matmul(acts_p, acts_s, w_out, lw, d, tm=512, tn=512, rows=n_p, extra_rows=n_s, name=f"mix_out{l}")
        x, xb = _ln_res(x_main, x_extra, m_p, m_s, ln_g[l, 0], ln_b[l, 0], tm=256, name=f"ln_mix{l}")
        experts, gates = _router(xb, router_wt, router_bc, tm=512, name=f"router{l}")
        x, xb = _moe_layer(x, experts.T, gates.T, w_gate_up, w_down, l, ln_g[l, 1], ln_b[l, 1], tm_ln=128,
                           split_at=n_p if l == DEPTH - 1 else None)
        x_main, x_extra, xb_main, xb_extra, xb_row0 = x, x, xb, xb, n_p
    y_p, y_s = x, xb
    stack = lambda k: jnp.stack(outs[k])
    return (y_p.reshape(bp, sp, d), y_s.reshape(bs, ts, d), stack("wk_p"), stack("wv_p"), stack("cv_p"),
            stack("hs_p"), stack("wk_s"), stack("wv_s"), stack("cv_s"), stack("hs_s"))
```

```python
import functools
import math

import jax
import jax.numpy as jnp
from jax import lax
from jax.experimental import pallas as pl
from jax.experimental.pallas import tpu as pltpu

F32 = jnp.float32
BF16 = jnp.bfloat16

DEPTH = 2
HEAD_DIM = 128
DILATED_PATTERNS = ((128, 1), (512, 4), (2048, 16))
NUM_BUCKETS = 32
MAX_EXACT = 16
MAX_DISTANCE = 2048
CONV_WIDTH = 31
N_EXPERTS = 32
N_GROUPS = 8
GROUP_SIZE = N_EXPERTS // N_GROUPS
TOP_K = 2
ALPHA = (2.0 * DEPTH) ** 0.25
EPS = 1e-5
NEG = -1e30
VMEM_LIMIT = 56 * 1024 * 1024
CONV_HALO = 32


def _params(*sem):
    return pltpu.CompilerParams(dimension_semantics=sem, vmem_limit_bytes=VMEM_LIMIT)


def _sigmoid(x):
    return 1.0 / (1.0 + jnp.exp(-x))


def _silu(x):
    return x * _sigmoid(x)


def _dot_nt(a, b, precision=None):
    return lax.dot_general(a, b, (((1,), (1,)), ((), ())), preferred_element_type=F32, precision=precision)


def _dot_tn(a, b, precision=None):
    return lax.dot_general(a, b, (((0,), (0,)), ((), ())), preferred_element_type=F32, precision=precision)


def _dot(a, b, precision=None):
    return jnp.dot(a, b, preferred_element_type=F32, precision=precision)


def _mm_kernel(*refs, n_x, n_main):
    xm_refs, xe_refs = refs[:n_x], refs[n_x:2 * n_x]
    w_ref, om_ref, oe_ref, wb_ref = refs[2 * n_x:]
    m = pl.program_id(1)

    @pl.when(m == 0)
    def _():
        wb_ref[...] = w_ref[...].astype(BF16)

    def product(x_refs, o_ref):
        acc = None
        off = 0
        for xr in x_refs:
            k = xr.shape[-1]
            part = _dot(xr[...], wb_ref[off:off + k, :])
            acc = part if acc is None else acc + part
            off += k
        o_ref[...] = acc.astype(o_ref.dtype)

    @pl.when(m < n_main)
    def _():
        product(xm_refs, om_ref)

    @pl.when(m == n_main)
    def _():
        product(xe_refs, oe_ref)


def _matmul(xs, xs_extra, w, layer, n_cols, *, tm, tn, name, rows, extra_rows, extra_row0=0):
    k_tot = sum(x.shape[1] for x in xs)
    assert w.shape[1] == k_tot and rows % tm == 0 and n_cols % tn == 0 and extra_row0 % extra_rows == 0
    assert [x.shape[1] for x in xs] == [x.shape[1] for x in xs_extra]
    n_main = rows // tm
    eoff = extra_row0 // extra_rows
    main_row = lambda m: jnp.minimum(m, n_main - 1)
    in_specs = [pl.BlockSpec((tm, x.shape[1]), lambda n, m: (main_row(m), 0)) for x in xs]
    in_specs += [pl.BlockSpec((extra_rows, x.shape[1]), lambda n, m: (eoff, 0)) for x in xs_extra]
    in_specs.append(pl.BlockSpec((None, k_tot, tn), lambda n, m: (layer, 0, n)))
    return pl.pallas_call(
        functools.partial(_mm_kernel, n_x=len(xs), n_main=n_main),
        grid=(n_cols // tn, n_main + 1),
        in_specs=in_specs,
        out_specs=[pl.BlockSpec((tm, tn), lambda n, m: (main_row(m), n)),
                   pl.BlockSpec((extra_rows, tn), lambda n, m: (0, n))],
        out_shape=[jax.ShapeDtypeStruct((rows, n_cols), F32), jax.ShapeDtypeStruct((extra_rows, n_cols), F32)],
        scratch_shapes=[pltpu.VMEM((k_tot, tn), BF16)],
        compiler_params=_params("arbitrary", "arbitrary"),
        name=name,
    )(*xs, *xs_extra, w)


def _layer_norm_rows(z, g, b):
    mu = jnp.mean(z, -1, keepdims=True)
    zc = z - mu
    var = jnp.mean(zc * zc, -1, keepdims=True)
    return zc * lax.rsqrt(var + EPS) * g + b


LN_ROW_CHUNK = 32


def _ln_res_kernel(x_ref, xe_ref, m_ref, me_ref, g_ref, b_ref, of_ref, ob_ref):
    last = pl.program_id(0) == pl.num_programs(0) - 1
    n_extra = me_ref.shape[0]
    for r0 in range(0, x_ref.shape[0], LN_ROW_CHUNK):
        rows = slice(r0, r0 + LN_ROW_CHUNK)
        res, mix = x_ref[rows, :], m_ref[rows, :]
        if r0 < n_extra:
            res = jnp.where(last, xe_ref[rows, :], res)
            mix = jnp.where(last, me_ref[rows, :], mix)
        y = _layer_norm_rows(ALPHA * res + mix, g_ref[...], b_ref[...])
        of_ref[rows, :] = y
        ob_ref[rows, :] = y.astype(BF16)


def _ln_res(x_main, x_extra, m_main, m_extra, g, b, *, tm, name):
    d = x_main.shape[1]
    n_main, n_extra = m_main.shape[0], m_extra.shape[0]
    assert n_main % tm == 0 and x_extra.shape[0] % n_extra == 0 and n_extra % LN_ROW_CHUNK == 0 and n_extra <= tm
    rows = n_main + n_extra
    xe_blk = x_extra.shape[0] // n_extra - 1
    main_spec = pl.BlockSpec((tm, d), lambda i: (jnp.minimum(i, n_main // tm - 1), 0))
    row_spec = pl.BlockSpec((tm, d), lambda i: (i, 0))
    vec_spec = pl.BlockSpec((1, d), lambda i: (0, 0))
    return pl.pallas_call(
        _ln_res_kernel,
        grid=(n_main // tm + 1,),
        in_specs=[main_spec, pl.BlockSpec((n_extra, d), lambda i: (xe_blk, 0)),
                  main_spec, pl.BlockSpec((n_extra, d), lambda i: (0, 0)), vec_spec, vec_spec],
        out_specs=[row_spec, row_spec],
        out_shape=[jax.ShapeDtypeStruct((rows, d), F32), jax.ShapeDtypeStruct((rows, d), BF16)],
        compiler_params=_params("arbitrary"),
        name=name,
    )(x_main, x_extra, m_main, m_extra, g.reshape(1, d), b.reshape(1, d))


def _ln_moe_kernel(dest_ref, x_ref, gt_ref, g_ref, b_ref, ys_hbm, oa_ref, ob_ref, yg_ref, sem,
                   *, tm, n_steps, split):
    i = pl.program_id(0)

    def start_rows(step):
        buf = step % 2
        base = step * (tm * TOP_K)
        for r in range(tm):
            for k in range(TOP_K):
                row = dest_ref[base + r * TOP_K + k]
                pltpu.make_async_copy(ys_hbm.at[pl.ds(row, 1), :], yg_ref.at[buf, pl.ds(k * tm + r, 1), :],
                                      sem.at[buf]).start()

    def wait_buffer(step):
        buf = step % 2
        pltpu.make_async_copy(ys_hbm.at[pl.ds(0, TOP_K * tm), :], yg_ref.at[buf], sem.at[buf]).wait()

    @pl.when(i == 0)
    def _():
        start_rows(0)

    wait_buffer(i)
    start_rows(i + 1)
    buf = i % 2
    for r0 in range(0, tm, LN_ROW_CHUNK):
        rows = slice(r0, r0 + LN_ROW_CHUNK)
        gt = gt_ref[rows, :]
        ff = (gt[:, 0:1] * yg_ref[buf, r0:r0 + LN_ROW_CHUNK, :]
              + gt[:, 1:2] * yg_ref[buf, tm + r0:tm + r0 + LN_ROW_CHUNK, :])
        y = _layer_norm_rows(ALPHA * x_ref[rows, :] + ff, g_ref[...], b_ref[...])
        if not split:
            oa_ref[rows, :] = y
            ob_ref[rows, :] = y.astype(BF16)
        else:
            @pl.when(i < n_steps - 1)
            def _():
                oa_ref[rows, :] = y

            if r0 < ob_ref.shape[0]:
                @pl.when(i == n_steps - 1)
                def _():
                    ob_ref[rows, :] = y

    @pl.when(i == n_steps - 1)
    def _():
        wait_buffer(i + 1)


def _ln_moe(x, ys, dest, gates, g, b, *, tm, name, split_at=None):
    rows, d = x.shape
    n_steps = pl.cdiv(rows, tm)
    dest_flat = jnp.pad(dest.reshape(-1), (0, (n_steps + 1) * tm * TOP_K - rows * TOP_K))
    row_spec = pl.BlockSpec((tm, d), lambda i, dr: (i, 0))
    vec_spec = pl.BlockSpec((1, d), lambda i, dr: (0, 0))
    if split_at is None:
        out_specs = [row_spec, row_spec]
        out_shape = [jax.ShapeDtypeStruct((rows, d), F32), jax.ShapeDtypeStruct((rows, d), BF16)]
    else:
        n_tail = rows - split_at
        assert split_at == (n_steps - 1) * tm and n_tail % LN_ROW_CHUNK == 0
        out_specs = [pl.BlockSpec((tm, d), lambda i, dr: (jnp.minimum(i, n_steps - 2), 0)),
                     pl.BlockSpec((n_tail, d), lambda i, dr: (0, 0))]
        out_shape = [jax.ShapeDtypeStruct((split_at, d), F32), jax.ShapeDtypeStruct((n_tail, d), F32)]
    return pl.pallas_call(
        functools.partial(_ln_moe_kernel, tm=tm, n_steps=n_steps, split=split_at is not None),
        grid_spec=pltpu.PrefetchScalarGridSpec(
            num_scalar_prefetch=1,
            grid=(n_steps,),
            in_specs=[row_spec, pl.BlockSpec((tm, TOP_K), lambda i, dr: (i, 0)), vec_spec, vec_spec,
                      pl.BlockSpec(memory_space=pl.ANY)],
            out_specs=out_specs,
            scratch_shapes=[pltpu.VMEM((2, TOP_K * tm, d), F32), pltpu.SemaphoreType.DMA((2,))],
        ),
        out_shape=out_shape,
        compiler_params=_params("arbitrary"),
        name=name,
    )(dest_flat, x, gates, g.reshape(1, d), b.reshape(1, d), ys)


def _rel_bucket(dist):
    dist = dist.astype(jnp.int32)
    d = jnp.maximum(dist, 1).astype(F32)
    large = MAX_EXACT + (jnp.log(d / MAX_EXACT) / math.log(MAX_DISTANCE / MAX_EXACT)
                         * (NUM_BUCKETS - MAX_EXACT)).astype(jnp.int32)
    large = jnp.minimum(large, NUM_BUCKETS - 1)
    return jnp.where(dist < MAX_EXACT, dist, large)


def _pattern_bias(rel_bias, d):
    base = jnp.moveaxis(rel_bias[_rel_bucket(jnp.maximum(d, 0))].astype(F32), -1, 0)
    outs = []
    for w, r in DILATED_PATTERNS:
        ok = (d >= 0) & (d % r == 0) & (d <= w)
        outs.append(jnp.where(ok[None], base, NEG))
    return jnp.stack(outs)


def _merged_bias(rel_bias, d):
    count = sum(((d >= 0) & (d % r == 0) & (d <= w)).astype(F32) for w, r in DILATED_PATTERNS)
    base = jnp.moveaxis(rel_bias[_rel_bucket(jnp.maximum(d, 0))].astype(F32), -1, 0)
    return jnp.where((count > 0)[None], base + jnp.log(jnp.maximum(count, 1.0))[None], NEG)


def _toeplitz_rows(vec, nd, tq):
    span = 2 * tq - 1
    win = jnp.stack([vec[:, k * tq:k * tq + span] for k in range(nd)], 1)
    return jnp.pad(win[..., ::-1], ((0, 0), (0, 0), (0, 1)))


def _attn_prompt_kernel(q_ref, k_ref, v_ref, rev_ref, o_ref, kb_ref, vb_ref, bias_ref, *, tq):
    i = pl.program_id(1)
    nb = q_ref.shape[0]

    @pl.when(i == 0)
    def _():
        kb_ref[...] = k_ref[...].astype(BF16)
        vb_ref[...] = v_ref[...].astype(BF16)
        for delta in range(bias_ref.shape[0]):
            window = jnp.broadcast_to(rev_ref[delta:delta + 1, :], (tq, 2 * tq))
            bias_ref[delta] = pltpu.roll(window, tq + 1, 1, stride=1, stride_axis=0)[:, 0:tq]

    qs = [(q_ref[b] * (HEAD_DIM ** -0.5)).astype(BF16) for b in range(nb)]

    def body(j, carry):
        start = pl.multiple_of(j * tq, tq)
        bias = bias_ref[i - j]
        out = []
        for b in range(nb):
            m, l, acc = carry[b]
            s = _dot_nt(qs[b], kb_ref[b, pl.ds(start, tq), :]) + bias
            m_new = jnp.maximum(m, jnp.max(s, -1, keepdims=True))
            p = jnp.exp(s - m_new)
            a = jnp.exp(m - m_new)
            l = a * l + jnp.sum(p, -1, keepdims=True)
            acc = a * acc + _dot(p.astype(BF16), vb_ref[b, pl.ds(start, tq), :])
            out.append((m_new, l, acc))
        return tuple(out)

    one = (jnp.full((tq, 1), NEG, F32), jnp.zeros((tq, 1), F32), jnp.zeros((tq, HEAD_DIM), F32))
    res = lax.fori_loop(0, i + 1, body, (one,) * nb)
    for b in range(nb):
        _, l, acc = res[b]
        o_ref[b] = (acc / l).astype(o_ref.dtype)


def _attn_prompt(proj, rel_bias, n_heads, *, tq):
    b, s, _ = proj.shape
    nd = s // tq
    rev = _toeplitz_rows(_merged_bias(rel_bias, jnp.arange(-(tq - 1), s)), nd, tq)
    return pl.pallas_call(
        functools.partial(_attn_prompt_kernel, tq=tq),
        grid=(n_heads, nd),
        in_specs=[
            pl.BlockSpec((b, tq, HEAD_DIM), lambda h, i: (0, i, h)),
            pl.BlockSpec((b, s, HEAD_DIM), lambda h, i: (0, 0, n_heads + h)),
            pl.BlockSpec((b, s, HEAD_DIM), lambda h, i: (0, 0, 2 * n_heads + h)),
            pl.BlockSpec((None, nd, 2 * tq), lambda h, i: (h, 0, 0)),
        ],
        out_specs=pl.BlockSpec((b, tq, HEAD_DIM), lambda h, i: (0, i, h)),
        out_shape=jax.ShapeDtypeStruct((b, s, n_heads * HEAD_DIM), BF16),
        scratch_shapes=[pltpu.VMEM((b, s, HEAD_DIM), BF16), pltpu.VMEM((b, s, HEAD_DIM), BF16),
                        pltpu.VMEM((nd, tq, tq), F32)],
        compiler_params=_params("arbitrary", "arbitrary"),
        name="attn_prompt",
    )(proj, proj, proj, rev)


QUERY_PAD = 16


def _attn_sample_kernel(q_ref, kn_ref, vn_ref, bias_ref, kc_hbm, vc_hbm, o_ref, kc_ref, vc_ref, kx_ref, vx_ref,
                        sem, *, hg, t, nhg):
    step = pl.program_id(0) * nhg + pl.program_id(1)
    n_steps = pl.num_programs(0) * nhg
    p = kc_ref.shape[1]
    n_keys, w = kx_ref.shape

    def copies(s, slot):
        b_, g_ = s // nhg, s % nhg
        out = []
        for h in range(hg):
            lanes = pl.ds(h * HEAD_DIM, HEAD_DIM)
            out.append(pltpu.make_async_copy(kc_hbm.at[b_, :, g_ * hg + h, :], kc_ref.at[slot, :, lanes], sem.at[0, slot]))
            out.append(pltpu.make_async_copy(vc_hbm.at[b_, :, g_ * hg + h, :], vc_ref.at[slot, :, lanes], sem.at[1, slot]))
        return out

    @pl.when(step == 0)
    def _():
        for c in copies(0, 0):
            c.start()

    slot = step % 2

    @pl.when(step + 1 < n_steps)
    def _():
        for c in copies(step + 1, 1 - slot):
            c.start()

    for c in copies(step, slot):
        c.wait()
    zrow = jnp.zeros((n_keys - p - t, w), F32)
    kx_ref[0:p, :] = kc_ref[slot].astype(BF16)
    vx_ref[0:p, :] = vc_ref[slot].astype(BF16)
    kx_ref[p:n_keys, :] = jnp.concatenate([kn_ref[...], zrow], 0).astype(BF16)
    vx_ref[p:n_keys, :] = jnp.concatenate([vn_ref[...], zrow], 0).astype(BF16)
    n_pat = bias_ref.shape[0]
    add = lambda a, b: a + b
    for h in range(hg):
        cs = slice(h * HEAD_DIM, (h + 1) * HEAD_DIM)
        q16 = jnp.concatenate([q_ref[:, cs], jnp.zeros((QUERY_PAD - t, HEAD_DIM), F32)], 0).astype(BF16)
        s = _dot_nt(q16, kx_ref[:, cs])[0:t, :] * (HEAD_DIM ** -0.5)
        outs, lses = [], []
        for g in range(n_pat):
            lg = s + bias_ref[g, h]
            m = jnp.max(lg, -1, keepdims=True)
            e = jnp.exp(lg - m)
            den = jnp.sum(e, -1, keepdims=True)
            p16 = jnp.concatenate([e / den, jnp.zeros((QUERY_PAD - t, n_keys), F32)], 0).astype(BF16)
            outs.append(_dot(p16, vx_ref[:, cs])[0:t, :])
            lses.append(m + jnp.log(den))
        top = functools.reduce(jnp.maximum, lses)
        ws = [jnp.exp(l - top) for l in lses]
        tot = functools.reduce(add, ws)
        o_ref[:, cs] = functools.reduce(add, [(wg / tot) * og for wg, og in zip(ws, outs)])


def _attn_sample(proj, cache_k, cache_v, rel_bias, t, *, hg):
    bt = proj.shape[0]
    b, p, n_heads, _ = cache_k.shape
    w = hg * HEAD_DIM
    nhg = n_heads // hg
    assert t <= QUERY_PAD and p % QUERY_PAD == 0 and bt == b * t
    n_keys = p + QUERY_PAD
    key = jnp.arange(n_keys)
    dist = jnp.where(key[None, :] < p + t, p + jnp.arange(t)[:, None] - key[None, :], -1)
    bias = _pattern_bias(rel_bias, dist)
    n_pat = bias.shape[0]
    return pl.pallas_call(
        functools.partial(_attn_sample_kernel, hg=hg, t=t, nhg=nhg),
        grid=(b, nhg),
        in_specs=[
            pl.BlockSpec((t, w), lambda bb, g: (bb, g)),
            pl.BlockSpec((t, w), lambda bb, g: (bb, nhg + g)),
            pl.BlockSpec((t, w), lambda bb, g: (bb, 2 * nhg + g)),
            pl.BlockSpec((n_pat, hg, t, n_keys), lambda bb, g: (0, g, 0, 0)),
            pl.BlockSpec(memory_space=pl.ANY),
            pl.BlockSpec(memory_space=pl.ANY),
        ],
        out_specs=pl.BlockSpec((t, w), lambda bb, g: (bb, g)),
        out_shape=jax.ShapeDtypeStruct((bt, n_heads * HEAD_DIM), F32),
        scratch_shapes=[pltpu.VMEM((2, p, w), F32), pltpu.VMEM((2, p, w), F32),
                        pltpu.VMEM((n_keys, w), BF16), pltpu.VMEM((n_keys, w), BF16),
                        pltpu.SemaphoreType.DMA((2, 2))],
        compiler_params=_params("arbitrary", "arbitrary"),
        name="attn_sample",
    )(proj, proj, proj, bias, cache_k, cache_v)


def _round_bf16(x):
    return x.astype(BF16).astype(F32)


CONV_ALIGN = 8


def _conv_kernel(a_ref, g_ref, st_ref, w_ref, cb_ref, ng_ref, nb_ref, o_ref, tail_ref, ext_ref, y_ref,
                 extr_ref, sh_ref, *, t, rc, cc):
    ti = pl.program_id(1)
    ch = a_ref.shape[-1]
    n_ext = CONV_HALO + t

    @pl.when(ti == 0)
    def _():
        ext_ref[0:CONV_HALO, :] = st_ref[...]
        extr_ref[n_ext:n_ext + CONV_ALIGN, :] = jnp.zeros((CONV_ALIGN, ch), F32)

    ext_ref[CONV_HALO:n_ext, :] = a_ref[...] * _sigmoid(g_ref[...])
    extr_ref[0:n_ext, :] = _round_bf16(ext_ref[...])
    for s in range(CONV_ALIGN):
        sh_ref[s] = extr_ref[s:s + n_ext, :]
    first = CONV_HALO - (CONV_WIDTH - 1)
    for r0 in range(0, t, rc):
        for c0 in range(0, ch, cc):
            acc = jnp.zeros((rc, cc), F32)
            for j in range(CONV_WIDTH):
                s, base = (first + j) % CONV_ALIGN, (first + j) // CONV_ALIGN * CONV_ALIGN
                acc = acc + w_ref[j:j + 1, c0:c0 + cc] * sh_ref[s, base + r0:base + r0 + rc, c0:c0 + cc]
            y_ref[r0:r0 + rc, c0:c0 + cc] = acc + cb_ref[:, c0:c0 + cc]
    for r0 in range(0, t, rc):
        y = _layer_norm_rows(y_ref[r0:r0 + rc, :], ng_ref[...], nb_ref[...])
        o_ref[r0:r0 + rc, :] = _silu(y).astype(o_ref.dtype)

    @pl.when(ti == pl.num_programs(1) - 1)
    def _():
        tail_ref[...] = ext_ref[t + first:n_ext, :]

    ext_ref[0:CONV_HALO, :] = ext_ref[t:n_ext, :]


def _conv_module(proj, a_blk, state, conv_w, conv_b, ng, nb, *, ch, t, out_dtype, name):
    b, length, _ = proj.shape
    keep = CONV_WIDTH - 1
    st = jnp.pad(state, ((0, 0), (CONV_HALO - keep, 0), (0, 0)))
    rc = min(t, 32)
    vec_spec = pl.BlockSpec((1, ch), lambda bb, i: (0, 0))
    return pl.pallas_call(
        functools.partial(_conv_kernel, t=t, rc=rc, cc=512),
        grid=(b, length // t),
        in_specs=[
            pl.BlockSpec((None, t, ch), lambda bb, i: (bb, i, a_blk)),
            pl.BlockSpec((None, t, ch), lambda bb, i: (bb, i, a_blk + 1)),
            pl.BlockSpec((None, CONV_HALO, ch), lambda bb, i: (bb, 0, 0)),
            pl.BlockSpec((CONV_WIDTH, ch), lambda bb, i: (0, 0)),
            vec_spec, vec_spec, vec_spec,
        ],
        out_specs=[
            pl.BlockSpec((None, t, ch), lambda bb, i: (bb, i, 0)),
            pl.BlockSpec((None, keep, ch), lambda bb, i: (bb, 0, 0)),
        ],
        out_shape=[jax.ShapeDtypeStruct((b, length, ch), out_dtype),
                   jax.ShapeDtypeStruct((b, keep, ch), F32)],
        scratch_shapes=[pltpu.VMEM((CONV_HALO + t, ch), F32), pltpu.VMEM((t, ch), F32),
                        pltpu.VMEM((CONV_HALO + t + CONV_ALIGN, ch), F32),
                        pltpu.VMEM((CONV_ALIGN, CONV_HALO + t, ch), F32)],
        compiler_params=_params("arbitrary", "arbitrary"),
        name=name,
    )(proj, proj, st, conv_w, conv_b.reshape(1, ch), ng.reshape(1, ch), nb.reshape(1, ch))


def _split3(x):
    a1 = x.astype(BF16)
    r1 = x - a1.astype(F32)
    a2 = r1.astype(BF16)
    a3 = (r1 - a2.astype(F32)).astype(BF16)
    return a1, a2, a3


def _hgrn_gates(q, f, lb):
    fg = lb + (1.0 - lb) * _sigmoid(f)
    return _silu(q), 1.0 - fg, jnp.log(fg)


def _hgrn_finish(o, g, ng, dtype):
    normed = o * lax.rsqrt(jnp.mean(o * o, -1, keepdims=True) + EPS) * ng
    return (normed * _silu(g)).astype(dtype)


def _hgrn_chunk_kernel(q_ref, f_ref, i_ref, g_ref, lb_ref, ng_ref, s0_ref, o_ref, sfin_ref, st_ref, *, hg, t, c):
    ti = pl.program_id(2)
    dk = HEAD_DIM

    @pl.when(ti == 0)
    def _():
        for h in range(hg):
            st_ref[h] = s0_ref[h].T

    lb = lb_ref[...]
    ng = ng_ref[...]
    causal = lax.broadcasted_iota(jnp.int32, (c, c), 0) >= lax.broadcasted_iota(jnp.int32, (c, c), 1)
    tri = causal.astype(BF16)
    heads = [slice(h * dk, (h + 1) * dk) for h in range(hg)]
    chunks = [slice(c0, c0 + c) for c0 in range(0, t, c)]
    q_in, decs, intra, incr = [], [], [], []
    for rows in chunks:
        qq, kk, logf = _hgrn_gates(q_ref[rows, :], f_ref[rows, :], lb)
        l1, l2, l3 = _split3(logf)
        cum = _dot(tri, l1) + _dot(tri, l2) + _dot(tri, l3)
        mid = cum[c // 2 - 1:c // 2, :]
        last = cum[c - 1:c, :]
        q_md = (qq * jnp.exp(cum - mid)).astype(BF16)
        k_md = (kk * jnp.exp(mid - cum)).astype(BF16)
        k_end = (kk * jnp.exp(last - cum)).astype(BF16)
        q_in.append((qq * jnp.exp(cum)).astype(BF16))
        decs.append(jnp.exp(last))
        vs = [i_ref[rows, cs].astype(BF16) for cs in heads]
        scores = [jnp.where(causal, _dot_nt(q_md[:, cs], k_md[:, cs]), 0.0).astype(BF16) for cs in heads]
        intra.append([_dot(sc, v) for sc, v in zip(scores, vs)])
        incr.append([_dot_tn(v, k_end[:, cs]) for v, cs in zip(vs, heads)])
    states = [st_ref[h] for h in range(hg)]
    for n, rows in enumerate(chunks):
        for h, cs in enumerate(heads):
            o = intra[n][h] + _dot_nt(q_in[n][:, cs], states[h].astype(BF16))
            states[h] = states[h] * decs[n][:, cs] + incr[n][h]
            o_ref[rows, cs] = _hgrn_finish(o, g_ref[rows, cs], ng, o_ref.dtype)
    for h in range(hg):
        st_ref[h] = states[h]

    @pl.when(ti == pl.num_programs(2) - 1)
    def _():
        for h in range(hg):
            sfin_ref[h] = st_ref[h].T


def _hgrn_step_kernel(q_ref, f_ref, i_ref, g_ref, lb_ref, ng_ref, s0_ref, o_ref, sfin_ref, *, hg, c):
    dk = HEAD_DIM
    hi = lax.Precision.HIGHEST
    ng = ng_ref[...]
    step = lax.broadcasted_iota(jnp.int32, (c, 1), 0)
    qq, kk, logf = _hgrn_gates(q_ref[...], f_ref[...], lb_ref[...])
    acc = [logf[0:1, :]]
    for r in range(1, c):
        acc.append(acc[-1] + logf[r:r + 1, :])
    cum = jnp.concatenate(acc, 0)
    last = cum[c - 1:c, :]
    q_in = qq * jnp.exp(cum)
    k_end = kk * jnp.exp(last - cum)
    dec = jnp.exp(last)
    for h in range(hg):
        cs = slice(h * dk, (h + 1) * dk)
        st = s0_ref[h].T
        v = _round_bf16(i_ref[:, cs])
        o_rows = []
        for r in range(c):
            pair = qq[r:r + 1, cs] * jnp.exp(cum[r:r + 1, cs] - cum[:, cs]) * kk[:, cs]
            sc = jnp.where(step <= r, jnp.sum(pair, -1, keepdims=True), 0.0)
            o_rows.append(jnp.sum(_round_bf16(sc) * v, 0, keepdims=True))
        o = _dot_nt(_round_bf16(q_in[:, cs]), _round_bf16(st), hi) + jnp.concatenate(o_rows, 0)
        sfin_ref[h] = (st * dec[:, cs] + _dot_tn(v, _round_bf16(k_end[:, cs]), hi)).T
        o_ref[:, cs] = _hgrn_finish(o, g_ref[:, cs], ng, o_ref.dtype)


def _hgrn(proj, s0, lb, ng, n_heads, *, hg, t, c, out_dtype, name):
    b, length, _ = proj.shape
    w = hg * HEAD_DIM
    nhg = n_heads // hg
    col = lambda k: pl.BlockSpec((None, t, w), lambda bb, g, i: (bb, i, k * nhg + g))
    st_spec = pl.BlockSpec((None, hg, HEAD_DIM, HEAD_DIM), lambda bb, g, i: (bb, g, 0, 0))
    if t == length == c:
        body, scratch = functools.partial(_hgrn_step_kernel, hg=hg, c=c), []
    else:
        body = functools.partial(_hgrn_chunk_kernel, hg=hg, t=t, c=c)
        scratch = [pltpu.VMEM((hg, HEAD_DIM, HEAD_DIM), F32)]
    return pl.pallas_call(
        body,
        grid=(b, nhg, length // t),
        in_specs=[col(0), col(1), col(2), col(3),
                  pl.BlockSpec((1, w), lambda bb, g, i: (0, g)),
                  pl.BlockSpec((1, HEAD_DIM), lambda bb, g, i: (0, 0)),
                  st_spec],
        out_specs=[pl.BlockSpec((None, t, w), lambda bb, g, i: (bb, i, g)), st_spec],
        out_shape=[jax.ShapeDtypeStruct((b, length, n_heads * HEAD_DIM), out_dtype),
                   jax.ShapeDtypeStruct(s0.shape, F32)],
        scratch_shapes=scratch,
        compiler_params=_params("arbitrary", "arbitrary", "arbitrary"),
        name=name,
    )(proj, proj, proj, proj, lb.reshape(1, -1), ng.reshape(1, HEAD_DIM), s0)


def _top2(vals):
    n = len(vals)
    m1 = functools.reduce(jnp.maximum, vals)
    i1 = jnp.full(m1.shape, n - 1, jnp.int32)
    for k in range(n - 2, -1, -1):
        i1 = jnp.where(vals[k] == m1, k, i1)
    rest = [jnp.where(i1 == k, -1.0, vals[k]) for k in range(n)]
    m2 = functools.reduce(jnp.maximum, rest)
    i2 = jnp.full(m1.shape, n - 1, jnp.int32)
    for k in range(n - 2, -1, -1):
        i2 = jnp.where(rest[k] == m2, k, i2)
    return m1, i1, m2, i2


def _router_kernel(x_ref, w_ref, b_ref, ex_ref, gt_ref):
    logits = _dot_nt(w_ref[...].astype(BF16), x_ref[...]) + b_ref[...]
    e = jnp.exp(logits - jnp.max(logits, 0, keepdims=True))
    p = e / jnp.sum(e, 0, keepdims=True)
    members = [p[k * N_GROUPS:(k + 1) * N_GROUPS, :] for k in range(GROUP_SIZE)]
    m1, i1, m2, i2 = _top2(members)
    score = m1 + m2
    gid = lax.broadcasted_iota(jnp.int32, score.shape, 0)
    best = jnp.max(score, 0, keepdims=True)
    gsel = jnp.min(jnp.where(score == best, gid, N_GROUPS), 0, keepdims=True)
    sel = gid == gsel
    pick_f = lambda a: jnp.sum(jnp.where(sel, a, 0.0), 0, keepdims=True)
    pick_i = lambda a: jnp.sum(jnp.where(sel, a, 0), 0, keepdims=True)
    p1, p2 = pick_f(m1), pick_f(m2)
    ex_ref[0:1, :] = gsel * GROUP_SIZE + pick_i(i1)
    ex_ref[1:2, :] = gsel * GROUP_SIZE + pick_i(i2)
    gt_ref[0:1, :] = p1 / (p1 + p2)
    gt_ref[1:2, :] = p2 / (p1 + p2)


def _router(x, w_t, b_col, *, tm, name):
    n, d = x.shape
    out_spec = pl.BlockSpec((TOP_K, tm), lambda i: (0, i))
    return pl.pallas_call(
        _router_kernel,
        grid=(pl.cdiv(n, tm),),
        in_specs=[pl.BlockSpec((tm, d), lambda i: (i, 0)),
                  pl.BlockSpec((N_EXPERTS, d), lambda i: (0, 0)),
                  pl.BlockSpec((N_EXPERTS, 1), lambda i: (0, 0))],
        out_specs=[out_spec, out_spec],
        out_shape=[jax.ShapeDtypeStruct((TOP_K, n), jnp.int32), jax.ShapeDtypeStruct((TOP_K, n), F32)],
        compiler_params=_params("arbitrary"),
        name=name,
    )(x, w_t, b_col)


MOE_SLOT_ROWS = 576


def _slot_index(s, j, nu_ref, nj):
    used = s < nu_ref[0]
    return jnp.minimum(s, nu_ref[0] - 1), jnp.where(used, j, nj - 1)


def _gate_up_kernel(se_ref, ss_ref, nu_ref, tok_ref, x_hbm, wg_ref, wu_ref, h_ref, xg_ref, xb_ref, sem,
                    *, n_slots, nj):
    s = pl.program_id(0)
    j = pl.program_id(1)
    n_used = nu_ref[0]
    rows = xb_ref.shape[0]
    part = rows // nj

    def start_rows(slot, first, count):
        buf = slot % 2
        base = ss_ref[slot] + first
        for r in range(count):
            tok = tok_ref[base + r]
            pltpu.make_async_copy(x_hbm.at[pl.ds(tok, 1), :], xg_ref.at[buf, pl.ds(first + r, 1), :],
                                  sem.at[buf]).start()

    def wait_buffer(slot):
        buf = slot % 2
        pltpu.make_async_copy(x_hbm.at[pl.ds(0, rows), :], xg_ref.at[buf], sem.at[buf]).wait()

    @pl.when(jnp.logical_and(s == 0, j == 0))
    def _():
        start_rows(0, 0, rows)

    @pl.when(jnp.logical_and(s <= n_used, j == 0))
    def _():
        wait_buffer(s)

    @pl.when(jnp.logical_and(s < n_used, j == 0))
    def _():
        xb_ref[...] = xg_ref[s % 2].astype(BF16)

    @pl.when(s < n_used)
    def _():
        start_rows(s + 1, j * part, part)
        x = xb_ref[...]
        gate = _dot(x, wg_ref[...].astype(BF16))
        up = _dot(x, wu_ref[...].astype(BF16))
        h_ref[...] = (_silu(gate) * up).astype(h_ref.dtype)

    @pl.when(jnp.logical_and(jnp.logical_and(s == n_slots - 1, j == nj - 1), n_used == n_slots))
    def _():
        wait_buffer(n_slots)


def _down_kernel(se_ref, nu_ref, h_ref, wd_ref, y_ref):
    @pl.when(pl.program_id(0) < nu_ref[0])
    def _():
        y_ref[...] = _dot(h_ref[...], wd_ref[...].astype(BF16))


def _expert_ffn(x_rows, slot_expert, slot_start, n_used, tok_sorted, w_gate_up, w_down, layer, *, th, tn):
    n, d = x_rows.shape
    d_exp = w_down.shape[2]
    n_slots = slot_expert.shape[0] - 1
    rows = n_slots * MOE_SLOT_ROWS
    nj = d_exp // th

    def w_map(col0):
        def index(s, j, se, ss, nu, tok):
            se_, je_ = _slot_index(s, j, nu, nj)
            return (layer, se[se_], 0, col0 + je_)
        return index

    def h_map(s, j, se, ss, nu, tok):
        return _slot_index(s, j, nu, nj)

    h = pl.pallas_call(
        functools.partial(_gate_up_kernel, n_slots=n_slots, nj=nj),
        grid_spec=pltpu.PrefetchScalarGridSpec(
            num_scalar_prefetch=4,
            grid=(n_slots, nj),
            in_specs=[
                pl.BlockSpec(memory_space=pl.ANY),
                pl.BlockSpec((None, None, d, th), w_map(0)),
                pl.BlockSpec((None, None, d, th), w_map(nj)),
            ],
            out_specs=pl.BlockSpec((MOE_SLOT_ROWS, th), h_map),
            scratch_shapes=[pltpu.VMEM((2, MOE_SLOT_ROWS, d), F32), pltpu.VMEM((MOE_SLOT_ROWS, d), BF16),
                            pltpu.SemaphoreType.DMA((2,))],
        ),
        out_shape=jax.ShapeDtypeStruct((rows, d_exp), BF16),
        compiler_params=_params("arbitrary", "arbitrary"),
        name=f"moe_gate_up_{layer}",
    )(slot_expert, slot_start, n_used, tok_sorted, x_rows, w_gate_up, w_gate_up)

    nj2 = d // tn

    def wd_map(s, j, se, nu):
        se_, je_ = _slot_index(s, j, nu, nj2)
        return (layer, se[se_], 0, je_)

    return pl.pallas_call(
        _down_kernel,
        grid_spec=pltpu.PrefetchScalarGridSpec(
            num_scalar_prefetch=2,
            grid=(n_slots, nj2),
            in_specs=[
                pl.BlockSpec((MOE_SLOT_ROWS, d_exp), lambda s, j, se, nu: (_slot_index(s, j, nu, nj2)[0], 0)),
                pl.BlockSpec((None, None, d_exp, tn), wd_map),
            ],
            out_specs=pl.BlockSpec((MOE_SLOT_ROWS, tn), lambda s, j, se, nu: _slot_index(s, j, nu, nj2)),
        ),
        out_shape=jax.ShapeDtypeStruct((rows, d), F32),
        compiler_params=_params("arbitrary", "arbitrary"),
        name=f"moe_down_{layer}",
    )(slot_expert, n_used, h, w_down)


def _moe_layer(x_rows, experts, gates, w_gate_up, w_down, layer, ln_g, ln_b, *, tm_ln, split_at=None):
    n, d = x_rows.shape
    n_assign = n * TOP_K
    r = MOE_SLOT_ROWS
    flat_e = experts.reshape(-1)
    order = jnp.argsort(flat_e).astype(jnp.int32)
    e_sorted = flat_e[order]
    tok_sorted = order // TOP_K
    counts = jnp.bincount(flat_e, length=N_EXPERTS).astype(jnp.int32)
    starts = jnp.cumsum(counts) - counts
    slots_per = (counts + r - 1) // r
    slot_end = jnp.cumsum(slots_per)
    slot_first = slot_end - slots_per
    n_slots = -(-n_assign // r) + N_EXPERTS
    sid = jnp.arange(n_slots + 1, dtype=jnp.int32)
    used = sid < slot_end[-1]
    slot_expert = jnp.minimum(jnp.searchsorted(slot_end, sid, side='right'), N_EXPERTS - 1).astype(jnp.int32)
    local = sid - slot_first[slot_expert]
    slot_start = jnp.where(used, starts[slot_expert] + local * r, 0).astype(jnp.int32)
    n_used = slot_end[-1].astype(jnp.int32).reshape(1)
    tok_sorted = jnp.pad(tok_sorted, (0, r))
    rank = jnp.arange(n_assign, dtype=jnp.int32) - starts[e_sorted]
    dest_sorted = (slot_first[e_sorted] + rank // r) * r + rank % r
    dest = dest_sorted[jnp.argsort(order)].reshape(n, TOP_K)
    ys = _expert_ffn(x_rows, slot_expert, slot_start, n_used, tok_sorted, w_gate_up, w_down, layer, th=256, tn=1024)
    return _ln_moe(x_rows, ys, dest, gates, ln_g, ln_b, tm=tm_ln, name=f"ln_moe_{layer}", split_at=split_at)


def kernel(x_prompt, x_sample, cache_win_k, cache_win_v, state_conv, state_hgrn, w_in_ab, rel_bias, conv_w,
           conv_b, conv_norm_g, conv_norm_b, w_out_ab, w_in_c, hgrn_lb, hgrn_norm_g, w_out_c, ln_g, ln_b,
           router_w, router_b, w_gate_up, w_down):
    bp, sp, d = x_prompt.shape
    bs, ts, _ = x_sample.shape
    n_p, n_s = bp * sp, bs * ts
    attn_w = d // 2
    n_ah = attn_w // HEAD_DIM
    conv_ch = d - attn_w
    n_hh = d // HEAD_DIM
    keep = CONV_WIDTH - 1

    lb_soft = jax.nn.softmax(hgrn_lb.astype(F32), axis=0)
    lower_bounds = jnp.cumsum(lb_soft, axis=0) - lb_soft[0]
    perm = (jnp.arange(N_EXPERTS) % N_GROUPS) * GROUP_SIZE + jnp.arange(N_EXPERTS) // N_GROUPS
    router_wt = router_w.T[perm]
    router_bc = router_b.astype(F32)[perm].reshape(N_EXPERTS, 1)

    x_main, x_extra = x_prompt.reshape(n_p, d), x_sample.reshape(n_s, d)
    xb_main, xb_extra, xb_row0 = x_main.astype(BF16), x_extra.astype(BF16), 0
    outs = {}
    for l in range(DEPTH):
        if l % 2 == 0:
            a = l // 2
            n_in = w_in_ab.shape[2]
            proj_p, proj_s = _matmul([xb_main], [xb_extra], w_in_ab, a, n_in, tm=512, tn=512, rows=n_p,
                                     extra_rows=n_s, extra_row0=xb_row0, name=f"ab_in{l}")
            pp3 = proj_p.reshape(bp, sp, n_in)
            attn_p = _attn_prompt(pp3, rel_bias, n_ah, tq=256)
            conv_p, tail_p = _conv_module(pp3, 3 * attn_w // conv_ch, jnp.zeros((bp, keep, conv_ch), F32),
                                          conv_w[a], conv_b[a], conv_norm_g[a], conv_norm_b[a],
                                          ch=conv_ch, t=128, out_dtype=BF16, name=f"conv_p{l}")
            attn_s = _attn_sample(proj_s, cache_win_k[a], cache_win_v[a], rel_bias, ts, hg=4)
            conv_s, tail_s = _conv_module(proj_s.reshape(bs, ts, n_in), 3 * attn_w // conv_ch, state_conv[a],
                                          conv_w[a], conv_b[a], conv_norm_g[a], conv_norm_b[a],
                                          ch=conv_ch, t=ts, out_dtype=F32, name=f"conv_s{l}")
            acts_p = [attn_p.reshape(n_p, attn_w), conv_p.reshape(n_p, conv_ch)]
            acts_s = [attn_s.astype(BF16), conv_s.reshape(n_s, conv_ch).astype(BF16)]
            w_out, lw = w_out_ab, a
            outs.setdefault("wk_p", []).append(proj_p[:, attn_w:2 * attn_w].reshape(bp, sp, n_ah, HEAD_DIM))
            outs.setdefault("wv_p", []).append(proj_p[:, 2 * attn_w:3 * attn_w].reshape(bp, sp, n_ah, HEAD_DIM))
            outs.setdefault("cv_p", []).append(tail_p)
            outs.setdefault("wk_s", []).append(proj_s[:, attn_w:2 * attn_w].reshape(bs, ts, n_ah, HEAD_DIM))
            outs.setdefault("wv_s", []).append(proj_s[:, 2 * attn_w:3 * attn_w].reshape(bs, ts, n_ah, HEAD_DIM))
            outs.setdefault("cv_s", []).append(tail_s)
        else:
            c = l // 2
            n_in = w_in_c.shape[2]
            proj_p, proj_s = _matmul([xb_main], [xb_extra], w_in_c, c, n_in, tm=512, tn=512, rows=n_p,
                                     extra_rows=n_s, extra_row0=xb_row0, name=f"c_in{l}")
            o_p, h_p = _hgrn(proj_p.reshape(bp, sp, n_in), jnp.zeros((bp, n_hh, HEAD_DIM, HEAD_DIM), F32),
                             lower_bounds[l], hgrn_norm_g[c], n_hh, hg=4, t=256, c=64,
                             out_dtype=BF16, name=f"hgrn_p{l}")
            o_s, h_s = _hgrn(proj_s.reshape(bs, ts, n_in), state_hgrn[c], lower_bounds[l], hgrn_norm_g[c],
                             n_hh, hg=4, t=ts, c=ts, out_dtype=F32, name=f"hgrn_s{l}")
            acts_p = [o_p.reshape(n_p, d)]
            acts_s = [o_s.reshape(n_s, d).astype(BF16)]
            w_out, lw = w_out_c, c
            outs.setdefault("hs_p", []).append(h_p)
            outs.setdefault("hs_s", []).append(h_s)
        m_p, m_s = _matmul(acts_p, acts_s, w_out, lw, d, tm=512, tn=512, rows=n_p, extra_rows=n_s, name=f"mix_out{l}")
        x, xb = _ln_res(x_main, x_extra, m_p, m_s, ln_g[l, 0], ln_b[l, 0], tm=256, name=f"ln_mix{l}")
        experts, gates = _router(xb, router_wt, router_bc, tm=512, name=f"router{l}")
        x, xb = _moe_layer(x, experts.T, gates.T, w_gate_up, w_down, l, ln_g[l, 1], ln_b[l, 1], tm_ln=128,
                           split_at=n_p if l == DEPTH - 1 else None)
        x_main, x_extra, xb_main, xb_extra, xb_row0 = x, x, xb, xb, n_p
    y_p, y_s = x, xb
    stack = lambda k: jnp.stack(outs[k])
    return (y_p.reshape(bp, sp, d), y_s.reshape(bs, ts, d), stack("wk_p"), stack("wv_p"), stack("cv_p"),
            stack("hs_p"), stack("wk_s"), stack("wv_s"), stack("cv_s"), stack("hs_s"))
```

```python
import functools
import math

import jax
import jax.numpy as jnp
from jax import lax
from jax.experimental import pallas as pl
from jax.experimental.pallas import tpu as pltpu

F32 = jnp.float32
BF16 = jnp.bfloat16

DEPTH = 2
HEAD_DIM = 128
DILATED_PATTERNS = ((128, 1), (512, 4), (2048, 16))
NUM_BUCKETS = 32
MAX_EXACT = 16
MAX_DISTANCE = 2048
CONV_WIDTH = 31
N_EXPERTS = 32
N_GROUPS = 8
GROUP_SIZE = N_EXPERTS // N_GROUPS
TOP_K = 2
ALPHA = (2.0 * DEPTH) ** 0.25
EPS = 1e-5
NEG = -1e30
VMEM_LIMIT = 56 * 1024 * 1024
CONV_HALO = 32


def _params(*sem):
    return pltpu.CompilerParams(dimension_semantics=sem, vmem_limit_bytes=VMEM_LIMIT)


def _sigmoid(x):
    return 1.0 / (1.0 + jnp.exp(-x))


def _silu(x):
    return x * _sigmoid(x)


def _dot_nt(a, b, precision=None):
    return lax.dot_general(a, b, (((1,), (1,)), ((), ())), preferred_element_type=F32, precision=precision)


def _dot_tn(a, b, precision=None):
    return lax.dot_general(a, b, (((0,), (0,)), ((), ())), preferred_element_type=F32, precision=precision)


def _dot(a, b, precision=None):
    return jnp.dot(a, b, preferred_element_type=F32, precision=precision)


def _mm_kernel(*refs, n_x, n_main):
    xm_refs, xe_refs = refs[:n_x], refs[n_x:2 * n_x]
    w_ref, om_ref, oe_ref, wb_ref = refs[2 * n_x:]
    m = pl.program_id(1)

    @pl.when(m == 0)
    def _():
        wb_ref[...] = w_ref[...].astype(BF16)

    def product(x_refs, o_ref):
        acc = None
        off = 0
        for xr in x_refs:
            k = xr.shape[-1]
            part = _dot(xr[...], wb_ref[off:off + k, :])
            acc = part if acc is None else acc + part
            off += k
        o_ref[...] = acc.astype(o_ref.dtype)

    @pl.when(m < n_main)
    def _():
        product(xm_refs, om_ref)

    @pl.when(m == n_main)
    def _():
        product(xe_refs, oe_ref)


def _matmul(xs, xs_extra, w, layer, n_cols, *, tm, tn, name, rows, extra_rows, extra_row0=0):
    k_tot = sum(x.shape[1] for x in xs)
    assert w.shape[1] == k_tot and rows % tm == 0 and n_cols % tn == 0 and extra_row0 % extra_rows == 0
    assert [x.shape[1] for x in xs] == [x.shape[1] for x in xs_extra]
    n_main = rows // tm
    eoff = extra_row0 // extra_rows
    main_row = lambda m: jnp.minimum(m, n_main - 1)
    in_specs = [pl.BlockSpec((tm, x.shape[1]), lambda n, m: (main_row(m), 0)) for x in xs]
    in_specs += [pl.BlockSpec((extra_rows, x.shape[1]), lambda n, m: (eoff, 0)) for x in xs_extra]
    in_specs.append(pl.BlockSpec((None, k_tot, tn), lambda n, m: (layer, 0, n)))
    return pl.pallas_call(
        functools.partial(_mm_kernel, n_x=len(xs), n_main=n_main),
        grid=(n_cols // tn, n_main + 1),
        in_specs=in_specs,
        out_specs=[pl.BlockSpec((tm, tn), lambda n, m: (main_row(m), n)),
                   pl.BlockSpec((extra_rows, tn), lambda n, m: (0, n))],
        out_shape=[jax.ShapeDtypeStruct((rows, n_cols), F32), jax.ShapeDtypeStruct((extra_rows, n_cols), F32)],
        scratch_shapes=[pltpu.VMEM((k_tot, tn), BF16)],
        compiler_params=_params("arbitrary", "arbitrary"),
        name=name,
    )(*xs, *xs_extra, w)


def _layer_norm_rows(z, g, b):
    mu = jnp.mean(z, -1, keepdims=True)
    zc = z - mu
    var = jnp.mean(zc * zc, -1, keepdims=True)
    return zc * lax.rsqrt(var + EPS) * g + b


LN_ROW_CHUNK = 32


def _ln_res_kernel(x_ref, xe_ref, m_ref, me_ref, g_ref, b_ref, of_ref, ob_ref):
    last = pl.program_id(0) == pl.num_programs(0) - 1
    n_extra = me_ref.shape[0]
    for r0 in range(0, x_ref.shape[0], LN_ROW_CHUNK):
        rows = slice(r0, r0 + LN_ROW_CHUNK)
        res, mix = x_ref[rows, :], m_ref[rows, :]
        if r0 < n_extra:
            res = jnp.where(last, xe_ref[rows, :], res)
            mix = jnp.where(last, me_ref[rows, :], mix)
        y = _layer_norm_rows(ALPHA * res + mix, g_ref[...], b_ref[...])
        of_ref[rows, :] = y
        ob_ref[rows, :] = y.astype(BF16)


def _ln_res(x_main, x_extra, m_main, m_extra, g, b, *, tm, name):
    d = x_main.shape[1]
    n_main, n_extra = m_main.shape[0], m_extra.shape[0]
    assert n_main % tm == 0 and x_extra.shape[0] % n_extra == 0 and n_extra % LN_ROW_CHUNK == 0 and n_extra <= tm
    rows = n_main + n_extra
    xe_blk = x_extra.shape[0] // n_extra - 1
    main_spec = pl.BlockSpec((tm, d), lambda i: (jnp.minimum(i, n_main // tm - 1), 0))
    row_spec = pl.BlockSpec((tm, d), lambda i: (i, 0))
    vec_spec = pl.BlockSpec((1, d), lambda i: (0, 0))
    return pl.pallas_call(
        _ln_res_kernel,
        grid=(n_main // tm + 1,),
        in_specs=[main_spec, pl.BlockSpec((n_extra, d), lambda i: (xe_blk, 0)),
                  main_spec, pl.BlockSpec((n_extra, d), lambda i: (0, 0)), vec_spec, vec_spec],
        out_specs=[row_spec, row_spec],
        out_shape=[jax.ShapeDtypeStruct((rows, d), F32), jax.ShapeDtypeStruct((rows, d), BF16)],
        compiler_params=_params("arbitrary"),
        name=name,
    )(x_main, x_extra, m_main, m_extra, g.reshape(1, d), b.reshape(1, d))


def _ln_moe_kernel(dest_ref, x_ref, gt_ref, g_ref, b_ref, ys_hbm, oa_ref, ob_ref, yg_ref, sem,
                   *, tm, n_steps, split):
    i = pl.program_id(0)

    def start_rows(step):
        buf = step % 2
        base = step * (tm * TOP_K)
        for r in range(tm):
            for k in range(TOP_K):
                row = dest_ref[base + r * TOP_K + k]
                pltpu.make_async_copy(ys_hbm.at[pl.ds(row, 1), :], yg_ref.at[buf, pl.ds(k * tm + r, 1), :],
                                      sem.at[buf]).start()

    def wait_buffer(step):
        buf = step % 2
        pltpu.make_async_copy(ys_hbm.at[pl.ds(0, TOP_K * tm), :], yg_ref.at[buf], sem.at[buf]).wait()

    @pl.when(i == 0)
    def _():
        start_rows(0)

    wait_buffer(i)
    start_rows(i + 1)
    buf = i % 2
    for r0 in range(0, tm, LN_ROW_CHUNK):
        rows = slice(r0, r0 + LN_ROW_CHUNK)
        gt = gt_ref[rows, :]
        ff = (gt[:, 0:1] * yg_ref[buf, r0:r0 + LN_ROW_CHUNK, :]
              + gt[:, 1:2] * yg_ref[buf, tm + r0:tm + r0 + LN_ROW_CHUNK, :])
        y = _layer_norm_rows(ALPHA * x_ref[rows, :] + ff, g_ref[...], b_ref[...])
        if not split:
            oa_ref[rows, :] = y
            ob_ref[rows, :] = y.astype(BF16)
        else:
            @pl.when(i < n_steps - 1)
            def _():
                oa_ref[rows, :] = y

            if r0 < ob_ref.shape[0]:
                @pl.when(i == n_steps - 1)
                def _():
                    ob_ref[rows, :] = y

    @pl.when(i == n_steps - 1)
    def _():
        wait_buffer(i + 1)


def _ln_moe(x, ys, dest, gates, g, b, *, tm, name, split_at=None):
    rows, d = x.shape
    n_steps = pl.cdiv(rows, tm)
    dest_flat = jnp.pad(dest.reshape(-1), (0, (n_steps + 1) * tm * TOP_K - rows * TOP_K))
    row_spec = pl.BlockSpec((tm, d), lambda i, dr: (i, 0))
    vec_spec = pl.BlockSpec((1, d), lambda i, dr: (0, 0))
    if split_at is None:
        out_specs = [row_spec, row_spec]
        out_shape = [jax.ShapeDtypeStruct((rows, d), F32), jax.ShapeDtypeStruct((rows, d), BF16)]
    else:
        n_tail = rows - split_at
        assert split_at == (n_steps - 1) * tm and n_tail % LN_ROW_CHUNK == 0
        out_specs = [pl.BlockSpec((tm, d), lambda i, dr: (jnp.minimum(i, n_steps - 2), 0)),
                     pl.BlockSpec((n_tail, d), lambda i, dr: (0, 0))]
        out_shape = [jax.ShapeDtypeStruct((split_at, d), F32), jax.ShapeDtypeStruct((n_tail, d), F32)]
    return pl.pallas_call(
        functools.partial(_ln_moe_kernel, tm=tm, n_steps=n_steps, split=split_at is not None),
        grid_spec=pltpu.PrefetchScalarGridSpec(
            num_scalar_prefetch=1,
            grid=(n_steps,),
            in_specs=[row_spec, pl.BlockSpec((tm, TOP_K), lambda i, dr: (i, 0)), vec_spec, vec_spec,
                      pl.BlockSpec(memory_space=pl.ANY)],
            out_specs=out_specs,
            scratch_shapes=[pltpu.VMEM((2, TOP_K * tm, d), F32), pltpu.SemaphoreType.DMA((2,))],
        ),
        out_shape=out_shape,
        compiler_params=_params("arbitrary"),
        name=name,
    )(dest_flat, x, gates, g.reshape(1, d), b.reshape(1, d), ys)


def _rel_bucket(dist):
    dist = dist.astype(jnp.int32)
    d = jnp.maximum(dist, 1).astype(F32)
    large = MAX_EXACT + (jnp.log(d / MAX_EXACT) / math.log(MAX_DISTANCE / MAX_EXACT)
                         * (NUM_BUCKETS - MAX_EXACT)).astype(jnp.int32)
    large = jnp.minimum(large, NUM_BUCKETS - 1)
    return jnp.where(dist < MAX_EXACT, dist, large)


def _pattern_bias(rel_bias, d):
    base = jnp.moveaxis(rel_bias[_rel_bucket(jnp.maximum(d, 0))].astype(F32), -1, 0)
    outs = []
    for w, r in DILATED_PATTERNS:
        ok = (d >= 0) & (d % r == 0) & (d <= w)
        outs.append(jnp.where(ok[None], base, NEG))
    return jnp.stack(outs)


def _merged_bias(rel_bias, d):
    count = sum(((d >= 0) & (d % r == 0) & (d <= w)).astype(F32) for w, r in DILATED_PATTERNS)
    base = jnp.moveaxis(rel_bias[_rel_bucket(jnp.maximum(d, 0))].astype(F32), -1, 0)
    return jnp.where((count > 0)[None], base + jnp.log(jnp.maximum(count, 1.0))[None], NEG)


def _toeplitz_rows(vec, nd, tq):
    span = 2 * tq - 1
    win = jnp.stack([vec[:, k * tq:k * tq + span] for k in range(nd)], 1)
    return jnp.pad(win[..., ::-1], ((0, 0), (0, 0), (0, 1)))


def _attn_prompt_kernel(q_ref, k_ref, v_ref, rev_ref, o_ref, kb_ref, vb_ref, bias_ref, *, tq):
    i = pl.program_id(1)
    nb = q_ref.shape[0]

    @pl.when(i == 0)
    def _():
        kb_ref[...] = k_ref[...].astype(BF16)
        vb_ref[...] = v_ref[...].astype(BF16)
        for delta in range(bias_ref.shape[0]):
            window = jnp.broadcast_to(rev_ref[delta:delta + 1, :], (tq, 2 * tq))
            bias_ref[delta] = pltpu.roll(window, tq + 1, 1, stride=1, stride_axis=0)[:, 0:tq]

    qs = [(q_ref[b] * (HEAD_DIM ** -0.5)).astype(BF16) for b in range(nb)]

    def body(j, carry):
        start = pl.multiple_of(j * tq, tq)
        bias = bias_ref[i - j]
        out = []
        for b in range(nb):
            m, l, acc = carry[b]
            s = _dot_nt(qs[b], kb_ref[b, pl.ds(start, tq), :]) + bias
            m_new = jnp.maximum(m, jnp.max(s, -1, keepdims=True))
            p = jnp.exp(s - m_new)
            a = jnp.exp(m - m_new)
            l = a * l + jnp.sum(p, -1, keepdims=True)
            acc = a * acc + _dot(p.astype(BF16), vb_ref[b, pl.ds(start, tq), :])
            out.append((m_new, l, acc))
        return tuple(out)

    one = (jnp.full((tq, 1), NEG, F32), jnp.zeros((tq, 1), F32), jnp.zeros((tq, HEAD_DIM), F32))
    res = lax.fori_loop(0, i + 1, body, (one,) * nb)
    for b in range(nb):
        _, l, acc = res[b]
        o_ref[b] = (acc / l).astype(o_ref.dtype)


def _attn_prompt(proj, rel_bias, n_heads, *, tq):
    b, s, _ = proj.shape
    nd = s // tq
    rev = _toeplitz_rows(_merged_bias(rel_bias, jnp.arange(-(tq - 1), s)), nd, tq)
    return pl.pallas_call(
        functools.partial(_attn_prompt_kernel, tq=tq),
        grid=(n_heads, nd),
        in_specs=[
            pl.BlockSpec((b, tq, HEAD_DIM), lambda h, i: (0, i, h)),
            pl.BlockSpec((b, s, HEAD_DIM), lambda h, i: (0, 0, n_heads + h)),
            pl.BlockSpec((b, s, HEAD_DIM), lambda h, i: (0, 0, 2 * n_heads + h)),
            pl.BlockSpec((None, nd, 2 * tq), lambda h, i: (h, 0, 0)),
        ],
        out_specs=pl.BlockSpec((b, tq, HEAD_DIM), lambda h, i: (0, i, h)),
        out_shape=jax.ShapeDtypeStruct((b, s, n_heads * HEAD_DIM), BF16),
        scratch_shapes=[pltpu.VMEM((b, s, HEAD_DIM), BF16), pltpu.VMEM((b, s, HEAD_DIM), BF16),
                        pltpu.VMEM((nd, tq, tq), F32)],
        compiler_params=_params("arbitrary", "arbitrary"),
        name="attn_prompt",
    )(proj, proj, proj, rev)


QUERY_PAD = 16


def _attn_sample_kernel(q_ref, kn_ref, vn_ref, bias_ref, kc_hbm, vc_hbm, o_ref, kc_ref, vc_ref, kx_ref, vx_ref,
                        sem, *, hg, t, nhg):
    step = pl.program_id(0) * nhg + pl.program_id(1)
    n_steps = pl.num_programs(0) * nhg
    p = kc_ref.shape[1]
    n_keys, w = kx_ref.shape

    def copies(s, slot):
        b_, g_ = s // nhg, s % nhg
        out = []
        for h in range(hg):
            lanes = pl.ds(h * HEAD_DIM, HEAD_DIM)
            out.append(pltpu.make_async_copy(kc_hbm.at[b_, :, g_ * hg + h, :], kc_ref.at[slot, :, lanes], sem.at[0, slot]))
            out.append(pltpu.make_async_copy(vc_hbm.at[b_, :, g_ * hg + h, :], vc_ref.at[slot, :, lanes], sem.at[1, slot]))
        return out

    @pl.when(step == 0)
    def _():
        for c in copies(0, 0):
            c.start()

    slot = step % 2

    @pl.when(step + 1 < n_steps)
    def _():
        for c in copies(step + 1, 1 - slot):
            c.start()

    for c in copies(step, slot):
        c.wait()
    zrow = jnp.zeros((n_keys - p - t, w), F32)
    kx_ref[0:p, :] = kc_ref[slot].astype(BF16)
    vx_ref[0:p, :] = vc_ref[slot].astype(BF16)
    kx_ref[p:n_keys, :] = jnp.concatenate([kn_ref[...], zrow], 0).astype(BF16)
    vx_ref[p:n_keys, :] = jnp.concatenate([vn_ref[...], zrow], 0).astype(BF16)
    n_pat = bias_ref.shape[0]
    add = lambda a, b: a + b
    for h in range(hg):
        cs = slice(h * HEAD_DIM, (h + 1) * HEAD_DIM)
        q16 = jnp.concatenate([q_ref[:, cs], jnp.zeros((QUERY_PAD - t, HEAD_DIM), F32)], 0).astype(BF16)
        s = _dot_nt(q16, kx_ref[:, cs])[0:t, :] * (HEAD_DIM ** -0.5)
        outs, lses = [], []
        for g in range(n_pat):
            lg = s + bias_ref[g, h]
            m = jnp.max(lg, -1, keepdims=True)
            e = jnp.exp(lg - m)
            den = jnp.sum(e, -1, keepdims=True)
            p16 = jnp.concatenate([e / den, jnp.zeros((QUERY_PAD - t, n_keys), F32)], 0).astype(BF16)
            outs.append(_dot(p16, vx_ref[:, cs])[0:t, :])
            lses.append(m + jnp.log(den))
        top = functools.reduce(jnp.maximum, lses)
        ws = [jnp.exp(l - top) for l in lses]
        tot = functools.reduce(add, ws)
        o_ref[:, cs] = functools.reduce(add, [(wg / tot) * og for wg, og in zip(ws, outs)])


def _attn_sample(proj, cache_k, cache_v, rel_bias, t, *, hg):
    bt = proj.shape[0]
    b, p, n_heads, _ = cache_k.shape
    w = hg * HEAD_DIM
    nhg = n_heads // hg
    assert t <= QUERY_PAD and p % QUERY_PAD == 0 and bt == b * t
    n_keys = p + QUERY_PAD
    key = jnp.arange(n_keys)
    dist = jnp.where(key[None, :] < p + t, p + jnp.arange(t)[:, None] - key[None, :], -1)
    bias = _pattern_bias(rel_bias, dist)
    n_pat = bias.shape[0]
    return pl.pallas_call(
        functools.partial(_attn_sample_kernel, hg=hg, t=t, nhg=nhg),
        grid=(b, nhg),
        in_specs=[
            pl.BlockSpec((t, w), lambda bb, g: (bb, g)),
            pl.BlockSpec((t, w), lambda bb, g: (bb, nhg + g)),
            pl.BlockSpec((t, w), lambda bb, g: (bb, 2 * nhg + g)),
            pl.BlockSpec((n_pat, hg, t, n_keys), lambda bb, g: (0, g, 0, 0)),
            pl.BlockSpec(memory_space=pl.ANY),
            pl.BlockSpec(memory_space=pl.ANY),
        ],
        out_specs=pl.BlockSpec((t, w), lambda bb, g: (bb, g)),
        out_shape=jax.ShapeDtypeStruct((bt, n_heads * HEAD_DIM), F32),
        scratch_shapes=[pltpu.VMEM((2, p, w), F32), pltpu.VMEM((2, p, w), F32),
                        pltpu.VMEM((n_keys, w), BF16), pltpu.VMEM((n_keys, w), BF16),
                        pltpu.SemaphoreType.DMA((2, 2))],
        compiler_params=_params("arbitrary", "arbitrary"),
        name="attn_sample",
    )(proj, proj, proj, bias, cache_k, cache_v)


def _round_bf16(x):
    return x.astype(BF16).astype(F32)


CONV_ALIGN = 8


def _conv_kernel(a_ref, g_ref, st_ref, w_ref, cb_ref, ng_ref, nb_ref, o_ref, tail_ref, ext_ref, y_ref,
                 extr_ref, sh_ref, *, t, rc, cc):
    ti = pl.program_id(1)
    ch = a_ref.shape[-1]
    n_ext = CONV_HALO + t

    @pl.when(ti == 0)
    def _():
        ext_ref[0:CONV_HALO, :] = st_ref[...]
        extr_ref[n_ext:n_ext + CONV_ALIGN, :] = jnp.zeros((CONV_ALIGN, ch), F32)

    ext_ref[CONV_HALO:n_ext, :] = a_ref[...] * _sigmoid(g_ref[...])
    extr_ref[0:n_ext, :] = _round_bf16(ext_ref[...])
    for s in range(CONV_ALIGN):
        sh_ref[s] = extr_ref[s:s + n_ext, :]
    first = CONV_HALO - (CONV_WIDTH - 1)
    for r0 in range(0, t, rc):
        for c0 in range(0, ch, cc):
            acc = jnp.zeros((rc, cc), F32)
            for j in range(CONV_WIDTH):
                s, base = (first + j) % CONV_ALIGN, (first + j) // CONV_ALIGN * CONV_ALIGN
                acc = acc + w_ref[j:j + 1, c0:c0 + cc] * sh_ref[s, base + r0:base + r0 + rc, c0:c0 + cc]
            y_ref[r0:r0 + rc, c0:c0 + cc] = acc + cb_ref[:, c0:c0 + cc]
    for r0 in range(0, t, rc):
        y = _layer_norm_rows(y_ref[r0:r0 + rc, :], ng_ref[...], nb_ref[...])
        o_ref[r0:r0 + rc, :] = _silu(y).astype(o_ref.dtype)

    @pl.when(ti == pl.num_programs(1) - 1)
    def _():
        tail_ref[...] = ext_ref[t + first:n_ext, :]

    ext_ref[0:CONV_HALO, :] = ext_ref[t:n_ext, :]


def _conv_module(proj, a_blk, state, conv_w, conv_b, ng, nb, *, ch, t, out_dtype, name):
    b, length, _ = proj.shape
    keep = CONV_WIDTH - 1
    st = jnp.pad(state, ((0, 0), (CONV_HALO - keep, 0), (0, 0)))
    rc = min(t, 32)
    vec_spec = pl.BlockSpec((1, ch), lambda bb, i: (0, 0))
    return pl.pallas_call(
        functools.partial(_conv_kernel, t=t, rc=rc, cc=512),
        grid=(b, length // t),
        in_specs=[
            pl.BlockSpec((None, t, ch), lambda bb, i: (bb, i, a_blk)),
            pl.BlockSpec((None, t, ch), lambda bb, i: (bb, i, a_blk + 1)),
            pl.BlockSpec((None, CONV_HALO, ch), lambda bb, i: (bb, 0, 0)),
            pl.BlockSpec((CONV_WIDTH, ch), lambda bb, i: (0, 0)),
            vec_spec, vec_spec, vec_spec,
        ],
        out_specs=[
            pl.BlockSpec((None, t, ch), lambda bb, i: (bb, i, 0)),
            pl.BlockSpec((None, keep, ch), lambda bb, i: (bb, 0, 0)),
        ],
        out_shape=[jax.ShapeDtypeStruct((b, length, ch), out_dtype),
                   jax.ShapeDtypeStruct((b, keep, ch), F32)],
        scratch_shapes=[pltpu.VMEM((CONV_HALO + t, ch), F32), pltpu.VMEM((t, ch), F32),
                        pltpu.VMEM((CONV_HALO + t + CONV_ALIGN, ch), F32),
                        pltpu.VMEM((CONV_ALIGN, CONV_HALO + t, ch), F32)],
        compiler_params=_params("arbitrary", "arbitrary"),
        name=name,
    )(proj, proj, st, conv_w, conv_b.reshape(1, ch), ng.reshape(1, ch), nb.reshape(1, ch))


def _split3(x):
    a1 = x.astype(BF16)
    r1 = x - a1.astype(F32)
    a2 = r1.astype(BF16)
    a3 = (r1 - a2.astype(F32)).astype(BF16)
    return a1, a2, a3


def _hgrn_gates(q, f, lb):
    fg = lb + (1.0 - lb) * _sigmoid(f)
    return _silu(q), 1.0 - fg, jnp.log(fg)


def _hgrn_finish(o, g, ng, dtype):
    normed = o * lax.rsqrt(jnp.mean(o * o, -1, keepdims=True) + EPS) * ng
    return (normed * _silu(g)).astype(dtype)


def _hgrn_chunk_kernel(q_ref, f_ref, i_ref, g_ref, lb_ref, ng_ref, s0_ref, o_ref, sfin_ref, st_ref, *, hg, t, c):
    ti = pl.program_id(2)
    dk = HEAD_DIM

    @pl.when(ti == 0)
    def _():
        for h in range(hg):
            st_ref[h] = s0_ref[h].T

    lb = lb_ref[...]
    ng = ng_ref[...]
    causal = lax.broadcasted_iota(jnp.int32, (c, c), 0) >= lax.broadcasted_iota(jnp.int32, (c, c), 1)
    tri = causal.astype(BF16)
    heads = [slice(h * dk, (h + 1) * dk) for h in range(hg)]
    chunks = [slice(c0, c0 + c) for c0 in range(0, t, c)]
    q_in, decs, intra, incr = [], [], [], []
    for rows in chunks:
        qq, kk, logf = _hgrn_gates(q_ref[rows, :], f_ref[rows, :], lb)
        l1, l2, l3 = _split3(logf)
        cum = _dot(tri, l1) + _dot(tri, l2) + _dot(tri, l3)
        mid = cum[c // 2 - 1:c // 2, :]
        last = cum[c - 1:c, :]
        q_md = (qq * jnp.exp(cum - mid)).astype(BF16)
        k_md = (kk * jnp.exp(mid - cum)).astype(BF16)
        k_end = (kk * jnp.exp(last - cum)).astype(BF16)
        q_in.append((qq * jnp.exp(cum)).astype(BF16))
        decs.append(jnp.exp(last))
        vs = [i_ref[rows, cs].astype(BF16) for cs in heads]
        scores = [jnp.where(causal, _dot_nt(q_md[:, cs], k_md[:, cs]), 0.0).astype(BF16) for cs in heads]
        intra.append([_dot(sc, v) for sc, v in zip(scores, vs)])
        incr.append([_dot_tn(v, k_end[:, cs]) for v, cs in zip(vs, heads)])
    states = [st_ref[h] for h in range(hg)]
    for n, rows in enumerate(chunks):
        for h, cs in enumerate(heads):
            o = intra[n][h] + _dot_nt(q_in[n][:, cs], states[h].astype(BF16))
            states[h] = states[h] * decs[n][:, cs] + incr[n][h]
            o_ref[rows, cs] = _hgrn_finish(o, g_ref[rows, cs], ng, o_ref.dtype)
    for h in range(hg):
        st_ref[h] = states[h]

    @pl.when(ti == pl.num_programs(2) - 1)
    def _():
        for h in range(hg):
            sfin_ref[h] = st_ref[h].T


def _hgrn_step_kernel(q_ref, f_ref, i_ref, g_ref, lb_ref, ng_ref, s0_ref, o_ref, sfin_ref, *, hg, c):
    dk = HEAD_DIM
    hi = lax.Precision.HIGHEST
    ng = ng_ref[...]
    step = lax.broadcasted_iota(jnp.int32, (c, 1), 0)
    qq, kk, logf = _hgrn_gates(q_ref[...], f_ref[...], lb_ref[...])
    acc = [logf[0:1, :]]
    for r in range(1, c):
        acc.append(acc[-1] + logf[r:r + 1, :])
    cum = jnp.concatenate(acc, 0)
    last = cum[c - 1:c, :]
    q_in = qq * jnp.exp(cum)
    k_end = kk * jnp.exp(last - cum)
    dec = jnp.exp(last)
    for h in range(hg):
        cs = slice(h * dk, (h + 1) * dk)
        st = s0_ref[h].T
        v = _round_bf16(i_ref[:, cs])
        o_rows = []
        for r in range(c):
            pair = qq[r:r + 1, cs] * jnp.exp(cum[r:r + 1, cs] - cum[:, cs]) * kk[:, cs]
            sc = jnp.where(step <= r, jnp.sum(pair, -1, keepdims=True), 0.0)
            o_rows.append(jnp.sum(_round_bf16(sc) * v, 0, keepdims=True))
        o = _dot_nt(_round_bf16(q_in[:, cs]), _round_bf16(st), hi) + jnp.concatenate(o_rows, 0)
        sfin_ref[h] = (st * dec[:, cs] + _dot_tn(v, _round_bf16(k_end[:, cs]), hi)).T
        o_ref[:, cs] = _hgrn_finish(o, g_ref[:, cs], ng, o_ref.dtype)


def _hgrn(proj, s0, lb, ng, n_heads, *, hg, t, c, out_dtype, name):
    b, length, _ = proj.shape
    w = hg * HEAD_DIM
    nhg = n_heads // hg
    col = lambda k: pl.BlockSpec((None, t, w), lambda bb, g, i: (bb, i, k * nhg + g))
    st_spec = pl.BlockSpec((None, hg, HEAD_DIM, HEAD_DIM), lambda bb, g, i: (bb, g, 0, 0))
    if t == length == c:
        body, scratch = functools.partial(_hgrn_step_kernel, hg=hg, c=c), []
    else:
        body = functools.partial(_hgrn_chunk_kernel, hg=hg, t=t, c=c)
        scratch = [pltpu.VMEM((hg, HEAD_DIM, HEAD_DIM), F32)]
    return pl.pallas_call(
        body,
        grid=(b, nhg, length // t),
        in_specs=[col(0), col(1), col(2), col(3),
                  pl.BlockSpec((1, w), lambda bb, g, i: (0, g)),
                  pl.BlockSpec((1, HEAD_DIM), lambda bb, g, i: (0, 0)),
                  st_spec],
        out_specs=[pl.BlockSpec((None, t, w), lambda bb, g, i: (bb, i, g)), st_spec],
        out_shape=[jax.ShapeDtypeStruct((b, length, n_heads * HEAD_DIM), out_dtype),
                   jax.ShapeDtypeStruct(s0.shape, F32)],
        scratch_shapes=scratch,
        compiler_params=_params("arbitrary", "arbitrary", "arbitrary"),
        name=name,
    )(proj, proj, proj, proj, lb.reshape(1, -1), ng.reshape(1, HEAD_DIM), s0)


def _top2(vals):
    n = len(vals)
    m1 = functools.reduce(jnp.maximum, vals)
    i1 = jnp.full(m1.shape, n - 1, jnp.int32)
    for k in range(n - 2, -1, -1):
        i1 = jnp.where(vals[k] == m1, k, i1)
    rest = [jnp.where(i1 == k, -1.0, vals[k]) for k in range(n)]
    m2 = functools.reduce(jnp.maximum, rest)
    i2 = jnp.full(m1.shape, n - 1, jnp.int32)
    for k in range(n - 2, -1, -1):
        i2 = jnp.where(rest[k] == m2, k, i2)
    return m1, i1, m2, i2


def _router_kernel(x_ref, w_ref, b_ref, ex_ref, gt_ref):
    logits = _dot_nt(w_ref[...].astype(BF16), x_ref[...]) + b_ref[...]
    e = jnp.exp(logits - jnp.max(logits, 0, keepdims=True))
    p = e / jnp.sum(e, 0, keepdims=True)
    members = [p[k * N_GROUPS:(k + 1) * N_GROUPS, :] for k in range(GROUP_SIZE)]
    m1, i1, m2, i2 = _top2(members)
    score = m1 + m2
    gid = lax.broadcasted_iota(jnp.int32, score.shape, 0)
    best = jnp.max(score, 0, keepdims=True)
    gsel = jnp.min(jnp.where(score == best, gid, N_GROUPS), 0, keepdims=True)
    sel = gid == gsel
    pick_f = lambda a: jnp.sum(jnp.where(sel, a, 0.0), 0, keepdims=True)
    pick_i = lambda a: jnp.sum(jnp.where(sel, a, 0), 0, keepdims=True)
    p1, p2 = pick_f(m1), pick_f(m2)
    ex_ref[0:1, :] = gsel * GROUP_SIZE + pick_i(i1)
    ex_ref[1:2, :] = gsel * GROUP_SIZE + pick_i(i2)
    gt_ref[0:1, :] = p1 / (p1 + p2)
    gt_ref[1:2, :] = p2 / (p1 + p2)


def _router(x, w_t, b_col, *, tm, name):
    n, d = x.shape
    out_spec = pl.BlockSpec((TOP_K, tm), lambda i: (0, i))
    return pl.pallas_call(
        _router_kernel,
        grid=(pl.cdiv(n, tm),),
        in_specs=[pl.BlockSpec((tm, d), lambda i: (i, 0)),
                  pl.BlockSpec((N_EXPERTS, d), lambda i: (0, 0)),
                  pl.BlockSpec((N_EXPERTS, 1), lambda i: (0, 0))],
        out_specs=[out_spec, out_spec],
        out_shape=[jax.ShapeDtypeStruct((TOP_K, n), jnp.int32), jax.ShapeDtypeStruct((TOP_K, n), F32)],
        compiler_params=_params("arbitrary"),
        name=name,
    )(x, w_t, b_col)


MOE_SLOT_ROWS = 576


def _slot_index(s, j, nu_ref, nj):
    used = s < nu_ref[0]
    return jnp.minimum(s, nu_ref[0] - 1), jnp.where(used, j, nj - 1)


def _gate_up_kernel(se_ref, ss_ref, nu_ref, tok_ref, x_hbm, wg_ref, wu_ref, h_ref, xg_ref, xb_ref, sem,
                    *, n_slots, nj):
    s = pl.program_id(0)
    j = pl.program_id(1)
    n_used = nu_ref[0]
    rows = xb_ref.shape[0]
    part = rows // nj

    def start_rows(slot, first, count):
        buf = slot % 2
        base = ss_ref[slot] + first
        for r in range(count):
            tok = tok_ref[base + r]
            pltpu.make_async_copy(x_hbm.at[pl.ds(tok, 1), :], xg_ref.at[buf, pl.ds(first + r, 1), :],
                                  sem.at[buf]).start()

    def wait_buffer(slot):
        buf = slot % 2
        pltpu.make_async_copy(x_hbm.at[pl.ds(0, rows), :], xg_ref.at[buf], sem.at[buf]).wait()

    @pl.when(jnp.logical_and(s == 0, j == 0))
    def _():
        start_rows(0, 0, rows)

    @pl.when(jnp.logical_and(s <= n_used, j == 0))
    def _():
        wait_buffer(s)

    @pl.when(jnp.logical_and(s < n_used, j == 0))
    def _():
        xb_ref[...] = xg_ref[s % 2].astype(BF16)

    @pl.when(s < n_used)
    def _():
        start_rows(s + 1, j * part, part)
        x = xb_ref[...]
        gate = _dot(x, wg_ref[...].astype(BF16))
        up = _dot(x, wu_ref[...].astype(BF16))
        h_ref[...] = (_silu(gate) * up).astype(h_ref.dtype)

    @pl.when(jnp.logical_and(jnp.logical_and(s == n_slots - 1, j == nj - 1), n_used == n_slots))
    def _():
        wait_buffer(n_slots)


def _down_kernel(se_ref, nu_ref, h_ref, wd_ref, y_ref):
    @pl.when(pl.program_id(0) < nu_ref[0])
    def _():
        y_ref[...] = _dot(h_ref[...], wd_ref[...].astype(BF16))


def _expert_ffn(x_rows, slot_expert, slot_start, n_used, tok_sorted, w_gate_up, w_down, layer, *, th, tn):
    n, d = x_rows.shape
    d_exp = w_down.shape[2]
    n_slots = slot_expert.shape[0] - 1
    rows = n_slots * MOE_SLOT_ROWS
    nj = d_exp // th

    def w_map(col0):
        def index(s, j, se, ss, nu, tok):
            se_, je_ = _slot_index(s, j, nu, nj)
            return (layer, se[se_], 0, col0 + je_)
        return index

    def h_map(s, j, se, ss, nu, tok):
        return _slot_index(s, j, nu, nj)

    h = pl.pallas_call(
        functools.partial(_gate_up_kernel, n_slots=n_slots, nj=nj),
        grid_spec=pltpu.PrefetchScalarGridSpec(
            num_scalar_prefetch=4,
            grid=(n_slots, nj),
            in_specs=[
                pl.BlockSpec(memory_space=pl.ANY),
                pl.BlockSpec((None, None, d, th), w_map(0)),
                pl.BlockSpec((None, None, d, th), w_map(nj)),
            ],
            out_specs=pl.BlockSpec((MOE_SLOT_ROWS, th), h_map),
            scratch_shapes=[pltpu.VMEM((2, MOE_SLOT_ROWS, d), F32), pltpu.VMEM((MOE_SLOT_ROWS, d), BF16),
                            pltpu.SemaphoreType.DMA((2,))],
        ),
        out_shape=jax.ShapeDtypeStruct((rows, d_exp), BF16),
        compiler_params=_params("arbitrary", "arbitrary"),
        name=f"moe_gate_up_{layer}",
    )(slot_expert, slot_start, n_used, tok_sorted, x_rows, w_gate_up, w_gate_up)

    nj2 = d // tn

    def wd_map(s, j, se, nu):
        se_, je_ = _slot_index(s, j, nu, nj2)
        return (layer, se[se_], 0, je_)

    return pl.pallas_call(
        _down_kernel,
        grid_spec=pltpu.PrefetchScalarGridSpec(
            num_scalar_prefetch=2,
            grid=(n_slots, nj2),
            in_specs=[
                pl.BlockSpec((MOE_SLOT_ROWS, d_exp), lambda s, j, se, nu: (_slot_index(s, j, nu, nj2)[0], 0)),
                pl.BlockSpec((None, None, d_exp, tn), wd_map),
            ],
            out_specs=pl.BlockSpec((MOE_SLOT_ROWS, tn), lambda s, j, se, nu: _slot_index(s, j, nu, nj2)),
        ),
        out_shape=jax.ShapeDtypeStruct((rows, d), F32),
        compiler_params=_params("arbitrary", "arbitrary"),
        name=f"moe_down_{layer}",
    )(slot_expert, n_used, h, w_down)


def _moe_layer(x_rows, experts, gates, w_gate_up, w_down, layer, ln_g, ln_b, *, tm_ln, split_at=None):
    n, d = x_rows.shape
    n_assign = n * TOP_K
    r = MOE_SLOT_ROWS
    flat_e = experts.reshape(-1)
    order = jnp.argsort(flat_e).astype(jnp.int32)
    e_sorted = flat_e[order]
    tok_sorted = order // TOP_K
    counts = jnp.bincount(flat_e, length=N_EXPERTS).astype(jnp.int32)
    starts = jnp.cumsum(counts) - counts
    slots_per = (counts + r - 1) // r
    slot_end = jnp.cumsum(slots_per)
    slot_first = slot_end - slots_per
    n_slots = -(-n_assign // r) + N_EXPERTS
    sid = jnp.arange(n_slots + 1, dtype=jnp.int32)
    used = sid < slot_end[-1]
    slot_expert = jnp.minimum(jnp.searchsorted(slot_end, sid, side='right'), N_EXPERTS - 1).astype(jnp.int32)
    local = sid - slot_first[slot_expert]
    slot_start = jnp.where(used, starts[slot_expert] + local * r, 0).astype(jnp.int32)
    n_used = slot_end[-1].astype(jnp.int32).reshape(1)
    tok_sorted = jnp.pad(tok_sorted, (0, r))
    rank = jnp.arange(n_assign, dtype=jnp.int32) - starts[e_sorted]
    dest_sorted = (slot_first[e_sorted] + rank // r) * r + rank % r
    dest = dest_sorted[jnp.argsort(order)].reshape(n, TOP_K)
    ys = _expert_ffn(x_rows, slot_expert, slot_start, n_used, tok_sorted, w_gate_up, w_down, layer, th=256, tn=2048)
    return _ln_moe(x_rows, ys, dest, gates, ln_g, ln_b, tm=tm_ln, name=f"ln_moe_{layer}", split_at=split_at)


def kernel(x_prompt, x_sample, cache_win_k, cache_win_v, state_conv, state_hgrn, w_in_ab, rel_bias, conv_w,
           conv_b, conv_norm_g, conv_norm_b, w_out_ab, w_in_c, hgrn_lb, hgrn_norm_g, w_out_c, ln_g, ln_b,
           router_w, router_b, w_gate_up, w_down):
    bp, sp, d = x_prompt.shape
    bs, ts, _ = x_sample.shape
    n_p, n_s = bp * sp, bs * ts
    attn_w = d // 2
    n_ah = attn_w // HEAD_DIM
    conv_ch = d - attn_w
    n_hh = d // HEAD_DIM
    keep = CONV_WIDTH - 1

    lb_soft = jax.nn.softmax(hgrn_lb.astype(F32), axis=0)
    lower_bounds = jnp.cumsum(lb_soft, axis=0) - lb_soft[0]
    perm = (jnp.arange(N_EXPERTS) % N_GROUPS) * GROUP_SIZE + jnp.arange(N_EXPERTS) // N_GROUPS
    router_wt = router_w.T[perm]
    router_bc = router_b.astype(F32)[perm].reshape(N_EXPERTS, 1)

    x_main, x_extra = x_prompt.reshape(n_p, d), x_sample.reshape(n_s, d)
    xb_main, xb_extra, xb_row0 = x_main.astype(BF16), x_extra.astype(BF16), 0
    outs = {}
    for l in range(DEPTH):
        if l % 2 == 0:
            a = l // 2
            n_in = w_in_ab.shape[2]
            proj_p, proj_s = _matmul([xb_main], [xb_extra], w_in_ab, a, n_in, tm=512, tn=512, rows=n_p,
                                     extra_rows=n_s, extra_row0=xb_row0, name=f"ab_in{l}")
            pp3 = proj_p.reshape(bp, sp, n_in)
            attn_p = _attn_prompt(pp3, rel_bias, n_ah, tq=256)
            conv_p, tail_p = _conv_module(pp3, 3 * attn_w // conv_ch, jnp.zeros((bp, keep, conv_ch), F32),
                                          conv_w[a], conv_b[a], conv_norm_g[a], conv_norm_b[a],
                                          ch=conv_ch, t=128, out_dtype=BF16, name=f"conv_p{l}")
            attn_s = _attn_sample(proj_s, cache_win_k[a], cache_win_v[a], rel_bias, ts, hg=4)
            conv_s, tail_s = _conv_module(proj_s.reshape(bs, ts, n_in), 3 * attn_w // conv_ch, state_conv[a],
                                          conv_w[a], conv_b[a], conv_norm_g[a], conv_norm_b[a],
                                          ch=conv_ch, t=ts, out_dtype=F32, name=f"conv_s{l}")
            acts_p = [attn_p.reshape(n_p, attn_w), conv_p.reshape(n_p, conv_ch)]
            acts_s = [attn_s.astype(BF16), conv_s.reshape(n_s, conv_ch).astype(BF16)]
            w_out, lw = w_out_ab, a
            outs.setdefault("wk_p", []).append(proj_p[:, attn_w:2 * attn_w].reshape(bp, sp, n_ah, HEAD_DIM))
            outs.setdefault("wv_p", []).append(proj_p[:, 2 * attn_w:3 * attn_w].reshape(bp, sp, n_ah, HEAD_DIM))
            outs.setdefault("cv_p", []).append(tail_p)
            outs.setdefault("wk_s", []).append(proj_s[:, attn_w:2 * attn_w].reshape(bs, ts, n_ah, HEAD_DIM))
            outs.setdefault("wv_s", []).append(proj_s[:, 2 * attn_w:3 * attn_w].reshape(bs, ts, n_ah, HEAD_DIM))
            outs.setdefault("cv_s", []).append(tail_s)
        else:
            c = l // 2
            n_in = w_in_c.shape[2]
            proj_p, proj_s = _matmul([xb_main], [xb_extra], w_in_c, c, n_in, tm=512, tn=512, rows=n_p,
                                     extra_rows=n_s, extra_row0=xb_row0, name=f"c_in{l}")
            o_p, h_p = _hgrn(proj_p.reshape(bp, sp, n_in), jnp.zeros((bp, n_hh, HEAD_DIM, HEAD_DIM), F32),
                             lower_bounds[l], hgrn_norm_g[c], n_hh, hg=4, t=256, c=64,
                             out_dtype=BF16, name=f"hgrn_p{l}")
            o_s, h_s = _hgrn(proj_s.reshape(bs, ts, n_in), state_hgrn[c], lower_bounds[l], hgrn_norm_g[c],
                             n_hh, hg=4, t=ts, c=ts, out_dtype=F32, name=f"hgrn_s{l}")
            acts_p = [o_p.reshape(n_p, d)]
            acts_s = [o_s.reshape(n_s, d).astype(BF16)]
            w_out, lw = w_out_c, c
            outs.setdefault("hs_p", []).append(h_p)
            outs.setdefault("hs_s", []).append(h_s)
        m_p, m_s = _matmul(acts_p, acts_s, w_out, lw, d, tm=512, tn=512, rows=n_p, extra_rows=n_s, name=f"mix_out{l}")
        x, xb = _ln_res(x_main, x_extra, m_p, m_s, ln_g[l, 0], ln_b[l, 0], tm=256, name=f"ln_mix{l}")
        experts, gates = _router(xb, router_wt, router_bc, tm=512, name=f"router{l}")
        x, xb = _moe_layer(x, experts.T, gates.T, w_gate_up, w_down, l, ln_g[l, 1], ln_b[l, 1], tm_ln=128,
                           split_at=n_p if l == DEPTH - 1 else None)
        x_main, x_extra, xb_main, xb_extra, xb_row0 = x, x, xb, xb, n_p
    y_p, y_s = x, xb
    stack = lambda k: jnp.stack(outs[k])
    return (y_p.reshape(bp, sp, d), y_s.reshape(bs, ts, d), stack("wk_p"), stack("wv_p"), stack("cv_p"),
            stack("hs_p"), stack("wk_s"), stack("wv_s"), stack("cv_s"), stack("hs_s"))
```

```python
import functools
import math

import jax
import jax.numpy as jnp
from jax import lax
from jax.experimental import pallas as pl
from jax.experimental.pallas import tpu as pltpu

F32 = jnp.float32
BF16 = jnp.bfloat16

DEPTH = 2
HEAD_DIM = 128
DILATED_PATTERNS = ((128, 1), (512, 4), (2048, 16))
NUM_BUCKETS = 32
MAX_EXACT = 16
MAX_DISTANCE = 2048
CONV_WIDTH = 31
N_EXPERTS = 32
N_GROUPS = 8
GROUP_SIZE = N_EXPERTS // N_GROUPS
TOP_K = 2
ALPHA = (2.0 * DEPTH) ** 0.25
EPS = 1e-5
NEG = -1e30
VMEM_LIMIT = 56 * 1024 * 1024
CONV_HALO = 32


def _params(*sem):
    return pltpu.CompilerParams(dimension_semantics=sem, vmem_limit_bytes=VMEM_LIMIT)


def _sigmoid(x):
    return 1.0 / (1.0 + jnp.exp(-x))


def _silu(x):
    return x * _sigmoid(x)


def _dot_nt(a, b, precision=None):
    return lax.dot_general(a, b, (((1,), (1,)), ((), ())), preferred_element_type=F32, precision=precision)


def _dot_tn(a, b, precision=None):
    return lax.dot_general(a, b, (((0,), (0,)), ((), ())), preferred_element_type=F32, precision=precision)


def _dot(a, b, precision=None):
    return jnp.dot(a, b, preferred_element_type=F32, precision=precision)


def _mm_kernel(*refs, n_x, n_main):
    xm_refs, xe_refs = refs[:n_x], refs[n_x:2 * n_x]
    w_ref, om_ref, oe_ref, wb_ref = refs[2 * n_x:]
    m = pl.program_id(1)

    @pl.when(m == 0)
    def _():
        wb_ref[...] = w_ref[...].astype(BF16)

    def product(x_refs, o_ref):
        acc = None
        off = 0
        for xr in x_refs:
            k = xr.shape[-1]
            part = _dot(xr[...], wb_ref[off:off + k, :])
            acc = part if acc is None else acc + part
            off += k
        o_ref[...] = acc.astype(o_ref.dtype)

    @pl.when(m < n_main)
    def _():
        product(xm_refs, om_ref)

    @pl.when(m == n_main)
    def _():
        product(xe_refs, oe_ref)


def _matmul(xs, xs_extra, w, layer, n_cols, *, tm, tn, name, rows, extra_rows, extra_row0=0):
    k_tot = sum(x.shape[1] for x in xs)
    assert w.shape[1] == k_tot and rows % tm == 0 and n_cols % tn == 0 and extra_row0 % extra_rows == 0
    assert [x.shape[1] for x in xs] == [x.shape[1] for x in xs_extra]
    n_main = rows // tm
    eoff = extra_row0 // extra_rows
    main_row = lambda m: jnp.minimum(m, n_main - 1)
    in_specs = [pl.BlockSpec((tm, x.shape[1]), lambda n, m: (main_row(m), 0)) for x in xs]
    in_specs += [pl.BlockSpec((extra_rows, x.shape[1]), lambda n, m: (eoff, 0)) for x in xs_extra]
    in_specs.append(pl.BlockSpec((None, k_tot, tn), lambda n, m: (layer, 0, n)))
    return pl.pallas_call(
        functools.partial(_mm_kernel, n_x=len(xs), n_main=n_main),
        grid=(n_cols // tn, n_main + 1),
        in_specs=in_specs,
        out_specs=[pl.BlockSpec((tm, tn), lambda n, m: (main_row(m), n)),
                   pl.BlockSpec((extra_rows, tn), lambda n, m: (0, n))],
        out_shape=[jax.ShapeDtypeStruct((rows, n_cols), F32), jax.ShapeDtypeStruct((extra_rows, n_cols), F32)],
        scratch_shapes=[pltpu.VMEM((k_tot, tn), BF16)],
        compiler_params=_params("arbitrary", "arbitrary"),
        name=name,
    )(*xs, *xs_extra, w)


def _layer_norm_rows(z, g, b):
    mu = jnp.mean(z, -1, keepdims=True)
    zc = z - mu
    var = jnp.mean(zc * zc, -1, keepdims=True)
    return zc * lax.rsqrt(var + EPS) * g + b


LN_ROW_CHUNK = 32


def _ln_res_kernel(x_ref, xe_ref, m_ref, me_ref, g_ref, b_ref, of_ref, ob_ref):
    last = pl.program_id(0) == pl.num_programs(0) - 1
    n_extra = me_ref.shape[0]
    for r0 in range(0, x_ref.shape[0], LN_ROW_CHUNK):
        rows = slice(r0, r0 + LN_ROW_CHUNK)
        res, mix = x_ref[rows, :], m_ref[rows, :]
        if r0 < n_extra:
            res = jnp.where(last, xe_ref[rows, :], res)
            mix = jnp.where(last, me_ref[rows, :], mix)
        y = _layer_norm_rows(ALPHA * res + mix, g_ref[...], b_ref[...])
        of_ref[rows, :] = y
        ob_ref[rows, :] = y.astype(BF16)


def _ln_res(x_main, x_extra, m_main, m_extra, g, b, *, tm, name):
    d = x_main.shape[1]
    n_main, n_extra = m_main.shape[0], m_extra.shape[0]
    assert n_main % tm == 0 and x_extra.shape[0] % n_extra == 0 and n_extra % LN_ROW_CHUNK == 0 and n_extra <= tm
    rows = n_main + n_extra
    xe_blk = x_extra.shape[0] // n_extra - 1
    main_spec = pl.BlockSpec((tm, d), lambda i: (jnp.minimum(i, n_main // tm - 1), 0))
    row_spec = pl.BlockSpec((tm, d), lambda i: (i, 0))
    vec_spec = pl.BlockSpec((1, d), lambda i: (0, 0))
    return pl.pallas_call(
        _ln_res_kernel,
        grid=(n_main // tm + 1,),
        in_specs=[main_spec, pl.BlockSpec((n_extra, d), lambda i: (xe_blk, 0)),
                  main_spec, pl.BlockSpec((n_extra, d), lambda i: (0, 0)), vec_spec, vec_spec],
        out_specs=[row_spec, row_spec],
        out_shape=[jax.ShapeDtypeStruct((rows, d), F32), jax.ShapeDtypeStruct((rows, d), BF16)],
        compiler_params=_params("arbitrary"),
        name=name,
    )(x_main, x_extra, m_main, m_extra, g.reshape(1, d), b.reshape(1, d))


def _ln_moe_kernel(dest_ref, x_ref, gt_ref, g_ref, b_ref, ys_hbm, oa_ref, ob_ref, yg_ref, sem,
                   *, tm, n_steps, split):
    i = pl.program_id(0)

    def start_rows(step):
        buf = step % 2
        base = step * (tm * TOP_K)
        for r in range(tm):
            for k in range(TOP_K):
                row = dest_ref[base + r * TOP_K + k]
                pltpu.make_async_copy(ys_hbm.at[pl.ds(row, 1), :], yg_ref.at[buf, pl.ds(k * tm + r, 1), :],
                                      sem.at[buf]).start(priority=k)

    def wait_buffer(step):
        buf = step % 2
        pltpu.make_async_copy(ys_hbm.at[pl.ds(0, TOP_K * tm), :], yg_ref.at[buf], sem.at[buf]).wait()

    @pl.when(i == 0)
    def _():
        start_rows(0)

    wait_buffer(i)
    start_rows(i + 1)
    buf = i % 2
    for r0 in range(0, tm, LN_ROW_CHUNK):
        rows = slice(r0, r0 + LN_ROW_CHUNK)
        gt = gt_ref[rows, :]
        ff = (gt[:, 0:1] * yg_ref[buf, r0:r0 + LN_ROW_CHUNK, :]
              + gt[:, 1:2] * yg_ref[buf, tm + r0:tm + r0 + LN_ROW_CHUNK, :])
        y = _layer_norm_rows(ALPHA * x_ref[rows, :] + ff, g_ref[...], b_ref[...])
        if not split:
            oa_ref[rows, :] = y
            ob_ref[rows, :] = y.astype(BF16)
        else:
            @pl.when(i < n_steps - 1)
            def _():
                oa_ref[rows, :] = y

            if r0 < ob_ref.shape[0]:
                @pl.when(i == n_steps - 1)
                def _():
                    ob_ref[rows, :] = y

    @pl.when(i == n_steps - 1)
    def _():
        wait_buffer(i + 1)


def _ln_moe(x, ys, dest, gates, g, b, *, tm, name, split_at=None):
    rows, d = x.shape
    n_steps = pl.cdiv(rows, tm)
    dest_flat = jnp.pad(dest.reshape(-1), (0, (n_steps + 1) * tm * TOP_K - rows * TOP_K))
    row_spec = pl.BlockSpec((tm, d), lambda i, dr: (i, 0))
    vec_spec = pl.BlockSpec((1, d), lambda i, dr: (0, 0))
    if split_at is None:
        out_specs = [row_spec, row_spec]
        out_shape = [jax.ShapeDtypeStruct((rows, d), F32), jax.ShapeDtypeStruct((rows, d), BF16)]
    else:
        n_tail = rows - split_at
        assert split_at == (n_steps - 1) * tm and n_tail % LN_ROW_CHUNK == 0
        out_specs = [pl.BlockSpec((tm, d), lambda i, dr: (jnp.minimum(i, n_steps - 2), 0)),
                     pl.BlockSpec((n_tail, d), lambda i, dr: (0, 0))]
        out_shape = [jax.ShapeDtypeStruct((split_at, d), F32), jax.ShapeDtypeStruct((n_tail, d), F32)]
    return pl.pallas_call(
        functools.partial(_ln_moe_kernel, tm=tm, n_steps=n_steps, split=split_at is not None),
        grid_spec=pltpu.PrefetchScalarGridSpec(
            num_scalar_prefetch=1,
            grid=(n_steps,),
            in_specs=[row_spec, pl.BlockSpec((tm, TOP_K), lambda i, dr: (i, 0)), vec_spec, vec_spec,
                      pl.BlockSpec(memory_space=pl.ANY)],
            out_specs=out_specs,
            scratch_shapes=[pltpu.VMEM((2, TOP_K * tm, d), F32), pltpu.SemaphoreType.DMA((2,))],
        ),
        out_shape=out_shape,
        compiler_params=_params("arbitrary"),
        name=name,
    )(dest_flat, x, gates, g.reshape(1, d), b.reshape(1, d), ys)


def _rel_bucket(dist):
    dist = dist.astype(jnp.int32)
    d = jnp.maximum(dist, 1).astype(F32)
    large = MAX_EXACT + (jnp.log(d / MAX_EXACT) / math.log(MAX_DISTANCE / MAX_EXACT)
                         * (NUM_BUCKETS - MAX_EXACT)).astype(jnp.int32)
    large = jnp.minimum(large, NUM_BUCKETS - 1)
    return jnp.where(dist < MAX_EXACT, dist, large)


def _pattern_bias(rel_bias, d):
    base = jnp.moveaxis(rel_bias[_rel_bucket(jnp.maximum(d, 0))].astype(F32), -1, 0)
    outs = []
    for w, r in DILATED_PATTERNS:
        ok = (d >= 0) & (d % r == 0) & (d <= w)
        outs.append(jnp.where(ok[None], base, NEG))
    return jnp.stack(outs)


def _merged_bias(rel_bias, d):
    count = sum(((d >= 0) & (d % r == 0) & (d <= w)).astype(F32) for w, r in DILATED_PATTERNS)
    base = jnp.moveaxis(rel_bias[_rel_bucket(jnp.maximum(d, 0))].astype(F32), -1, 0)
    return jnp.where((count > 0)[None], base + jnp.log(jnp.maximum(count, 1.0))[None], NEG)


def _toeplitz_rows(vec, nd, tq):
    span = 2 * tq - 1
    win = jnp.stack([vec[:, k * tq:k * tq + span] for k in range(nd)], 1)
    return jnp.pad(win[..., ::-1], ((0, 0), (0, 0), (0, 1)))


def _attn_prompt_kernel(q_ref, k_ref, v_ref, rev_ref, o_ref, kb_ref, vb_ref, bias_ref, *, tq):
    i = pl.program_id(1)
    nb = q_ref.shape[0]

    @pl.when(i == 0)
    def _():
        kb_ref[...] = k_ref[...].astype(BF16)
        vb_ref[...] = v_ref[...].astype(BF16)
        for delta in range(bias_ref.shape[0]):
            window = jnp.broadcast_to(rev_ref[delta:delta + 1, :], (tq, 2 * tq))
            bias_ref[delta] = pltpu.roll(window, tq + 1, 1, stride=1, stride_axis=0)[:, 0:tq]

    qs = [(q_ref[b] * (HEAD_DIM ** -0.5)).astype(BF16) for b in range(nb)]

    def body(j, carry):
        start = pl.multiple_of(j * tq, tq)
        bias = bias_ref[i - j]
        out = []
        for b in range(nb):
            m, l, acc = carry[b]
            s = _dot_nt(qs[b], kb_ref[b, pl.ds(start, tq), :]) + bias
            m_new = jnp.maximum(m, jnp.max(s, -1, keepdims=True))
            p = jnp.exp(s - m_new)
            a = jnp.exp(m - m_new)
            l = a * l + jnp.sum(p, -1, keepdims=True)
            acc = a * acc + _dot(p.astype(BF16), vb_ref[b, pl.ds(start, tq), :])
            out.append((m_new, l, acc))
        return tuple(out)

    one = (jnp.full((tq, 1), NEG, F32), jnp.zeros((tq, 1), F32), jnp.zeros((tq, HEAD_DIM), F32))
    res = lax.fori_loop(0, i + 1, body, (one,) * nb)
    for b in range(nb):
        _, l, acc = res[b]
        o_ref[b] = (acc / l).astype(o_ref.dtype)


def _attn_prompt(proj, rel_bias, n_heads, *, tq):
    b, s, _ = proj.shape
    nd = s // tq
    rev = _toeplitz_rows(_merged_bias(rel_bias, jnp.arange(-(tq - 1), s)), nd, tq)
    return pl.pallas_call(
        functools.partial(_attn_prompt_kernel, tq=tq),
        grid=(n_heads, nd),
        in_specs=[
            pl.BlockSpec((b, tq, HEAD_DIM), lambda h, i: (0, i, h)),
            pl.BlockSpec((b, s, HEAD_DIM), lambda h, i: (0, 0, n_heads + h)),
            pl.BlockSpec((b, s, HEAD_DIM), lambda h, i: (0, 0, 2 * n_heads + h)),
            pl.BlockSpec((None, nd, 2 * tq), lambda h, i: (h, 0, 0)),
        ],
        out_specs=pl.BlockSpec((b, tq, HEAD_DIM), lambda h, i: (0, i, h)),
        out_shape=jax.ShapeDtypeStruct((b, s, n_heads * HEAD_DIM), BF16),
        scratch_shapes=[pltpu.VMEM((b, s, HEAD_DIM), BF16), pltpu.VMEM((b, s, HEAD_DIM), BF16),
                        pltpu.VMEM((nd, tq, tq), F32)],
        compiler_params=_params("arbitrary", "arbitrary"),
        name="attn_prompt",
    )(proj, proj, proj, rev)


QUERY_PAD = 16


def _attn_sample_kernel(q_ref, kn_ref, vn_ref, bias_ref, kc_hbm, vc_hbm, o_ref, kc_ref, vc_ref, kx_ref, vx_ref,
                        sem, *, hg, t, nhg):
    step = pl.program_id(0) * nhg + pl.program_id(1)
    n_steps = pl.num_programs(0) * nhg
    p = kc_ref.shape[1]
    n_keys, w = kx_ref.shape

    def copies(s, slot):
        b_, g_ = s // nhg, s % nhg
        out = []
        for h in range(hg):
            lanes = pl.ds(h * HEAD_DIM, HEAD_DIM)
            out.append(pltpu.make_async_copy(kc_hbm.at[b_, :, g_ * hg + h, :], kc_ref.at[slot, :, lanes], sem.at[0, slot]))
            out.append(pltpu.make_async_copy(vc_hbm.at[b_, :, g_ * hg + h, :], vc_ref.at[slot, :, lanes], sem.at[1, slot]))
        return out

    @pl.when(step == 0)
    def _():
        for c in copies(0, 0):
            c.start()

    slot = step % 2

    @pl.when(step + 1 < n_steps)
    def _():
        for c in copies(step + 1, 1 - slot):
            c.start()

    for c in copies(step, slot):
        c.wait()
    zrow = jnp.zeros((n_keys - p - t, w), F32)
    kx_ref[0:p, :] = kc_ref[slot].astype(BF16)
    vx_ref[0:p, :] = vc_ref[slot].astype(BF16)
    kx_ref[p:n_keys, :] = jnp.concatenate([kn_ref[...], zrow], 0).astype(BF16)
    vx_ref[p:n_keys, :] = jnp.concatenate([vn_ref[...], zrow], 0).astype(BF16)
    n_pat = bias_ref.shape[0]
    add = lambda a, b: a + b
    for h in range(hg):
        cs = slice(h * HEAD_DIM, (h + 1) * HEAD_DIM)
        q16 = jnp.concatenate([q_ref[:, cs], jnp.zeros((QUERY_PAD - t, HEAD_DIM), F32)], 0).astype(BF16)
        s = _dot_nt(q16, kx_ref[:, cs])[0:t, :] * (HEAD_DIM ** -0.5)
        outs, lses = [], []
        for g in range(n_pat):
            lg = s + bias_ref[g, h]
            m = jnp.max(lg, -1, keepdims=True)
            e = jnp.exp(lg - m)
            den = jnp.sum(e, -1, keepdims=True)
            p16 = jnp.concatenate([e / den, jnp.zeros((QUERY_PAD - t, n_keys), F32)], 0).astype(BF16)
            outs.append(_dot(p16, vx_ref[:, cs])[0:t, :])
            lses.append(m + jnp.log(den))
        top = functools.reduce(jnp.maximum, lses)
        ws = [jnp.exp(l - top) for l in lses]
        tot = functools.reduce(add, ws)
        o_ref[:, cs] = functools.reduce(add, [(wg / tot) * og for wg, og in zip(ws, outs)])


def _attn_sample(proj, cache_k, cache_v, rel_bias, t, *, hg):
    bt = proj.shape[0]
    b, p, n_heads, _ = cache_k.shape
    w = hg * HEAD_DIM
    nhg = n_heads // hg
    assert t <= QUERY_PAD and p % QUERY_PAD == 0 and bt == b * t
    n_keys = p + QUERY_PAD
    key = jnp.arange(n_keys)
    dist = jnp.where(key[None, :] < p + t, p + jnp.arange(t)[:, None] - key[None, :], -1)
    bias = _pattern_bias(rel_bias, dist)
    n_pat = bias.shape[0]
    return pl.pallas_call(
        functools.partial(_attn_sample_kernel, hg=hg, t=t, nhg=nhg),
        grid=(b, nhg),
        in_specs=[
            pl.BlockSpec((t, w), lambda bb, g: (bb, g)),
            pl.BlockSpec((t, w), lambda bb, g: (bb, nhg + g)),
            pl.BlockSpec((t, w), lambda bb, g: (bb, 2 * nhg + g)),
            pl.BlockSpec((n_pat, hg, t, n_keys), lambda bb, g: (0, g, 0, 0)),
            pl.BlockSpec(memory_space=pl.ANY),
            pl.BlockSpec(memory_space=pl.ANY),
        ],
        out_specs=pl.BlockSpec((t, w), lambda bb, g: (bb, g)),
        out_shape=jax.ShapeDtypeStruct((bt, n_heads * HEAD_DIM), F32),
        scratch_shapes=[pltpu.VMEM((2, p, w), F32), pltpu.VMEM((2, p, w), F32),
                        pltpu.VMEM((n_keys, w), BF16), pltpu.VMEM((n_keys, w), BF16),
                        pltpu.SemaphoreType.DMA((2, 2))],
        compiler_params=_params("arbitrary", "arbitrary"),
        name="attn_sample",
    )(proj, proj, proj, bias, cache_k, cache_v)


def _round_bf16(x):
    return x.astype(BF16).astype(F32)


CONV_ALIGN = 8


def _conv_kernel(a_ref, g_ref, st_ref, w_ref, cb_ref, ng_ref, nb_ref, o_ref, tail_ref, ext_ref, y_ref,
                 extr_ref, sh_ref, *, t, rc, cc):
    ti = pl.program_id(1)
    ch = a_ref.shape[-1]
    n_ext = CONV_HALO + t

    @pl.when(ti == 0)
    def _():
        ext_ref[0:CONV_HALO, :] = st_ref[...]
        extr_ref[n_ext:n_ext + CONV_ALIGN, :] = jnp.zeros((CONV_ALIGN, ch), F32)

    ext_ref[CONV_HALO:n_ext, :] = a_ref[...] * _sigmoid(g_ref[...])
    extr_ref[0:n_ext, :] = _round_bf16(ext_ref[...])
    for s in range(CONV_ALIGN):
        sh_ref[s] = extr_ref[s:s + n_ext, :]
    first = CONV_HALO - (CONV_WIDTH - 1)
    for r0 in range(0, t, rc):
        for c0 in range(0, ch, cc):
            acc = jnp.zeros((rc, cc), F32)
            for j in range(CONV_WIDTH):
                s, base = (first + j) % CONV_ALIGN, (first + j) // CONV_ALIGN * CONV_ALIGN
                acc = acc + w_ref[j:j + 1, c0:c0 + cc] * sh_ref[s, base + r0:base + r0 + rc, c0:c0 + cc]
            y_ref[r0:r0 + rc, c0:c0 + cc] = acc + cb_ref[:, c0:c0 + cc]
    for r0 in range(0, t, rc):
        y = _layer_norm_rows(y_ref[r0:r0 + rc, :], ng_ref[...], nb_ref[...])
        o_ref[r0:r0 + rc, :] = _silu(y).astype(o_ref.dtype)

    @pl.when(ti == pl.num_programs(1) - 1)
    def _():
        tail_ref[...] = ext_ref[t + first:n_ext, :]

    ext_ref[0:CONV_HALO, :] = ext_ref[t:n_ext, :]


def _conv_module(proj, a_blk, state, conv_w, conv_b, ng, nb, *, ch, t, out_dtype, name):
    b, length, _ = proj.shape
    keep = CONV_WIDTH - 1
    st = jnp.pad(state, ((0, 0), (CONV_HALO - keep, 0), (0, 0)))
    rc = min(t, 32)
    vec_spec = pl.BlockSpec((1, ch), lambda bb, i: (0, 0))
    return pl.pallas_call(
        functools.partial(_conv_kernel, t=t, rc=rc, cc=512),
        grid=(b, length // t),
        in_specs=[
            pl.BlockSpec((None, t, ch), lambda bb, i: (bb, i, a_blk)),
            pl.BlockSpec((None, t, ch), lambda bb, i: (bb, i, a_blk + 1)),
            pl.BlockSpec((None, CONV_HALO, ch), lambda bb, i: (bb, 0, 0)),
            pl.BlockSpec((CONV_WIDTH, ch), lambda bb, i: (0, 0)),
            vec_spec, vec_spec, vec_spec,
        ],
        out_specs=[
            pl.BlockSpec((None, t, ch), lambda bb, i: (bb, i, 0)),
            pl.BlockSpec((None, keep, ch), lambda bb, i: (bb, 0, 0)),
        ],
        out_shape=[jax.ShapeDtypeStruct((b, length, ch), out_dtype),
                   jax.ShapeDtypeStruct((b, keep, ch), F32)],
        scratch_shapes=[pltpu.VMEM((CONV_HALO + t, ch), F32), pltpu.VMEM((t, ch), F32),
                        pltpu.VMEM((CONV_HALO + t + CONV_ALIGN, ch), F32),
                        pltpu.VMEM((CONV_ALIGN, CONV_HALO + t, ch), F32)],
        compiler_params=_params("arbitrary", "arbitrary"),
        name=name,
    )(proj, proj, st, conv_w, conv_b.reshape(1, ch), ng.reshape(1, ch), nb.reshape(1, ch))


def _split3(x):
    a1 = x.astype(BF16)
    r1 = x - a1.astype(F32)
    a2 = r1.astype(BF16)
    a3 = (r1 - a2.astype(F32)).astype(BF16)
    return a1, a2, a3


def _hgrn_gates(q, f, lb):
    fg = lb + (1.0 - lb) * _sigmoid(f)
    return _silu(q), 1.0 - fg, jnp.log(fg)


def _hgrn_finish(o, g, ng, dtype):
    normed = o * lax.rsqrt(jnp.mean(o * o, -1, keepdims=True) + EPS) * ng
    return (normed * _silu(g)).astype(dtype)


def _hgrn_chunk_kernel(q_ref, f_ref, i_ref, g_ref, lb_ref, ng_ref, s0_ref, o_ref, sfin_ref, st_ref, *, hg, t, c):
    ti = pl.program_id(2)
    dk = HEAD_DIM

    @pl.when(ti == 0)
    def _():
        for h in range(hg):
            st_ref[h] = s0_ref[h].T

    lb = lb_ref[...]
    ng = ng_ref[...]
    causal = lax.broadcasted_iota(jnp.int32, (c, c), 0) >= lax.broadcasted_iota(jnp.int32, (c, c), 1)
    tri = causal.astype(BF16)
    heads = [slice(h * dk, (h + 1) * dk) for h in range(hg)]
    chunks = [slice(c0, c0 + c) for c0 in range(0, t, c)]
    q_in, decs, intra, incr = [], [], [], []
    for rows in chunks:
        qq, kk, logf = _hgrn_gates(q_ref[rows, :], f_ref[rows, :], lb)
        l1, l2, l3 = _split3(logf)
        cum = _dot(tri, l1) + _dot(tri, l2) + _dot(tri, l3)
        mid = cum[c // 2 - 1:c // 2, :]
        last = cum[c - 1:c, :]
        q_md = (qq * jnp.exp(cum - mid)).astype(BF16)
        k_md = (kk * jnp.exp(mid - cum)).astype(BF16)
        k_end = (kk * jnp.exp(last - cum)).astype(BF16)
        q_in.append((qq * jnp.exp(cum)).astype(BF16))
        decs.append(jnp.exp(last))
        vs = [i_ref[rows, cs].astype(BF16) for cs in heads]
        scores = [jnp.where(causal, _dot_nt(q_md[:, cs], k_md[:, cs]), 0.0).astype(BF16) for cs in heads]
        intra.append([_dot(sc, v) for sc, v in zip(scores, vs)])
        incr.append([_dot_tn(v, k_end[:, cs]) for v, cs in zip(vs, heads)])
    states = [st_ref[h] for h in range(hg)]
    for n, rows in enumerate(chunks):
        for h, cs in enumerate(heads):
            o = intra[n][h] + _dot_nt(q_in[n][:, cs], states[h].astype(BF16))
            states[h] = states[h] * decs[n][:, cs] + incr[n][h]
            o_ref[rows, cs] = _hgrn_finish(o, g_ref[rows, cs], ng, o_ref.dtype)
    for h in range(hg):
        st_ref[h] = states[h]

    @pl.when(ti == pl.num_programs(2) - 1)
    def _():
        for h in range(hg):
            sfin_ref[h] = st_ref[h].T


def _hgrn_step_kernel(q_ref, f_ref, i_ref, g_ref, lb_ref, ng_ref, s0_ref, o_ref, sfin_ref, *, hg, c):
    dk = HEAD_DIM
    hi = lax.Precision.HIGHEST
    ng = ng_ref[...]
    step = lax.broadcasted_iota(jnp.int32, (c, 1), 0)
    qq, kk, logf = _hgrn_gates(q_ref[...], f_ref[...], lb_ref[...])
    acc = [logf[0:1, :]]
    for r in range(1, c):
        acc.append(acc[-1] + logf[r:r + 1, :])
    cum = jnp.concatenate(acc, 0)
    last = cum[c - 1:c, :]
    q_in = qq * jnp.exp(cum)
    k_end = kk * jnp.exp(last - cum)
    dec = jnp.exp(last)
    for h in range(hg):
        cs = slice(h * dk, (h + 1) * dk)
        st = s0_ref[h].T
        v = _round_bf16(i_ref[:, cs])
        o_rows = []
        for r in range(c):
            pair = qq[r:r + 1, cs] * jnp.exp(cum[r:r + 1, cs] - cum[:, cs]) * kk[:, cs]
            sc = jnp.where(step <= r, jnp.sum(pair, -1, keepdims=True), 0.0)
            o_rows.append(jnp.sum(_round_bf16(sc) * v, 0, keepdims=True))
        o = _dot_nt(_round_bf16(q_in[:, cs]), _round_bf16(st), hi) + jnp.concatenate(o_rows, 0)
        sfin_ref[h] = (st * dec[:, cs] + _dot_tn(v, _round_bf16(k_end[:, cs]), hi)).T
        o_ref[:, cs] = _hgrn_finish(o, g_ref[:, cs], ng, o_ref.dtype)


def _hgrn(proj, s0, lb, ng, n_heads, *, hg, t, c, out_dtype, name):
    b, length, _ = proj.shape
    w = hg * HEAD_DIM
    nhg = n_heads // hg
    col = lambda k: pl.BlockSpec((None, t, w), lambda bb, g, i: (bb, i, k * nhg + g))
    st_spec = pl.BlockSpec((None, hg, HEAD_DIM, HEAD_DIM), lambda bb, g, i: (bb, g, 0, 0))
    if t == length == c:
        body, scratch = functools.partial(_hgrn_step_kernel, hg=hg, c=c), []
    else:
        body = functools.partial(_hgrn_chunk_kernel, hg=hg, t=t, c=c)
        scratch = [pltpu.VMEM((hg, HEAD_DIM, HEAD_DIM), F32)]
    return pl.pallas_call(
        body,
        grid=(b, nhg, length // t),
        in_specs=[col(0), col(1), col(2), col(3),
                  pl.BlockSpec((1, w), lambda bb, g, i: (0, g)),
                  pl.BlockSpec((1, HEAD_DIM), lambda bb, g, i: (0, 0)),
                  st_spec],
        out_specs=[pl.BlockSpec((None, t, w), lambda bb, g, i: (bb, i, g)), st_spec],
        out_shape=[jax.ShapeDtypeStruct((b, length, n_heads * HEAD_DIM), out_dtype),
                   jax.ShapeDtypeStruct(s0.shape, F32)],
        scratch_shapes=scratch,
        compiler_params=_params("arbitrary", "arbitrary", "arbitrary"),
        name=name,
    )(proj, proj, proj, proj, lb.reshape(1, -1), ng.reshape(1, HEAD_DIM), s0)


def _top2(vals):
    n = len(vals)
    m1 = functools.reduce(jnp.maximum, vals)
    i1 = jnp.full(m1.shape, n - 1, jnp.int32)
    for k in range(n - 2, -1, -1):
        i1 = jnp.where(vals[k] == m1, k, i1)
    rest = [jnp.where(i1 == k, -1.0, vals[k]) for k in range(n)]
    m2 = functools.reduce(jnp.maximum, rest)
    i2 = jnp.full(m1.shape, n - 1, jnp.int32)
    for k in range(n - 2, -1, -1):
        i2 = jnp.where(rest[k] == m2, k, i2)
    return m1, i1, m2, i2


def _router_kernel(x_ref, w_ref, b_ref, ex_ref, gt_ref):
    logits = _dot_nt(w_ref[...].astype(BF16), x_ref[...]) + b_ref[...]
    e = jnp.exp(logits - jnp.max(logits, 0, keepdims=True))
    p = e / jnp.sum(e, 0, keepdims=True)
    members = [p[k * N_GROUPS:(k + 1) * N_GROUPS, :] for k in range(GROUP_SIZE)]
    m1, i1, m2, i2 = _top2(members)
    score = m1 + m2
    gid = lax.broadcasted_iota(jnp.int32, score.shape, 0)
    best = jnp.max(score, 0, keepdims=True)
    gsel = jnp.min(jnp.where(score == best, gid, N_GROUPS), 0, keepdims=True)
    sel = gid == gsel
    pick_f = lambda a: jnp.sum(jnp.where(sel, a, 0.0), 0, keepdims=True)
    pick_i = lambda a: jnp.sum(jnp.where(sel, a, 0), 0, keepdims=True)
    p1, p2 = pick_f(m1), pick_f(m2)
    ex_ref[0:1, :] = gsel * GROUP_SIZE + pick_i(i1)
    ex_ref[1:2, :] = gsel * GROUP_SIZE + pick_i(i2)
    gt_ref[0:1, :] = p1 / (p1 + p2)
    gt_ref[1:2, :] = p2 / (p1 + p2)


def _router(x, w_t, b_col, *, tm, name):
    n, d = x.shape
    out_spec = pl.BlockSpec((TOP_K, tm), lambda i: (0, i))
    return pl.pallas_call(
        _router_kernel,
        grid=(pl.cdiv(n, tm),),
        in_specs=[pl.BlockSpec((tm, d), lambda i: (i, 0)),
                  pl.BlockSpec((N_EXPERTS, d), lambda i: (0, 0)),
                  pl.BlockSpec((N_EXPERTS, 1), lambda i: (0, 0))],
        out_specs=[out_spec, out_spec],
        out_shape=[jax.ShapeDtypeStruct((TOP_K, n), jnp.int32), jax.ShapeDtypeStruct((TOP_K, n), F32)],
        compiler_params=_params("arbitrary"),
        name=name,
    )(x, w_t, b_col)


MOE_SLOT_ROWS = 576


def _slot_index(s, j, nu_ref, nj):
    used = s < nu_ref[0]
    return jnp.minimum(s, nu_ref[0] - 1), jnp.where(used, j, nj - 1)


def _gate_up_kernel(se_ref, ss_ref, nu_ref, tok_ref, x_hbm, wg_ref, wu_ref, h_ref, xg_ref, xb_ref, sem,
                    *, n_slots, nj):
    s = pl.program_id(0)
    j = pl.program_id(1)
    n_used = nu_ref[0]
    rows = xb_ref.shape[0]
    part = rows // nj

    def start_rows(slot, first, count):
        buf = slot % 2
        base = ss_ref[slot] + first
        for r in range(count):
            tok = tok_ref[base + r]
            pltpu.make_async_copy(x_hbm.at[pl.ds(tok, 1), :], xg_ref.at[buf, pl.ds(first + r, 1), :],
                                  sem.at[buf]).start()

    def wait_buffer(slot):
        buf = slot % 2
        pltpu.make_async_copy(x_hbm.at[pl.ds(0, rows), :], xg_ref.at[buf], sem.at[buf]).wait()

    @pl.when(jnp.logical_and(s == 0, j == 0))
    def _():
        start_rows(0, 0, rows)

    @pl.when(jnp.logical_and(s <= n_used, j == 0))
    def _():
        wait_buffer(s)

    @pl.when(jnp.logical_and(s < n_used, j == 0))
    def _():
        xb_ref[...] = xg_ref[s % 2].astype(BF16)

    @pl.when(s < n_used)
    def _():
        start_rows(s + 1, j * part, part)
        x = xb_ref[...]
        gate = _dot(x, wg_ref[...].astype(BF16))
        up = _dot(x, wu_ref[...].astype(BF16))
        h_ref[...] = (_silu(gate) * up).astype(h_ref.dtype)

    @pl.when(jnp.logical_and(jnp.logical_and(s == n_slots - 1, j == nj - 1), n_used == n_slots))
    def _():
        wait_buffer(n_slots)


def _down_kernel(se_ref, nu_ref, h_ref, wd_ref, y_ref):
    @pl.when(pl.program_id(0) < nu_ref[0])
    def _():
        y_ref[...] = _dot(h_ref[...], wd_ref[...].astype(BF16))


def _expert_ffn(x_rows, slot_expert, slot_start, n_used, tok_sorted, w_gate_up, w_down, layer, *, th, tn):
    n, d = x_rows.shape
    d_exp = w_down.shape[2]
    n_slots = slot_expert.shape[0] - 1
    rows = n_slots * MOE_SLOT_ROWS
    nj = d_exp // th

    def w_map(col0):
        def index(s, j, se, ss, nu, tok):
            se_, je_ = _slot_index(s, j, nu, nj)
            return (layer, se[se_], 0, col0 + je_)
        return index

    def h_map(s, j, se, ss, nu, tok):
        return _slot_index(s, j, nu, nj)

    h = pl.pallas_call(
        functools.partial(_gate_up_kernel, n_slots=n_slots, nj=nj),
        grid_spec=pltpu.PrefetchScalarGridSpec(
            num_scalar_prefetch=4,
            grid=(n_slots, nj),
            in_specs=[
                pl.BlockSpec(memory_space=pl.ANY),
                pl.BlockSpec((None, None, d, th), w_map(0)),
                pl.BlockSpec((None, None, d, th), w_map(nj)),
            ],
            out_specs=pl.BlockSpec((MOE_SLOT_ROWS, th), h_map),
            scratch_shapes=[pltpu.VMEM((2, MOE_SLOT_ROWS, d), F32), pltpu.VMEM((MOE_SLOT_ROWS, d), BF16),
                            pltpu.SemaphoreType.DMA((2,))],
        ),
        out_shape=jax.ShapeDtypeStruct((rows, d_exp), BF16),
        compiler_params=_params("arbitrary", "arbitrary"),
        name=f"moe_gate_up_{layer}",
    )(slot_expert, slot_start, n_used, tok_sorted, x_rows, w_gate_up, w_gate_up)

    nj2 = d // tn

    def wd_map(s, j, se, nu):
        se_, je_ = _slot_index(s, j, nu, nj2)
        return (layer, se[se_], 0, je_)

    return pl.pallas_call(
        _down_kernel,
        grid_spec=pltpu.PrefetchScalarGridSpec(
            num_scalar_prefetch=2,
            grid=(n_slots, nj2),
            in_specs=[
                pl.BlockSpec((MOE_SLOT_ROWS, d_exp), lambda s, j, se, nu: (_slot_index(s, j, nu, nj2)[0], 0)),
                pl.BlockSpec((None, None, d_exp, tn), wd_map),
            ],
            out_specs=pl.BlockSpec((MOE_SLOT_ROWS, tn), lambda s, j, se, nu: _slot_index(s, j, nu, nj2)),
        ),
        out_shape=jax.ShapeDtypeStruct((rows, d), F32),
        compiler_params=_params("arbitrary", "arbitrary"),
        name=f"moe_down_{layer}",
    )(slot_expert, n_used, h, w_down)


def _moe_layer(x_rows, experts, gates, w_gate_up, w_down, layer, ln_g, ln_b, *, tm_ln, split_at=None):
    n, d = x_rows.shape
    n_assign = n * TOP_K
    r = MOE_SLOT_ROWS
    flat_e = experts.reshape(-1)
    order = jnp.argsort(flat_e).astype(jnp.int32)
    e_sorted = flat_e[order]
    tok_sorted = order // TOP_K
    counts = jnp.bincount(flat_e, length=N_EXPERTS).astype(jnp.int32)
    starts = jnp.cumsum(counts) - counts
    slots_per = (counts + r - 1) // r
    slot_end = jnp.cumsum(slots_per)
    slot_first = slot_end - slots_per
    n_slots = -(-n_assign // r) + N_EXPERTS
    sid = jnp.arange(n_slots + 1, dtype=jnp.int32)
    used = sid < slot_end[-1]
    slot_expert = jnp.minimum(jnp.searchsorted(slot_end, sid, side='right'), N_EXPERTS - 1).astype(jnp.int32)
    local = sid - slot_first[slot_expert]
    slot_start = jnp.where(used, starts[slot_expert] + local * r, 0).astype(jnp.int32)
    n_used = slot_end[-1].astype(jnp.int32).reshape(1)
    tok_sorted = jnp.pad(tok_sorted, (0, r))
    rank = jnp.arange(n_assign, dtype=jnp.int32) - starts[e_sorted]
    dest_sorted = (slot_first[e_sorted] + rank // r) * r + rank % r
    dest = dest_sorted[jnp.argsort(order)].reshape(n, TOP_K)
    ys = _expert_ffn(x_rows, slot_expert, slot_start, n_used, tok_sorted, w_gate_up, w_down, layer, th=256, tn=2048)
    return _ln_moe(x_rows, ys, dest, gates, ln_g, ln_b, tm=tm_ln, name=f"ln_moe_{layer}", split_at=split_at)


def kernel(x_prompt, x_sample, cache_win_k, cache_win_v, state_conv, state_hgrn, w_in_ab, rel_bias, conv_w,
           conv_b, conv_norm_g, conv_norm_b, w_out_ab, w_in_c, hgrn_lb, hgrn_norm_g, w_out_c, ln_g, ln_b,
           router_w, router_b, w_gate_up, w_down):
    bp, sp, d = x_prompt.shape
    bs, ts, _ = x_sample.shape
    n_p, n_s = bp * sp, bs * ts
    attn_w = d // 2
    n_ah = attn_w // HEAD_DIM
    conv_ch = d - attn_w
    n_hh = d // HEAD_DIM
    keep = CONV_WIDTH - 1

    lb_soft = jax.nn.softmax(hgrn_lb.astype(F32), axis=0)
    lower_bounds = jnp.cumsum(lb_soft, axis=0) - lb_soft[0]
    perm = (jnp.arange(N_EXPERTS) % N_GROUPS) * GROUP_SIZE + jnp.arange(N_EXPERTS) // N_GROUPS
    router_wt = router_w.T[perm]
    router_bc = router_b.astype(F32)[perm].reshape(N_EXPERTS, 1)

    x_main, x_extra = x_prompt.reshape(n_p, d), x_sample.reshape(n_s, d)
    xb_main, xb_extra, xb_row0 = x_main.astype(BF16), x_extra.astype(BF16), 0
    outs = {}
    for l in range(DEPTH):
        if l % 2 == 0:
            a = l // 2
            n_in = w_in_ab.shape[2]
            proj_p, proj_s = _matmul([xb_main], [xb_extra], w_in_ab, a, n_in, tm=512, tn=512, rows=n_p,
                                     extra_rows=n_s, extra_row0=xb_row0, name=f"ab_in{l}")
            pp3 = proj_p.reshape(bp, sp, n_in)
            attn_p = _attn_prompt(pp3, rel_bias, n_ah, tq=256)
            conv_p, tail_p = _conv_module(pp3, 3 * attn_w // conv_ch, jnp.zeros((bp, keep, conv_ch), F32),
                                          conv_w[a], conv_b[a], conv_norm_g[a], conv_norm_b[a],
                                          ch=conv_ch, t=128, out_dtype=BF16, name=f"conv_p{l}")
            attn_s = _attn_sample(proj_s, cache_win_k[a], cache_win_v[a], rel_bias, ts, hg=4)
            conv_s, tail_s = _conv_module(proj_s.reshape(bs, ts, n_in), 3 * attn_w // conv_ch, state_conv[a],
                                          conv_w[a], conv_b[a], conv_norm_g[a], conv_norm_b[a],
                                          ch=conv_ch, t=ts, out_dtype=F32, name=f"conv_s{l}")
            acts_p = [attn_p.reshape(n_p, attn_w), conv_p.reshape(n_p, conv_ch)]
            acts_s = [attn_s.astype(BF16), conv_s.reshape(n_s, conv_ch).astype(BF16)]
            w_out, lw = w_out_ab, a
            outs.setdefault("wk_p", []).append(proj_p[:, attn_w:2 * attn_w].reshape(bp, sp, n_ah, HEAD_DIM))
            outs.setdefault("wv_p", []).append(proj_p[:, 2 * attn_w:3 * attn_w].reshape(bp, sp, n_ah, HEAD_DIM))
            outs.setdefault("cv_p", []).append(tail_p)
            outs.setdefault("wk_s", []).append(proj_s[:, attn_w:2 * attn_w].reshape(bs, ts, n_ah, HEAD_DIM))
            outs.setdefault("wv_s", []).append(proj_s[:, 2 * attn_w:3 * attn_w].reshape(bs, ts, n_ah, HEAD_DIM))
            outs.setdefault("cv_s", []).append(tail_s)
        else:
            c = l // 2
            n_in = w_in_c.shape[2]
            proj_p, proj_s = _matmul([xb_main], [xb_extra], w_in_c, c, n_in, tm=512, tn=512, rows=n_p,
                                     extra_rows=n_s, extra_row0=xb_row0, name=f"c_in{l}")
            o_p, h_p = _hgrn(proj_p.reshape(bp, sp, n_in), jnp.zeros((bp, n_hh, HEAD_DIM, HEAD_DIM), F32),
                             lower_bounds[l], hgrn_norm_g[c], n_hh, hg=4, t=256, c=64,
                             out_dtype=BF16, name=f"hgrn_p{l}")
            o_s, h_s = _hgrn(proj_s.reshape(bs, ts, n_in), state_hgrn[c], lower_bounds[l], hgrn_norm_g[c],
                             n_hh, hg=4, t=ts, c=ts, out_dtype=F32, name=f"hgrn_s{l}")
            acts_p = [o_p.reshape(n_p, d)]
            acts_s = [o_s.reshape(n_s, d).astype(BF16)]
            w_out, lw = w_out_c, c
            outs.setdefault("hs_p", []).append(h_p)
            outs.setdefault("hs_s", []).append(h_s)
        m_p, m_s = _matmul(acts_p, acts_s, w_out, lw, d, tm=512, tn=512, rows=n_p, extra_rows=n_s, name=f"mix_out{l}")
        x, xb = _ln_res(x_main, x_extra, m_p, m_s, ln_g[l, 0], ln_b[l, 0], tm=256, name=f"ln_mix{l}")
        experts, gates = _router(xb, router_wt, router_bc, tm=512, name=f"router{l}")
        x, xb = _moe_layer(x, experts.T, gates.T, w_gate_up, w_down, l, ln_g[l, 1], ln_b[l, 1], tm_ln=128,
                           split_at=n_p if l == DEPTH - 1 else None)
        x_main, x_extra, xb_main, xb_extra, xb_row0 = x, x, xb, xb, n_p
    y_p, y_s = x, xb
    stack = lambda k: jnp.stack(outs[k])
    return (y_p.reshape(bp, sp, d), y_s.reshape(bs, ts, d), stack("wk_p"), stack("wv_p"), stack("cv_p"),
            stack("hs_p"), stack("wk_s"), stack("wv_s"), stack("cv_s"), stack("hs_s"))
```
